```python
import math
import jax, jax.numpy as jnp
from jax import lax
import numpy as np

D_MODEL = 1024
BATCH = 16
SEQ = 2048
DEPTH = 1

GRID_W = 64
CTX_LEN = 256

N_HEADS = 8
N_KV_HEADS = 2
HEAD_DIM = 128
ATTN_DIM = N_HEADS * HEAD_DIM
KV_DIM = N_KV_HEADS * HEAD_DIM
Q_BLOCK = 128
ROPE_THETA = 10000.0
ROPE_AXIS_DIM = HEAD_DIM // 2

SSM_DIM = 512
SSM_GROUP = 16
N_SSM_GROUPS = SSM_DIM // SSM_GROUP
SSM_STATE = 64
N_DIRECTIONS = 2
DT_MIN = 1e-3
DT_MAX = 1e-1

N_BRANCHES = 2
Q_END = ATTN_DIM
K_END = Q_END + KV_DIM
V_END = K_END + KV_DIM
U_END = V_END + SSM_DIM
D_IN = U_END + N_BRANCHES * D_MODEL

D_FF = 2816
CONV_W = 3

N_MOD = 6
EPS = 1e-6

kernel_name = 'hybrid_s5_gqa_convffn_dit_prefix'


def _rmsnorm(x, w):
    xf = x.astype(jnp.float32)
    y = xf * lax.rsqrt(jnp.mean(xf * xf, axis=-1, keepdims=True) + EPS)
    return (y * w.astype(jnp.float32)).astype(x.dtype)


def _modulate(h, shift, scale):
    return h * (1.0 + scale) + shift


def _heads(t, n_heads):
    b, l, _ = t.shape
    return t.reshape(b, l, n_heads, HEAD_DIM).transpose(0, 2, 1, 3)


def _tokens(t):
    b, n, l, dh = t.shape
    return t.transpose(0, 2, 1, 3).reshape(b, l, n * dh)


def _rope_tables(rows, cols):
    inv_freq = ROPE_THETA ** (-jnp.arange(0, ROPE_AXIS_DIM, 2, dtype=jnp.float32) / ROPE_AXIS_DIM)
    ang = jnp.concatenate([rows[:, None] * inv_freq, cols[:, None] * inv_freq], axis=-1)
    return jnp.cos(ang), jnp.sin(ang)


def _apply_rope(t, cos, sin):
    tf = t.astype(jnp.float32).reshape(*t.shape[:-1], HEAD_DIM // 2, 2)
    t0, t1 = tf[..., 0], tf[..., 1]
    out = jnp.stack([t0 * cos - t1 * sin, t0 * sin + t1 * cos], axis=-1)
    return out.reshape(t.shape).astype(t.dtype)


def _gqa_sweep(q, k, v):
    b, h, lq, dh = q.shape
    rep = h // N_KV_HEADS
    nblk = lq // Q_BLOCK
    qb = jnp.moveaxis(q.reshape(b, N_KV_HEADS, rep, nblk, Q_BLOCK, dh), 3, 0)
    scale = 1.0 / math.sqrt(HEAD_DIM)

    def one_block(q_blk):
        s = jnp.einsum('bkgqd,bktd->bkgqt', q_blk, k).astype(jnp.float32) * scale
        p = jax.nn.softmax(s, axis=-1).astype(v.dtype)
        return jnp.einsum('bkgqt,bktd->bkgqd', p, v)

    o = lax.map(one_block, qb)
    return jnp.moveaxis(o, 0, 3).reshape(b, h, lq, dh)


def _zoh(lam_re, lam_im, log_dt, b_re, b_im):
    lam_re = lam_re.astype(jnp.float32)
    lam_im = lam_im.astype(jnp.float32)
    b_re = b_re.astype(jnp.float32)
    b_im = b_im.astype(jnp.float32)
    dt = jnp.exp(log_dt.astype(jnp.float32))[:, None]
    mag = jnp.exp(lam_re * dt)
    ang = lam_im * dt
    abar_re = mag * jnp.cos(ang)
    abar_im = mag * jnp.sin(ang)
    den = lam_re * lam_re + lam_im * lam_im
    nr = abar_re - 1.0
    ni = abar_im
    f_re = (nr * lam_re + ni * lam_im) / den
    f_im = (ni * lam_re - nr * lam_im) / den
    bbar_re = f_re[..., None] * b_re - f_im[..., None] * b_im
    bbar_im = f_re[..., None] * b_im + f_im[..., None] * b_re
    return abar_re, abar_im, bbar_re, bbar_im


def _ssm_combine(first, second):
    a_re, a_im, x_re, x_im = first
    b_re, b_im, y_re, y_im = second
    return (a_re * b_re - a_im * b_im,
            a_re * b_im + a_im * b_re,
            b_re * x_re - b_im * x_im + y_re,
            b_re * x_im + b_im * x_re + y_im)


def _s5_direction(u_ctx, u_lat, lam_re, lam_im, log_dt, b_re, b_im, reverse):
    abar_re, abar_im, bbar_re, bbar_im = _zoh(lam_re, lam_im, log_dt, b_re, b_im)

    def drive(u):
        return (jnp.einsum('blgp,gnp->blgn', u, bbar_re),
                jnp.einsum('blgp,gnp->blgn', u, bbar_im))

    def scan(bu_re, bu_im):
        l = bu_re.shape[1]
        a_re = jnp.broadcast_to(abar_re, (1, l) + abar_re.shape)
        a_im = jnp.broadcast_to(abar_im, (1, l) + abar_im.shape)
        _, _, s_re, s_im = lax.associative_scan(
            _ssm_combine, (a_re, a_im, bu_re, bu_im), reverse=reverse, axis=1)
        return s_re, s_im

    ctx_end = 0 if reverse else -1
    lat_start = -1 if reverse else 0
    sc_re, sc_im = scan(*drive(u_ctx))
    s0_re, s0_im = sc_re[:, ctx_end], sc_im[:, ctx_end]
    bl_re, bl_im = drive(u_lat)
    bl_re = bl_re.at[:, lat_start].add(abar_re * s0_re - abar_im * s0_im)
    bl_im = bl_im.at[:, lat_start].add(abar_re * s0_im + abar_im * s0_re)
    sl_re, sl_im = scan(bl_re, bl_im)
    return sl_re, sl_im, sc_re, sc_im


def _s5_readout(s_re, s_im, c_re, c_im):
    y = (jnp.einsum('blgn,gpn->blgp', s_re, c_re.astype(jnp.float32))
         - jnp.einsum('blgn,gpn->blgp', s_im, c_im.astype(jnp.float32)))
    return y.reshape(y.shape[0], y.shape[1], SSM_DIM)


def _s5_bidirectional(u_lat, u_ctx, lam_re, lam_im, log_dt, b_re, b_im, c_re, c_im, d_skip, with_ctx_out):
    b, l, _ = u_lat.shape
    lc = u_ctx.shape[1]
    ul = u_lat.astype(jnp.float32).reshape(b, l, N_SSM_GROUPS, SSM_GROUP)
    uc = u_ctx.astype(jnp.float32).reshape(b, lc, N_SSM_GROUPS, SSM_GROUP)
    d = d_skip.astype(jnp.float32)
    y_lat = u_lat.astype(jnp.float32) * d
    ctx_terms = [u_ctx.astype(jnp.float32) * d] if with_ctx_out else []
    for direction in range(N_DIRECTIONS):
        sl_re, sl_im, sc_re, sc_im = _s5_direction(
            uc, ul, lam_re[direction], lam_im[direction], log_dt[direction],
            b_re[direction], b_im[direction], reverse=(direction == 1))
        y_lat = y_lat + _s5_readout(sl_re, sl_im, c_re[direction], c_im[direction])
        if with_ctx_out:
            ctx_terms.append(_s5_readout(sc_re, sc_im, c_re[direction], c_im[direction]))
    y_ctx = sum(ctx_terms).astype(u_ctx.dtype) if with_ctx_out else None
    return y_lat.astype(u_lat.dtype), y_ctx


def _merge_branches(attn_tok, ssm_tok, g, w_attn_br, w_glu, w_out):
    p_attn = attn_tok @ w_attn_br
    glu_a, glu_b = jnp.split(jax.nn.gelu(ssm_tok) @ w_glu, 2, axis=-1)
    p_ssm = glu_a * jax.nn.sigmoid(glu_b)
    g_attn, g_ssm = jnp.split(g, N_BRANCHES, axis=-1)
    return (jax.nn.sigmoid(g_attn) * p_attn + jax.nn.sigmoid(g_ssm) * p_ssm) @ w_out


def _conv_ffn(h, w_up, conv_w, conv_b, w_down):
    z = h @ w_up
    l = z.shape[1]
    pad = CONV_W // 2
    zp = jnp.pad(z, ((0, 0), (pad, pad), (0, 0)))
    z = sum(zp[:, j:j + l] * conv_w[j] for j in range(CONV_W)) + conv_b
    val, gate = jnp.split(z, 2, axis=-1)
    return (jax.nn.silu(gate) * val) @ w_down


def setup_inputs(seed: int = 0) -> dict:
    key = jax.random.key(seed)
    ks = jax.random.split(key, 32)
    f32 = jnp.float32
    G, N, P = N_SSM_GROUPS, SSM_STATE, SSM_GROUP

    def nrm(k, shape, scale):
        return jax.random.normal(k, shape, f32) * scale

    n_idx = jnp.arange(N, dtype=f32)
    lam_re = -0.5 + nrm(ks[10], (DEPTH, N_DIRECTIONS, G, N), 0.01)
    lam_im = math.pi * n_idx + nrm(ks[11], (DEPTH, N_DIRECTIONS, G, N), 0.01)
    log_dt = jax.random.uniform(ks[12], (DEPTH, N_DIRECTIONS, G), f32,
                                math.log(DT_MIN), math.log(DT_MAX))
    return {
        'x': nrm(ks[0], (BATCH, SEQ, D_MODEL), 1.0),
        'c': nrm(ks[1], (BATCH, D_MODEL), 1.0),
        'ctx': nrm(ks[2], (BATCH, CTX_LEN, D_MODEL), 1.0),
        'c_ctx': nrm(ks[3], (D_MODEL,), 1.0),
        'w_mod': nrm(ks[4], (DEPTH, D_MODEL, N_MOD * D_MODEL), 0.5 * D_MODEL ** -0.5),
        'b_mod': nrm(ks[5], (DEPTH, N_MOD * D_MODEL), 0.02),
        'norm1_w': 1.0 + nrm(ks[6], (DEPTH, D_MODEL), 0.02),
        'norm2_w': 1.0 + nrm(ks[7], (DEPTH, D_MODEL), 0.02),
        'w_in': nrm(ks[8], (DEPTH, D_MODEL, D_IN), D_MODEL ** -0.5),
        'q_norm_w': 1.0 + nrm(ks[9], (DEPTH, HEAD_DIM), 0.02),
        'k_norm_w': 1.0 + nrm(ks[13], (DEPTH, HEAD_DIM), 0.02),
        'w_attn_br': nrm(ks[14], (DEPTH, ATTN_DIM, D_MODEL), ATTN_DIM ** -0.5),
        'ssm_lambda_re': lam_re,
        'ssm_lambda_im': lam_im,
        'ssm_log_dt': log_dt,
        'ssm_b_re': nrm(ks[15], (DEPTH, N_DIRECTIONS, G, N, P), (2 * P) ** -0.5),
        'ssm_b_im': nrm(ks[16], (DEPTH, N_DIRECTIONS, G, N, P), (2 * P) ** -0.5),
        'ssm_c_re': nrm(ks[17], (DEPTH, N_DIRECTIONS, G, P, N), (2 * N) ** -0.5),
        'ssm_c_im': nrm(ks[18], (DEPTH, N_DIRECTIONS, G, P, N), (2 * N) ** -0.5),
        'ssm_d': nrm(ks[19], (DEPTH, SSM_DIM), 1.0),
        'w_glu': nrm(ks[20], (DEPTH, SSM_DIM, 2 * D_MODEL), SSM_DIM ** -0.5),
        'w_out': nrm(ks[21], (DEPTH, D_MODEL, D_MODEL), D_MODEL ** -0.5),
        'w_up': nrm(ks[22], (DEPTH, D_MODEL, 2 * D_FF), D_MODEL ** -0.5),
        'conv_w': nrm(ks[23], (DEPTH, CONV_W, 2 * D_FF), CONV_W ** -0.5),
        'conv_b': nrm(ks[24], (DEPTH, 2 * D_FF), 0.02),
        'w_down': nrm(ks[25], (DEPTH, D_FF, D_MODEL), D_FF ** -0.5),
        'final_norm_w': 1.0 + nrm(ks[26], (D_MODEL,), 0.02),
    }


def reference(x, c, ctx, c_ctx, w_mod, b_mod, norm1_w, norm2_w, w_in, q_norm_w, k_norm_w,
              w_attn_br, ssm_lambda_re, ssm_lambda_im, ssm_log_dt, ssm_b_re, ssm_b_im,
              ssm_c_re, ssm_c_im, ssm_d, w_glu, w_out, w_up, conv_w, conv_b, w_down,
              final_norm_w):
    l = x.shape[1]
    ROWS = l // GRID_W
    rows = jnp.repeat(jnp.arange(ROWS, dtype=jnp.float32), GRID_W)
    cols = jnp.tile(jnp.arange(GRID_W, dtype=jnp.float32), ROWS)
    cos, sin = _rope_tables(rows, cols)

    for layer in range(DEPTH):
        update_ctx = layer < DEPTH - 1
        mod = jax.nn.silu(c) @ w_mod[layer] + b_mod[layer]
        mod_c = jax.nn.silu(c_ctx) @ w_mod[layer] + b_mod[layer]
        sh1, sc1, g1, sh2, sc2, g2 = jnp.split(mod[:, None, :], N_MOD, axis=-1)
        csh1, csc1, cg1, csh2, csc2, cg2 = jnp.split(mod_c, N_MOD, axis=-1)
        w_in_l = w_in[layer]

        h = _modulate(_rmsnorm(x, norm1_w[layer]), sh1, sc1)
        hc = _modulate(_rmsnorm(ctx, norm1_w[layer]), csh1, csc1)
        q, k, v, u, g = jnp.split(h @ w_in_l, (Q_END, K_END, V_END, U_END), axis=-1)
        kc, vc, uc = jnp.split(hc @ w_in_l[:, Q_END:U_END], (KV_DIM, 2 * KV_DIM), axis=-1)

        q = _apply_rope(_rmsnorm(_heads(q, N_HEADS), q_norm_w[layer]), cos, sin)
        k = _apply_rope(_rmsnorm(_heads(k, N_KV_HEADS), k_norm_w[layer]), cos, sin)
        v = _heads(v, N_KV_HEADS)
        kc = _rmsnorm(_heads(kc, N_KV_HEADS), k_norm_w[layer])
        vc = _heads(vc, N_KV_HEADS)
        attn_tok = _tokens(_gqa_sweep(q, jnp.concatenate([kc, k], axis=2),
                                      jnp.concatenate([vc, v], axis=2)))

        y_ssm, y_ssm_c = _s5_bidirectional(
            u, uc, ssm_lambda_re[layer], ssm_lambda_im[layer], ssm_log_dt[layer],
            ssm_b_re[layer], ssm_b_im[layer], ssm_c_re[layer], ssm_c_im[layer], ssm_d[layer],
            with_ctx_out=update_ctx)

        x_mix = _merge_branches(attn_tok, y_ssm, g, w_attn_br[layer], w_glu[layer], w_out[layer])
        x = x + g1 * x_mix

        h2 = _modulate(_rmsnorm(x, norm2_w[layer]), sh2, sc2)
        x = x + g2 * _conv_ffn(h2, w_up[layer], conv_w[layer], conv_b[layer], w_down[layer])

        if update_ctx:
            qc = _rmsnorm(_heads(hc @ w_in_l[:, :Q_END], N_HEADS), q_norm_w[layer])
            gc = hc @ w_in_l[:, U_END:]
            attn_c = _tokens(_gqa_sweep(qc, kc, vc))
            ctx = ctx + cg1 * _merge_branches(attn_c, y_ssm_c, gc, w_attn_br[layer],
                                              w_glu[layer], w_out[layer])
            hc2 = _modulate(_rmsnorm(ctx, norm2_w[layer]), csh2, csc2)
            ctx = ctx + cg2 * _conv_ffn(hc2, w_up[layer], conv_w[layer], conv_b[layer], w_down[layer])

    return _rmsnorm(x, final_norm_w)
```

```python
import functools
import math

import jax
import jax.numpy as jnp
from jax import lax
from jax.experimental import pallas as pl
from jax.experimental.pallas import tpu as pltpu

F32 = jnp.float32
BF16 = jnp.bfloat16

D_MODEL = 1024
GRID_W = 64
N_HEADS = 8
N_KV_HEADS = 2
HEAD_DIM = 128
HEADS_PER_KV = N_HEADS // N_KV_HEADS
ATTN_DIM = N_HEADS * HEAD_DIM
KV_DIM = N_KV_HEADS * HEAD_DIM
ROPE_THETA = 10000.0
ROPE_AXIS_DIM = HEAD_DIM // 2
SSM_DIM = 512
SSM_GROUP = 16
N_SSM_GROUPS = SSM_DIM // SSM_GROUP
SSM_STATE = 64
Q_END = ATTN_DIM
K_END = Q_END + KV_DIM
V_END = K_END + KV_DIM
U_END = V_END + SSM_DIM
D_FF = 2816
N_MOD = 6
EPS = 1e-6

CHUNK = 16
CHUNK_W = CHUNK * SSM_GROUP
PAIR_W = 2 * SSM_STATE

FF_CHUNK = 256
N_FF_CHUNKS = D_FF // FF_CHUNK
HALO = 16

VMEM_LIMIT = 56 * 1024 * 1024

TM_PROJ = 512
TQ_ATTN = 512
TM_MERGE = 512
TM_FFN = 512


def _sigmoid(x):
    return 1.0 / (1.0 + jnp.exp(-x))


def _gelu_tanh(x):
    return 0.5 * x * (1.0 + jnp.tanh(math.sqrt(2.0 / math.pi) * (x + 0.044715 * (x * x * x))))


def _rms(x):
    return x * lax.rsqrt(jnp.mean(x * x, axis=-1, keepdims=True) + EPS)


def _bdot(a, b):
    return jnp.dot(a, b, preferred_element_type=F32)


def _mod_kernel(c_ref, w_ref, b_ref, o_ref):
    c = c_ref[...]
    a = c * _sigmoid(c)
    o_ref[...] = jnp.dot(a, w_ref[...], precision=lax.Precision.HIGHEST,
                         preferred_element_type=F32) + b_ref[...]


def _modulation(c_rows, w_mod, b_mod):
    rows = c_rows.shape[0]
    n = w_mod.shape[1]
    tn = 1536
    return pl.pallas_call(
        _mod_kernel,
        grid=(n // tn,),
        in_specs=[pl.BlockSpec((rows, D_MODEL), lambda j: (0, 0)),
                  pl.BlockSpec((D_MODEL, tn), lambda j: (0, j)),
                  pl.BlockSpec((1, tn), lambda j: (0, j))],
        out_specs=pl.BlockSpec((rows, tn), lambda j: (0, j)),
        out_shape=jax.ShapeDtypeStruct((rows, n), F32),
        compiler_params=pltpu.CompilerParams(dimension_semantics=("arbitrary",),
                                             vmem_limit_bytes=VMEM_LIMIT),
        name="modulation",
    )(c_rows, w_mod, b_mod.reshape(1, n))


def _norm_rope_store(t, nw, cos, sin, out_ref, col):
    t = _rms(t) * nw
    if cos is not None:
        t = t * cos + pltpu.roll(t, HEAD_DIM // 2, 1) * sin
    out_ref[0, :, col:col + HEAD_DIM] = t.astype(out_ref.dtype)


def _inproj_latent_kernel(x_ref, sh_ref, sc_ref, n1_ref, wq_ref, wk_ref, wv_ref, wu_ref, wg_ref,
                          qn_ref, kn_ref, cos_ref, sin_ref,
                          q_ref, k_ref, v_ref, u_ref, g_ref):
    h = _rms(x_ref[0]) * n1_ref[...]
    hb = (h * (1.0 + sc_ref[0]) + sh_ref[0]).astype(BF16)
    cos = cos_ref[...]
    sin = sin_ref[...]
    q = _bdot(hb, wq_ref[...])
    for hd in range(N_HEADS):
        _norm_rope_store(q[:, hd * HEAD_DIM:(hd + 1) * HEAD_DIM], qn_ref[...], cos, sin,
                         q_ref, hd * HEAD_DIM)
    k = _bdot(hb, wk_ref[...])
    for hd in range(N_KV_HEADS):
        _norm_rope_store(k[:, hd * HEAD_DIM:(hd + 1) * HEAD_DIM], kn_ref[...], cos, sin,
                         k_ref, hd * HEAD_DIM)
    v_ref[0] = _bdot(hb, wv_ref[...]).astype(v_ref.dtype)
    u_ref[0] = _bdot(hb, wu_ref[...])
    g_ref[0] = _bdot(hb, wg_ref[...])


def _inproj_ctx_kernel(x_ref, sh_ref, sc_ref, n1_ref, wk_ref, wv_ref, wu_ref, kn_ref,
                       k_ref, v_ref, u_ref):
    h = _rms(x_ref[0]) * n1_ref[...]
    hb = (h * (1.0 + sc_ref[...]) + sh_ref[...]).astype(BF16)
    k = _bdot(hb, wk_ref[...])
    for hd in range(N_KV_HEADS):
        _norm_rope_store(k[:, hd * HEAD_DIM:(hd + 1) * HEAD_DIM], kn_ref[...], None, None,
                         k_ref, hd * HEAD_DIM)
    v_ref[0] = _bdot(hb, wv_ref[...]).astype(v_ref.dtype)
    u_ref[0] = _bdot(hb, wu_ref[...])


def _const_spec(shape):
    nd = len(shape)
    return pl.BlockSpec(shape, lambda *_: (0,) * nd)


def _inproj_latent(x, sh1, sc1, n1, wq, wk, wv, wu, wg, qn, kn, cos2, sin2):
    b, l, _ = x.shape
    tm = TM_PROJ
    tok = lambda w: pl.BlockSpec((1, tm, w), lambda i, j: (i, j, 0))
    per_b = pl.BlockSpec((1, 1, D_MODEL), lambda i, j: (i, 0, 0))
    rope = pl.BlockSpec((tm, HEAD_DIM), lambda i, j: (j, 0))
    return pl.pallas_call(
        _inproj_latent_kernel,
        grid=(b, l // tm),
        in_specs=[tok(D_MODEL), per_b, per_b, _const_spec((1, D_MODEL)),
                  _const_spec(wq.shape), _const_spec(wk.shape), _const_spec(wv.shape),
                  _const_spec(wu.shape), _const_spec(wg.shape),
                  _const_spec((1, HEAD_DIM)), _const_spec((1, HEAD_DIM)), rope, rope],
        out_specs=[tok(ATTN_DIM), tok(KV_DIM), tok(KV_DIM), tok(SSM_DIM), tok(2 * D_MODEL)],
        out_shape=[jax.ShapeDtypeStruct((b, l, ATTN_DIM), BF16),
                   jax.ShapeDtypeStruct((b, l, KV_DIM), BF16),
                   jax.ShapeDtypeStruct((b, l, KV_DIM), BF16),
                   jax.ShapeDtypeStruct((b, l, SSM_DIM), F32),
                   jax.ShapeDtypeStruct((b, l, 2 * D_MODEL), F32)],
        compiler_params=pltpu.CompilerParams(dimension_semantics=("arbitrary", "arbitrary"),
                                             vmem_limit_bytes=VMEM_LIMIT),
        name="inproj_latent",
    )(x, sh1, sc1, n1, wq, wk, wv, wu, wg, qn, kn, cos2, sin2)


def _inproj_ctx(ctx, csh1, csc1, n1, wk, wv, wu, kn):
    b, lc, _ = ctx.shape
    tok = lambda w: pl.BlockSpec((1, lc, w), lambda i: (i, 0, 0))
    return pl.pallas_call(
        _inproj_ctx_kernel,
        grid=(b,),
        in_specs=[tok(D_MODEL), _const_spec((1, D_MODEL)), _const_spec((1, D_MODEL)),
                  _const_spec((1, D_MODEL)), _const_spec(wk.shape), _const_spec(wv.shape),
                  _const_spec(wu.shape), _const_spec((1, HEAD_DIM))],
        out_specs=[tok(KV_DIM), tok(KV_DIM), tok(SSM_DIM)],
        out_shape=[jax.ShapeDtypeStruct((b, lc, KV_DIM), BF16),
                   jax.ShapeDtypeStruct((b, lc, KV_DIM), BF16),
                   jax.ShapeDtypeStruct((b, lc, SSM_DIM), F32)],
        compiler_params=pltpu.CompilerParams(dimension_semantics=("arbitrary",),
                                             vmem_limit_bytes=VMEM_LIMIT),
        name="inproj_ctx",
    )(ctx, csh1, csc1, n1, wk, wv, wu, kn)


def _attn_kernel(q_ref, kc_ref, k_ref, vc_ref, v_ref, o_ref):
    kc = kc_ref[0]
    k = k_ref[0]
    vc = vc_ref[0]
    v = v_ref[0]
    nt = (((1,), (1,)), ((), ()))
    for r in range(HEADS_PER_KV):
        q = q_ref[0, :, r * HEAD_DIM:(r + 1) * HEAD_DIM]
        s_c = lax.dot_general(q, kc, nt, preferred_element_type=F32)
        s_l = lax.dot_general(q, k, nt, preferred_element_type=F32)
        m = jnp.maximum(jnp.max(s_c, axis=-1, keepdims=True), jnp.max(s_l, axis=-1, keepdims=True))
        p_c = jnp.exp(s_c - m)
        p_l = jnp.exp(s_l - m)
        denom = jnp.sum(p_c, axis=-1, keepdims=True) + jnp.sum(p_l, axis=-1, keepdims=True)
        o = _bdot(p_c.astype(BF16), vc) + _bdot(p_l.astype(BF16), v)
        o_ref[0, :, r * HEAD_DIM:(r + 1) * HEAD_DIM] = (o / denom).astype(o_ref.dtype)


def _attention(q, kc, k, vc, v):
    b, l, _ = q.shape
    lc = kc.shape[1]
    tq = TQ_ATTN
    qspec = pl.BlockSpec((1, tq, HEADS_PER_KV * HEAD_DIM), lambda i, h, j: (i, j, h))
    kv = lambda n: pl.BlockSpec((1, n, HEAD_DIM), lambda i, h, j: (i, 0, h))
    return pl.pallas_call(
        _attn_kernel,
        grid=(b, N_KV_HEADS, l // tq),
        in_specs=[qspec, kv(lc), kv(l), kv(lc), kv(l)],
        out_specs=qspec,
        out_shape=jax.ShapeDtypeStruct((b, l, ATTN_DIM), BF16),
        compiler_params=pltpu.CompilerParams(
            dimension_semantics=("arbitrary", "arbitrary", "arbitrary"),
            vmem_limit_bytes=VMEM_LIMIT),
        name="attention",
    )(q, kc, k, vc, v)


def _s5_tables(lam_re, lam_im, log_dt, b_re, b_im, c_re, c_im, d_skip):
    hp = lax.Precision.HIGHEST
    g_, n_, p_ = N_SSM_GROUPS, SSM_STATE, SSM_GROUP
    dt = jnp.exp(log_dt)[..., None]
    mag = jnp.exp(lam_re * dt)
    ang = lam_im * dt
    a_re = mag * jnp.cos(ang)
    a_im = mag * jnp.sin(ang)
    den = lam_re * lam_re + lam_im * lam_im
    nr = a_re - 1.0
    ni = a_im
    f_re = (nr * lam_re + ni * lam_im) / den
    f_im = (ni * lam_re - nr * lam_im) / den
    bb_re = f_re[..., None] * b_re - f_im[..., None] * b_im
    bb_im = f_re[..., None] * b_im + f_im[..., None] * b_re

    pw_re = [jnp.ones_like(a_re)]
    pw_im = [jnp.zeros_like(a_im)]
    for _ in range(CHUNK):
        pr, pi = pw_re[-1], pw_im[-1]
        pw_re.append(pr * a_re - pi * a_im)
        pw_im.append(pr * a_im + pi * a_re)
    pw_re = jnp.stack(pw_re)
    pw_im = jnp.stack(pw_im)

    pb_re = pw_re[..., None] * bb_re - pw_im[..., None] * bb_im
    pb_im = pw_re[..., None] * bb_im + pw_im[..., None] * bb_re
    cp_re = c_re * pw_re[:, :, :, None, :] - c_im * pw_im[:, :, :, None, :]
    cp_im = c_re * pw_im[:, :, :, None, :] + c_im * pw_re[:, :, :, None, :]

    kern = (jnp.einsum('dgpn,tdgnq->tdgpq', c_re, pb_re[:CHUNK], precision=hp)
            - jnp.einsum('dgpn,tdgnq->tdgpq', c_im, pb_im[:CHUNK], precision=hp))

    src = jnp.arange(CHUNK)[:, None]
    tgt = jnp.arange(CHUNK)[None, :]
    m_dirs, e_dirs, f_dirs = [], [], []
    for d in range(2):
        lag = (tgt - src) if d == 0 else (src - tgt)
        mk = kern[:, d][jnp.clip(lag, 0, CHUNK - 1)]
        mk = jnp.where((lag >= 0)[:, :, None, None, None], mk, 0.0)
        m_dirs.append(mk.transpose(2, 0, 4, 1, 3).reshape(g_, CHUNK_W, CHUNK_W))
        tau_in = (CHUNK - 1 - jnp.arange(CHUNK)) if d == 0 else jnp.arange(CHUNK)
        e_re = pb_re[tau_in, d].transpose(1, 0, 3, 2).reshape(g_, CHUNK_W, n_)
        e_im = pb_im[tau_in, d].transpose(1, 0, 3, 2).reshape(g_, CHUNK_W, n_)
        e_dirs.append((e_re, e_im))
        tau_out = (jnp.arange(CHUNK) + 1) if d == 0 else (CHUNK - jnp.arange(CHUNK))
        fr = cp_re[tau_out, d].transpose(1, 3, 0, 2).reshape(g_, n_, CHUNK_W)
        fi = -cp_im[tau_out, d].transpose(1, 3, 0, 2).reshape(g_, n_, CHUNK_W)
        f_dirs.append((fr, fi))

    def pair_cols(e):
        z = jnp.zeros_like(e)
        even = jnp.concatenate([e, z], axis=-1)
        odd = jnp.concatenate([z, e], axis=-1)
        return jnp.where((jnp.arange(g_) % 2 == 0)[:, None, None], even, odd)

    def pair_rows(f):
        z = jnp.zeros_like(f)
        even = jnp.concatenate([f, z], axis=1)
        odd = jnp.concatenate([z, f], axis=1)
        return jnp.where((jnp.arange(g_) % 2 == 0)[:, None, None], even, odd)

    m_tab = jnp.stack(m_dirs).astype(BF16)
    er_tab = jnp.stack([pair_cols(e[0]) for e in e_dirs]).astype(BF16)
    ei_tab = jnp.stack([pair_cols(e[1]) for e in e_dirs]).astype(BF16)
    fr_tab = jnp.stack([pair_rows(f[0]) for f in f_dirs]).astype(BF16)
    fi_tab = jnp.stack([pair_rows(f[1]) for f in f_dirs]).astype(BF16)
    a_tab = jnp.stack([pw_re[CHUNK].reshape(2, g_ // 2, 1, PAIR_W),
                       pw_im[CHUNK].reshape(2, g_ // 2, 1, PAIR_W)], axis=1)
    d_tab = jnp.tile(d_skip.reshape(g_, 1, p_), (1, CHUNK, 1)).reshape(g_, 1, CHUNK_W)
    return m_tab, er_tab, ei_tab, fr_tab, fi_tab, a_tab, d_tab


def _s5_kernel(u_ref, m_ref, er_ref, ei_ref, fr_ref, fi_ref, a_ref, d_ref, y_ref,
               cre_ref, cim_ref, sre_ref, sim_ref, *, n_batch, n_ctx_chunks, n_lat_chunks):
    rows_ctx = n_ctx_chunks * n_batch
    u = [u_ref[0], u_ref[1]]
    ub = [t.astype(BF16) for t in u]
    y = [u[gi][rows_ctx:] * d_ref[gi] for gi in range(2)]
    for d in range(2):
        cre_ref[...] = _bdot(ub[0], er_ref[d, 0]) + _bdot(ub[1], er_ref[d, 1])
        cim_ref[...] = _bdot(ub[0], ei_ref[d, 0]) + _bdot(ub[1], ei_ref[d, 1])
        a_re = jnp.broadcast_to(a_ref[d, 0, 0], (n_batch, PAIR_W))
        a_im = jnp.broadcast_to(a_ref[d, 1, 0], (n_batch, PAIR_W))

        def step(s, row):
            s_re, s_im = s
            c_re = cre_ref[pl.ds(row, n_batch), :]
            c_im = cim_ref[pl.ds(row, n_batch), :]
            return (s_re * a_re - s_im * a_im + c_re, s_re * a_im + s_im * a_re + c_im)

        def ctx_step(j, s):
            kc = j if d == 0 else n_ctx_chunks - 1 - j
            return step(s, pl.multiple_of(kc * n_batch, n_batch))

        def lat_step(j, s):
            kl = j if d == 0 else n_lat_chunks - 1 - j
            row = pl.multiple_of(kl * n_batch, n_batch)
            sre_ref[pl.ds(row, n_batch), :] = s[0]
            sim_ref[pl.ds(row, n_batch), :] = s[1]
            return step(s, rows_ctx + row)

        zero = jnp.zeros((n_batch, PAIR_W), F32)
        s = lax.fori_loop(0, n_ctx_chunks, ctx_step, (zero, zero))
        lax.fori_loop(0, n_lat_chunks, lat_step, s)
        sre_b = sre_ref[...].astype(BF16)
        sim_b = sim_ref[...].astype(BF16)
        for gi in range(2):
            y[gi] = (y[gi] + _bdot(ub[gi][rows_ctx:], m_ref[d, gi])
                     + _bdot(sre_b, fr_ref[d, gi]) + _bdot(sim_b, fi_ref[d, gi]))
    y_ref[0] = y[0]
    y_ref[1] = y[1]


def _s5(u_t, tables, n_batch, n_ctx_chunks, n_lat_chunks):
    m_tab, er_tab, ei_tab, fr_tab, fi_tab, a_tab, d_tab = tables
    g_ = N_SSM_GROUPS
    rows_all = (n_ctx_chunks + n_lat_chunks) * n_batch
    rows_lat = n_lat_chunks * n_batch
    pair4 = lambda r, c: pl.BlockSpec((2, 2, r, c), lambda j: (0, j, 0, 0))
    kern = functools.partial(_s5_kernel, n_batch=n_batch, n_ctx_chunks=n_ctx_chunks,
                             n_lat_chunks=n_lat_chunks)
    return pl.pallas_call(
        kern,
        grid=(g_ // 2,),
        in_specs=[pl.BlockSpec((2, rows_all, CHUNK_W), lambda j: (j, 0, 0)),
                  pair4(CHUNK_W, CHUNK_W), pair4(CHUNK_W, PAIR_W), pair4(CHUNK_W, PAIR_W),
                  pair4(PAIR_W, CHUNK_W), pair4(PAIR_W, CHUNK_W),
                  pl.BlockSpec((2, 2, 1, 1, PAIR_W), lambda j: (0, 0, j, 0, 0)),
                  pl.BlockSpec((2, 1, CHUNK_W), lambda j: (j, 0, 0))],
        out_specs=pl.BlockSpec((2, rows_lat, CHUNK_W), lambda j: (j, 0, 0)),
        out_shape=jax.ShapeDtypeStruct((g_, rows_lat, CHUNK_W), F32),
        scratch_shapes=[pltpu.VMEM((rows_all, PAIR_W), F32), pltpu.VMEM((rows_all, PAIR_W), F32),
                        pltpu.VMEM((rows_lat, PAIR_W), F32), pltpu.VMEM((rows_lat, PAIR_W), F32)],
        compiler_params=pltpu.CompilerParams(dimension_semantics=("arbitrary",),
                                             vmem_limit_bytes=VMEM_LIMIT),
        name="s5_scan",
    )(u_t, m_tab, er_tab, ei_tab, fr_tab, fi_tab, a_tab, d_tab)


def _merge_kernel(attn_ref, ys_ref, g_ref, x_ref, g1_ref, sh2_ref, sc2_ref, n2_ref,
                  wab_ref, wglu_ref, wout_ref, x1_ref, h2_ref):
    p_attn = _bdot(attn_ref[0], wab_ref[...])
    glu = _bdot(_gelu_tanh(ys_ref[0]).astype(BF16), wglu_ref[...])
    p_ssm = glu[:, :D_MODEL] * _sigmoid(glu[:, D_MODEL:])
    mix = _sigmoid(g_ref[0, :, :D_MODEL]) * p_attn + _sigmoid(g_ref[0, :, D_MODEL:]) * p_ssm
    x1 = x_ref[0] + g1_ref[0] * _bdot(mix.astype(BF16), wout_ref[...])
    x1_ref[0] = x1
    h2 = _rms(x1) * n2_ref[...]
    h2_ref[0] = (h2 * (1.0 + sc2_ref[0]) + sh2_ref[0]).astype(h2_ref.dtype)


def _merge(attn, y_ssm, g, x, g1, sh2, sc2, n2, wab, wglu, wout):
    b, l, _ = x.shape
    tm = TM_MERGE
    tok = lambda w: pl.BlockSpec((1, tm, w), lambda i, j: (i, j, 0))
    per_b = pl.BlockSpec((1, 1, D_MODEL), lambda i, j: (i, 0, 0))
    return pl.pallas_call(
        _merge_kernel,
        grid=(b, l // tm),
        in_specs=[tok(ATTN_DIM), tok(SSM_DIM), tok(2 * D_MODEL), tok(D_MODEL),
                  per_b, per_b, per_b, _const_spec((1, D_MODEL)),
                  _const_spec(wab.shape), _const_spec(wglu.shape), _const_spec(wout.shape)],
        out_specs=[tok(D_MODEL), tok(D_MODEL)],
        out_shape=[jax.ShapeDtypeStruct((b, l, D_MODEL), F32),
                   jax.ShapeDtypeStruct((b, l, D_MODEL), BF16)],
        compiler_params=pltpu.CompilerParams(dimension_semantics=("arbitrary", "arbitrary"),
                                             vmem_limit_bytes=VMEM_LIMIT),
        name="merge",
    )(attn, y_ssm, g, x, g1, sh2, sc2, n2, wab, wglu, wout)


def _ffn_kernel(hp_ref, h_ref, hn_ref, x1_ref, g2_ref, wv_ref, wg_ref, cwv_ref, cwg_ref,
                cbv_ref, cbg_ref, wd_ref, fw_ref, o_ref, zv_ref, zg_ref, *, tm):
    j = pl.program_id(1)
    keep_prev = (j > 0).astype(F32)
    keep_next = (j < pl.num_programs(1) - 1).astype(F32)
    hcat = jnp.concatenate([hp_ref[0], h_ref[0], hn_ref[0]], axis=0)
    row = lax.broadcasted_iota(jnp.int32, (tm + 2 * HALO, 1), 0)
    keep = jnp.where(row < HALO, keep_prev, jnp.where(row >= HALO + tm, keep_next, 1.0))

    def conv(z_ref, cw, cb):
        return (z_ref[pl.ds(HALO - 1, tm), :] * cw[0:1] + z_ref[pl.ds(HALO, tm), :] * cw[1:2]
                + z_ref[pl.ds(HALO + 1, tm), :] * cw[2:3] + cb)

    acc = jnp.zeros((tm, D_MODEL), F32)
    for f in range(N_FF_CHUNKS):
        zv_ref[...] = _bdot(hcat, wv_ref[f]) * keep
        zg_ref[...] = _bdot(hcat, wg_ref[f]) * keep
        val = conv(zv_ref, cwv_ref[f], cbv_ref[f])
        gate = conv(zg_ref, cwg_ref[f], cbg_ref[f])
        act = (gate * _sigmoid(gate) * val).astype(BF16)
        acc = acc + _bdot(act, wd_ref[f])
    x2 = x1_ref[0] + g2_ref[0] * acc
    o_ref[0] = _rms(x2) * fw_ref[...]


def _conv_ffn(h2, x1, g2, wv, wg, cwv, cwg, cbv, cbg, wd, fw):
    b, l, _ = x1.shape
    tm = TM_FFN
    nh = tm // HALO
    last = l // HALO - 1
    tok = lambda: pl.BlockSpec((1, tm, D_MODEL), lambda i, j: (i, j, 0))
    prev = pl.BlockSpec((1, HALO, D_MODEL), lambda i, j: (i, jnp.maximum(j * nh - 1, 0), 0))
    nxt = pl.BlockSpec((1, HALO, D_MODEL), lambda i, j: (i, jnp.minimum((j + 1) * nh, last), 0))
    per_b = pl.BlockSpec((1, 1, D_MODEL), lambda i, j: (i, 0, 0))
    once = lambda shape: pl.BlockSpec(shape, lambda *_: (0,) * len(shape),
                                      pipeline_mode=pl.Buffered(1))
    return pl.pallas_call(
        functools.partial(_ffn_kernel, tm=tm),
        grid=(b, l // tm),
        in_specs=[prev, tok(), nxt, tok(), per_b, once(wv.shape), once(wg.shape),
                  once(cwv.shape), once(cwg.shape), once(cbv.shape), once(cbg.shape),
                  once(wd.shape), _const_spec((1, D_MODEL))],
        out_specs=tok(),
        out_shape=jax.ShapeDtypeStruct((b, l, D_MODEL), F32),
        scratch_shapes=[pltpu.VMEM((tm + 2 * HALO, FF_CHUNK), F32),
                        pltpu.VMEM((tm + 2 * HALO, FF_CHUNK), F32)],
        compiler_params=pltpu.CompilerParams(dimension_semantics=("arbitrary", "arbitrary"),
                                             vmem_limit_bytes=VMEM_LIMIT),
        name="conv_ffn",
    )(h2, h2, h2, x1, g2, wv, wg, cwv, cwg, cbv, cbg, wd, fw)


def _rope_tables(l):
    rows = jnp.repeat(jnp.arange(l // GRID_W, dtype=F32), GRID_W)
    cols = jnp.tile(jnp.arange(GRID_W, dtype=F32), l // GRID_W)
    inv_freq = ROPE_THETA ** (-jnp.arange(0, ROPE_AXIS_DIM, 2, dtype=F32) / ROPE_AXIS_DIM)
    ang = jnp.concatenate([rows[:, None] * inv_freq, cols[:, None] * inv_freq], axis=-1)
    cos, sin = jnp.cos(ang), jnp.sin(ang)
    return jnp.concatenate([cos, cos], axis=-1), jnp.concatenate([-sin, sin], axis=-1)


def _ff_chunks_cols(w):
    return w.reshape(w.shape[0], N_FF_CHUNKS, FF_CHUNK).transpose(1, 0, 2)


def kernel(x, c, ctx, c_ctx, w_mod, b_mod, norm1_w, norm2_w, w_in, q_norm_w, k_norm_w, w_attn_br,
           ssm_lambda_re, ssm_lambda_im, ssm_log_dt, ssm_b_re, ssm_b_im, ssm_c_re, ssm_c_im, ssm_d,
           w_glu, w_out, w_up, conv_w, conv_b, w_down, final_norm_w):
    b, l, d = x.shape
    lc = ctx.shape[1]
    depth = w_mod.shape[0]
    assert depth == 1 and d == D_MODEL and l % TM_PROJ == 0 and lc % CHUNK == 0 and l % CHUNK == 0
    layer = 0

    pad = (-(b + 1)) % 8
    c_rows = jnp.concatenate([c, c_ctx[None, :], jnp.zeros((pad, d), F32)], axis=0)
    mod = _modulation(c_rows, w_mod[layer], b_mod[layer])
    sh1, sc1, g1, sh2, sc2, g2 = [m[:b, None, :] for m in jnp.split(mod, N_MOD, axis=-1)]
    csh1, csc1 = mod[b:b + 1, :d], mod[b:b + 1, d:2 * d]

    perm = jnp.concatenate([jnp.arange(0, HEAD_DIM, 2), jnp.arange(1, HEAD_DIM, 2)])
    w_in_l = w_in[layer]
    head_perm = lambda w: w.reshape(d, -1, HEAD_DIM)[:, :, perm].reshape(d, -1)
    wq = head_perm(w_in_l[:, :Q_END]).astype(BF16)
    wk = head_perm(w_in_l[:, Q_END:K_END]).astype(BF16)
    wv = w_in_l[:, K_END:V_END].astype(BF16)
    wu = w_in_l[:, V_END:U_END].astype(BF16)
    wg = w_in_l[:, U_END:].astype(BF16)
    qn = (q_norm_w[layer][perm] * (1.0 / math.sqrt(HEAD_DIM))).reshape(1, HEAD_DIM)
    kn = k_norm_w[layer][perm].reshape(1, HEAD_DIM)
    n1 = norm1_w[layer].reshape(1, d)
    cos2, sin2 = _rope_tables(l)

    q, k, v, u, g = _inproj_latent(x, sh1, sc1, n1, wq, wk, wv, wu, wg, qn, kn, cos2, sin2)
    kc, vc, uc = _inproj_ctx(ctx, csh1, csc1, n1, wk, wv, wu, kn)

    attn = _attention(q, kc, k, vc, v)

    n_ctx_chunks, n_lat_chunks = lc // CHUNK, l // CHUNK
    u_all = jnp.concatenate([uc, u], axis=1).reshape(b, n_ctx_chunks + n_lat_chunks, CHUNK,
                                                     N_SSM_GROUPS, SSM_GROUP)
    u_t = u_all.transpose(3, 1, 0, 2, 4).reshape(N_SSM_GROUPS, -1, CHUNK_W)
    tables = _s5_tables(ssm_lambda_re[layer], ssm_lambda_im[layer], ssm_log_dt[layer],
                        ssm_b_re[layer], ssm_b_im[layer], ssm_c_re[layer], ssm_c_im[layer],
                        ssm_d[layer])
    y_t = _s5(u_t, tables, b, n_ctx_chunks, n_lat_chunks)
    y_ssm = y_t.reshape(N_SSM_GROUPS, n_lat_chunks, b, CHUNK, SSM_GROUP)
    y_ssm = y_ssm.transpose(2, 1, 3, 0, 4).reshape(b, l, SSM_DIM)

    x1, h2 = _merge(attn, y_ssm, g, x, g1, sh2, sc2, norm2_w[layer].reshape(1, d),
                    w_attn_br[layer].astype(BF16), w_glu[layer].astype(BF16),
                    w_out[layer].astype(BF16))

    w_up_l = w_up[layer]
    cw = conv_w[layer]
    cb = conv_b[layer]
    ffc = lambda t: t.reshape(t.shape[0], N_FF_CHUNKS, FF_CHUNK).transpose(1, 0, 2)
    return _conv_ffn(
        h2, x1, g2,
        _ff_chunks_cols(w_up_l[:, :D_FF]).astype(BF16), _ff_chunks_cols(w_up_l[:, D_FF:]).astype(BF16),
        ffc(cw[:, :D_FF]), ffc(cw[:, D_FF:]),
        ffc(cb[None, :D_FF]), ffc(cb[None, D_FF:]),
        w_down[layer].reshape(N_FF_CHUNKS, FF_CHUNK, d).astype(BF16),
        final_norm_w.reshape(1, d))
```

```python
import functools
import math

import jax
import jax.numpy as jnp
from jax import lax
from jax.experimental import pallas as pl
from jax.experimental.pallas import tpu as pltpu

F32 = jnp.float32
BF16 = jnp.bfloat16

D_MODEL = 1024
GRID_W = 64
N_HEADS = 8
N_KV_HEADS = 2
HEAD_DIM = 128
HEADS_PER_KV = N_HEADS // N_KV_HEADS
ATTN_DIM = N_HEADS * HEAD_DIM
KV_DIM = N_KV_HEADS * HEAD_DIM
ROPE_THETA = 10000.0
ROPE_AXIS_DIM = HEAD_DIM // 2
SSM_DIM = 512
SSM_GROUP = 16
N_SSM_GROUPS = SSM_DIM // SSM_GROUP
SSM_STATE = 64
Q_END = ATTN_DIM
K_END = Q_END + KV_DIM
V_END = K_END + KV_DIM
U_END = V_END + SSM_DIM
D_IN = U_END + 2 * D_MODEL
D_FF = 2816
N_MOD = 6
EPS = 1e-6

LANES = 128
CHUNK = 8
N_LANE_BLOCKS = SSM_DIM // LANES
GROUPS_PER_BLOCK = LANES // SSM_GROUP
CHUNK_W = CHUNK * LANES
STATE_W = GROUPS_PER_BLOCK * SSM_STATE

FF_CHUNK = 256
N_FF_CHUNKS = D_FF // FF_CHUNK
HALO = 16

VMEM_LIMIT = 56 * 1024 * 1024

TM_PROJ = 512
TQ_ATTN = 512
TM_MERGE = 512
TM_FFN = 512
S5_STATE_ROWS = 512
S5_OUT_ROWS = 1024


def _sigmoid(x):
    return 1.0 / (1.0 + jnp.exp(-x))


def _gelu_tanh(x):
    return 0.5 * x * (1.0 + jnp.tanh(math.sqrt(2.0 / math.pi) * (x + 0.044715 * (x * x * x))))


def _rms(x):
    return x * lax.rsqrt(jnp.mean(x * x, axis=-1, keepdims=True) + EPS)


def _bdot(a, b):
    return jnp.dot(a, b, preferred_element_type=F32)


def _const_spec(shape):
    nd = len(shape)
    return pl.BlockSpec(shape, lambda *_: (0,) * nd)


def _params(n_axes):
    return pltpu.CompilerParams(dimension_semantics=("arbitrary",) * n_axes,
                                vmem_limit_bytes=VMEM_LIMIT)


def _mod_kernel(c_ref, w_ref, b_ref, o_ref):
    c = c_ref[...]
    a = c * _sigmoid(c)
    o_ref[...] = jnp.dot(a, w_ref[...], precision=lax.Precision.HIGHEST,
                         preferred_element_type=F32) + b_ref[...]


def _modulation(c_rows, w_mod, b_mod):
    rows = c_rows.shape[0]
    n = w_mod.shape[1]
    tn = 1536
    return pl.pallas_call(
        _mod_kernel,
        grid=(n // tn,),
        in_specs=[pl.BlockSpec((rows, D_MODEL), lambda j: (0, 0)),
                  pl.BlockSpec((D_MODEL, tn), lambda j: (0, j)),
                  pl.BlockSpec((1, tn), lambda j: (0, j))],
        out_specs=pl.BlockSpec((rows, tn), lambda j: (0, j)),
        out_shape=jax.ShapeDtypeStruct((rows, n), F32),
        compiler_params=_params(1),
        name="modulation",
    )(c_rows, w_mod, b_mod.reshape(1, n))


def _norm_rope_store(t, nw, rope, out_ref, col):
    t = _rms(t) * nw
    if rope is not None:
        cos, sin_prev, sin_next = rope
        t = t * cos + pltpu.roll(t, 1, 1) * sin_prev + pltpu.roll(t, HEAD_DIM - 1, 1) * sin_next
    out_ref[0, :, col:col + HEAD_DIM] = t.astype(out_ref.dtype)


def _store_chunk_rows(u, u_scr, u4_ref):
    n_chunks = u.shape[0] // CHUNK
    for j in range(N_LANE_BLOCKS):
        u_scr[j] = u[:, j * LANES:(j + 1) * LANES]
        for s in range(CHUNK):
            u4_ref[j, :, s * LANES:(s + 1) * LANES] = u_scr[j, pl.ds(s, n_chunks, stride=CHUNK), :]


def _inproj_latent_kernel(x_ref, sh_ref, sc_ref, n1_ref, w_ref, qn_ref, kn_ref,
                          cos_ref, sinp_ref, sinn_ref,
                          q_ref, k_ref, v_ref, u4_ref, g_ref, u_scr):
    h = _rms(x_ref[0]) * n1_ref[...]
    hb = (h * (1.0 + sc_ref[0]) + sh_ref[0]).astype(BF16)
    rope = (cos_ref[...], sinp_ref[...], sinn_ref[...])
    q = _bdot(hb, w_ref[:, :Q_END])
    for hd in range(N_HEADS):
        _norm_rope_store(q[:, hd * HEAD_DIM:(hd + 1) * HEAD_DIM], qn_ref[...], rope,
                         q_ref, hd * HEAD_DIM)
    k = _bdot(hb, w_ref[:, Q_END:K_END])
    for hd in range(N_KV_HEADS):
        _norm_rope_store(k[:, hd * HEAD_DIM:(hd + 1) * HEAD_DIM], kn_ref[...], rope,
                         k_ref, hd * HEAD_DIM)
    v_ref[0] = _bdot(hb, w_ref[:, K_END:V_END]).astype(v_ref.dtype)
    _store_chunk_rows(_bdot(hb, w_ref[:, V_END:U_END]), u_scr, u4_ref)
    g_ref[0] = _bdot(hb, w_ref[:, U_END:])


def _inproj_ctx_kernel(x_ref, sh_ref, sc_ref, n1_ref, w_ref, kn_ref, k_ref, v_ref, u4_ref, u_scr):
    h = _rms(x_ref[0]) * n1_ref[...]
    hb = (h * (1.0 + sc_ref[...]) + sh_ref[...]).astype(BF16)
    k = _bdot(hb, w_ref[:, 0:KV_DIM])
    for hd in range(N_KV_HEADS):
        _norm_rope_store(k[:, hd * HEAD_DIM:(hd + 1) * HEAD_DIM], kn_ref[...], None,
                         k_ref, hd * HEAD_DIM)
    v_ref[0] = _bdot(hb, w_ref[:, KV_DIM:2 * KV_DIM]).astype(v_ref.dtype)
    _store_chunk_rows(_bdot(hb, w_ref[:, 2 * KV_DIM:]), u_scr, u4_ref)


def _inproj_latent(x, sh1, sc1, n1, w_in_b, qn, kn, rope):
    b, l, _ = x.shape
    tm = TM_PROJ
    tok = lambda w: pl.BlockSpec((1, tm, w), lambda i, j: (i, j, 0))
    per_b = pl.BlockSpec((1, 1, D_MODEL), lambda i, j: (i, 0, 0))
    rope_spec = pl.BlockSpec((tm, HEAD_DIM), lambda i, j: (j, 0))
    u4_spec = pl.BlockSpec((N_LANE_BLOCKS, tm // CHUNK, CHUNK_W), lambda i, j: (0, j, i))
    return pl.pallas_call(
        _inproj_latent_kernel,
        grid=(b, l // tm),
        in_specs=[tok(D_MODEL), per_b, per_b, _const_spec((1, D_MODEL)), _const_spec(w_in_b.shape),
                  _const_spec((1, HEAD_DIM)), _const_spec((1, HEAD_DIM)),
                  rope_spec, rope_spec, rope_spec],
        out_specs=[tok(ATTN_DIM), tok(KV_DIM), tok(KV_DIM), u4_spec, tok(2 * D_MODEL)],
        out_shape=[jax.ShapeDtypeStruct((b, l, ATTN_DIM), BF16),
                   jax.ShapeDtypeStruct((b, l, KV_DIM), BF16),
                   jax.ShapeDtypeStruct((b, l, KV_DIM), BF16),
                   jax.ShapeDtypeStruct((N_LANE_BLOCKS, l // CHUNK, b * CHUNK_W), F32),
                   jax.ShapeDtypeStruct((b, l, 2 * D_MODEL), F32)],
        scratch_shapes=[pltpu.VMEM((N_LANE_BLOCKS, tm, LANES), F32)],
        compiler_params=_params(2),
        name="inproj_latent",
    )(x, sh1, sc1, n1, w_in_b, qn, kn, *rope)


def _inproj_ctx(ctx, csh1, csc1, n1, w_ctx_b, kn):
    b, lc, _ = ctx.shape
    tok = lambda w: pl.BlockSpec((1, lc, w), lambda i: (i, 0, 0))
    u4_spec = pl.BlockSpec((N_LANE_BLOCKS, lc // CHUNK, CHUNK_W), lambda i: (0, 0, i))
    return pl.pallas_call(
        _inproj_ctx_kernel,
        grid=(b,),
        in_specs=[tok(D_MODEL), _const_spec((1, D_MODEL)), _const_spec((1, D_MODEL)),
                  _const_spec((1, D_MODEL)), _const_spec(w_ctx_b.shape), _const_spec((1, HEAD_DIM))],
        out_specs=[tok(KV_DIM), tok(KV_DIM), u4_spec],
        out_shape=[jax.ShapeDtypeStruct((b, lc, KV_DIM), BF16),
                   jax.ShapeDtypeStruct((b, lc, KV_DIM), BF16),
                   jax.ShapeDtypeStruct((N_LANE_BLOCKS, lc // CHUNK, b * CHUNK_W), F32)],
        scratch_shapes=[pltpu.VMEM((N_LANE_BLOCKS, lc, LANES), F32)],
        compiler_params=_params(1),
        name="inproj_ctx",
    )(ctx, csh1, csc1, n1, w_ctx_b, kn)


def _attn_kernel(q_ref, kc_ref, k_ref, vc_ref, v_ref, o_ref):
    kc = kc_ref[0]
    k = k_ref[0]
    vc = vc_ref[0]
    v = v_ref[0]
    nt = (((1,), (1,)), ((), ()))
    for r in range(HEADS_PER_KV):
        q = q_ref[0, :, r * HEAD_DIM:(r + 1) * HEAD_DIM]
        s_c = lax.dot_general(q, kc, nt, preferred_element_type=F32)
        s_l = lax.dot_general(q, k, nt, preferred_element_type=F32)
        m = jnp.maximum(jnp.max(s_c, axis=-1, keepdims=True), jnp.max(s_l, axis=-1, keepdims=True))
        p_c = jnp.exp(s_c - m)
        p_l = jnp.exp(s_l - m)
        denom = jnp.sum(p_c, axis=-1, keepdims=True) + jnp.sum(p_l, axis=-1, keepdims=True)
        o = _bdot(p_c.astype(BF16), vc) + _bdot(p_l.astype(BF16), v)
        o_ref[0, :, r * HEAD_DIM:(r + 1) * HEAD_DIM] = (o / denom).astype(o_ref.dtype)


def _attention(q, kc, k, vc, v):
    b, l, _ = q.shape
    lc = kc.shape[1]
    tq = TQ_ATTN
    qspec = pl.BlockSpec((1, tq, HEADS_PER_KV * HEAD_DIM), lambda i, h, j: (i, j, h))
    kv = lambda n: pl.BlockSpec((1, n, HEAD_DIM), lambda i, h, j: (i, 0, h))
    return pl.pallas_call(
        _attn_kernel,
        grid=(b, N_KV_HEADS, l // tq),
        in_specs=[qspec, kv(lc), kv(l), kv(lc), kv(l)],
        out_specs=qspec,
        out_shape=jax.ShapeDtypeStruct((b, l, ATTN_DIM), BF16),
        compiler_params=_params(3),
        name="attention",
    )(q, kc, k, vc, v)


def _s5_tables(lam_re, lam_im, log_dt, b_re, b_im, c_re, c_im, d_skip):
    hp = lax.Precision.HIGHEST
    nj, na, n_, t_ = N_LANE_BLOCKS, GROUPS_PER_BLOCK, SSM_STATE, CHUNK
    dt = jnp.exp(log_dt)[..., None]
    mag = jnp.exp(lam_re * dt)
    ang = lam_im * dt
    a_re = mag * jnp.cos(ang)
    a_im = mag * jnp.sin(ang)
    den = lam_re * lam_re + lam_im * lam_im
    nr = a_re - 1.0
    ni = a_im
    f_re = (nr * lam_re + ni * lam_im) / den
    f_im = (ni * lam_re - nr * lam_im) / den
    bb_re = f_re[..., None] * b_re - f_im[..., None] * b_im
    bb_im = f_re[..., None] * b_im + f_im[..., None] * b_re

    pw_re = [jnp.ones_like(a_re)]
    pw_im = [jnp.zeros_like(a_im)]
    for _ in range(t_):
        pr, pi = pw_re[-1], pw_im[-1]
        pw_re.append(pr * a_re - pi * a_im)
        pw_im.append(pr * a_im + pi * a_re)
    pw_re = jnp.stack(pw_re)
    pw_im = jnp.stack(pw_im)

    pb_re = pw_re[..., None] * bb_re - pw_im[..., None] * bb_im
    pb_im = pw_re[..., None] * bb_im + pw_im[..., None] * bb_re
    cp_re = c_re * pw_re[:, :, :, None, :] - c_im * pw_im[:, :, :, None, :]
    cp_im = c_re * pw_im[:, :, :, None, :] + c_im * pw_re[:, :, :, None, :]

    kern = (jnp.einsum('dgpn,tdgnq->tdgpq', c_re, pb_re[:t_], precision=hp)
            - jnp.einsum('dgpn,tdgnq->tdgpq', c_im, pb_im[:t_], precision=hp))

    eye = jnp.eye(na, dtype=F32)
    src = jnp.arange(t_)[:, None]
    tgt = jnp.arange(t_)[None, :]
    fwd = jnp.where((tgt >= src)[:, :, None, None, None],
                    kern[:, 0][jnp.clip(tgt - src, 0, t_ - 1)], 0.0)
    rev = jnp.where((src >= tgt)[:, :, None, None, None],
                    kern[:, 1][jnp.clip(src - tgt, 0, t_ - 1)], 0.0)
    lag = (fwd + rev).reshape(t_, t_, nj, na, SSM_GROUP, SSM_GROUP)
    m_tab = jnp.einsum('ab,stjapq->jsaqtbp', eye, lag).reshape(nj, CHUNK_W, CHUNK_W).astype(BF16)

    tau_in = jnp.stack([t_ - 1 - jnp.arange(t_), jnp.arange(t_)])
    tau_out = jnp.stack([jnp.arange(t_) + 1, t_ - jnp.arange(t_)])
    dsel = jnp.arange(2)[:, None]
    e_src = jnp.stack([pb_re[tau_in, dsel], pb_im[tau_in, dsel]], axis=2)
    e_src = e_src.reshape(2, t_, 2, nj, na, n_, SSM_GROUP)
    e_tab = jnp.einsum('ab,dscjanq->djsaqcbn', eye, e_src).reshape(2, nj, CHUNK_W, 2 * STATE_W)
    f_src = jnp.stack([cp_re[tau_out, dsel], -cp_im[tau_out, dsel]], axis=2)
    f_src = f_src.reshape(2, t_, 2, nj, na, SSM_GROUP, n_)
    f_tab = jnp.einsum('ab,dtcjapn->djcbntap', eye, f_src).reshape(2, nj, 2 * STATE_W, CHUNK_W)
    a_tab = jnp.stack([pw_re[t_], pw_im[t_]], axis=1).reshape(2, 2, nj, 1, STATE_W)
    a_tab = a_tab.transpose(0, 2, 1, 3, 4)
    d_tab = jnp.tile(d_skip.reshape(nj, 1, LANES), (1, t_, 1)).reshape(nj, 1, CHUNK_W)
    return m_tab, e_tab.astype(BF16), f_tab.astype(BF16), a_tab, d_tab


def _s5_state_kernel(uc_ref, ul_ref, e_ref, a_ref, sin_ref, c_scr, sre_scr, sim_scr, *, nb, n_chunks):
    d = pl.program_id(1)
    r = pl.program_id(2)
    a_re = jnp.broadcast_to(a_ref[0, 0, 0], (nb, STATE_W))
    a_im = jnp.broadcast_to(a_ref[0, 0, 1], (nb, STATE_W))

    def scan(store):
        def body(i, s):
            kk = jnp.where(d == 0, i, n_chunks - 1 - i)
            row = pl.multiple_of(kk * nb, nb)
            s_re, s_im = s
            if store:
                sin_ref[0, 0, pl.ds(row, nb), 0:STATE_W] = s_re.astype(sin_ref.dtype)
                sin_ref[0, 0, pl.ds(row, nb), STATE_W:] = s_im.astype(sin_ref.dtype)
            c_re = c_scr[pl.ds(row, nb), 0:STATE_W]
            c_im = c_scr[pl.ds(row, nb), STATE_W:]
            return (s_re * a_re - s_im * a_im + c_re, s_re * a_im + s_im * a_re + c_im)

        s = lax.fori_loop(0, n_chunks, body, (sre_scr[...], sim_scr[...]))
        sre_scr[...] = s[0]
        sim_scr[...] = s[1]

    @pl.when(r == 0)
    def _():
        sre_scr[...] = jnp.zeros_like(sre_scr)
        sim_scr[...] = jnp.zeros_like(sim_scr)
        c_scr[...] = _bdot(uc_ref[0].astype(BF16), e_ref[0, 0])
        scan(False)

    @pl.when(r > 0)
    def _():
        c_scr[...] = _bdot(ul_ref[0].astype(BF16), e_ref[0, 0])
        scan(True)


def _s5_states(u4c, u4l, e_tab, a_tab, nb):
    rows = S5_STATE_ROWS
    assert u4c.shape[1] == rows
    n_tiles = u4l.shape[1] // rows

    def lat_tile(d, r):
        t = jnp.maximum(r - 1, 0)
        return jnp.where(d == 0, t, n_tiles - 1 - t)

    kern = functools.partial(_s5_state_kernel, nb=nb, n_chunks=rows // nb)
    return pl.pallas_call(
        kern,
        grid=(N_LANE_BLOCKS, 2, n_tiles + 1),
        in_specs=[pl.BlockSpec((1, rows, CHUNK_W), lambda j, d, r: (j, 0, 0)),
                  pl.BlockSpec((1, rows, CHUNK_W), lambda j, d, r: (j, lat_tile(d, r), 0)),
                  pl.BlockSpec((1, 1, CHUNK_W, 2 * STATE_W), lambda j, d, r: (d, j, 0, 0)),
                  pl.BlockSpec((1, 1, 2, 1, STATE_W), lambda j, d, r: (d, j, 0, 0, 0))],
        out_specs=pl.BlockSpec((1, 1, rows, 2 * STATE_W), lambda j, d, r: (d, j, lat_tile(d, r), 0)),
        out_shape=jax.ShapeDtypeStruct((2, N_LANE_BLOCKS, u4l.shape[1], 2 * STATE_W), BF16),
        scratch_shapes=[pltpu.VMEM((rows, 2 * STATE_W), F32),
                        pltpu.VMEM((nb, STATE_W), F32), pltpu.VMEM((nb, STATE_W), F32)],
        compiler_params=_params(3),
        name="s5_states",
    )(u4c, u4l, e_tab, a_tab)


def _s5_out_kernel(ul_ref, sin_ref, m_ref, f_ref, d_ref, y_ref):
    u = ul_ref[0]
    y_ref[0] = (u * d_ref[0] + _bdot(u.astype(BF16), m_ref[0])
                + _bdot(sin_ref[0, 0], f_ref[0, 0]) + _bdot(sin_ref[1, 0], f_ref[1, 0]))


def _s5_outputs(u4l, sin, m_tab, f_tab, d_tab):
    rows = S5_OUT_ROWS
    return pl.pallas_call(
        _s5_out_kernel,
        grid=(N_LANE_BLOCKS, u4l.shape[1] // rows),
        in_specs=[pl.BlockSpec((1, rows, CHUNK_W), lambda j, r: (j, r, 0)),
                  pl.BlockSpec((2, 1, rows, 2 * STATE_W), lambda j, r: (0, j, r, 0)),
                  pl.BlockSpec((1, CHUNK_W, CHUNK_W), lambda j, r: (j, 0, 0)),
                  pl.BlockSpec((2, 1, 2 * STATE_W, CHUNK_W), lambda j, r: (0, j, 0, 0)),
                  pl.BlockSpec((1, 1, CHUNK_W), lambda j, r: (j, 0, 0))],
        out_specs=pl.BlockSpec((1, rows, CHUNK_W), lambda j, r: (j, r, 0)),
        out_shape=jax.ShapeDtypeStruct(u4l.shape, F32),
        compiler_params=_params(2),
        name="s5_outputs",
    )(u4l, sin, m_tab, f_tab, d_tab)


def _merge_kernel(attn_ref, y4_ref, g_ref, x_ref, g1_ref, sh2_ref, sc2_ref, n2_ref,
                  wab_ref, wglu_ref, wout_ref, x1_ref, h2_ref, ys_scr):
    n_chunks = ys_scr.shape[1] // CHUNK
    for j in range(N_LANE_BLOCKS):
        for s in range(CHUNK):
            ys_scr[j, pl.ds(s, n_chunks, stride=CHUNK), :] = y4_ref[j, :, s * LANES:(s + 1) * LANES]
    y_ssm = jnp.concatenate([ys_scr[j] for j in range(N_LANE_BLOCKS)], axis=-1)
    p_attn = _bdot(attn_ref[0], wab_ref[...])
    glu = _bdot(_gelu_tanh(y_ssm).astype(BF16), wglu_ref[...])
    p_ssm = glu[:, :D_MODEL] * _sigmoid(glu[:, D_MODEL:])
    mix = _sigmoid(g_ref[0, :, :D_MODEL]) * p_attn + _sigmoid(g_ref[0, :, D_MODEL:]) * p_ssm
    x1 = x_ref[0] + g1_ref[0] * _bdot(mix.astype(BF16), wout_ref[...])
    x1_ref[0] = x1
    h2 = _rms(x1) * n2_ref[...]
    h2_ref[0] = (h2 * (1.0 + sc2_ref[0]) + sh2_ref[0]).astype(h2_ref.dtype)


def _merge(attn, y4, g, x, g1, sh2, sc2, n2, wab, wglu, wout):
    b, l, _ = x.shape
    tm = TM_MERGE
    tok = lambda w: pl.BlockSpec((1, tm, w), lambda i, j: (i, j, 0))
    per_b = pl.BlockSpec((1, 1, D_MODEL), lambda i, j: (i, 0, 0))
    y4_spec = pl.BlockSpec((N_LANE_BLOCKS, tm // CHUNK, CHUNK_W), lambda i, j: (0, j, i))
    return pl.pallas_call(
        _merge_kernel,
        grid=(b, l // tm),
        in_specs=[tok(ATTN_DIM), y4_spec, tok(2 * D_MODEL), tok(D_MODEL),
                  per_b, per_b, per_b, _const_spec((1, D_MODEL)),
                  _const_spec(wab.shape), _const_spec(wglu.shape), _const_spec(wout.shape)],
        out_specs=[tok(D_MODEL), tok(D_MODEL)],
        out_shape=[jax.ShapeDtypeStruct((b, l, D_MODEL), F32),
                   jax.ShapeDtypeStruct((b, l, D_MODEL), BF16)],
        scratch_shapes=[pltpu.VMEM((N_LANE_BLOCKS, tm, LANES), F32)],
        compiler_params=_params(2),
        name="merge",
    )(attn, y4, g, x, g1, sh2, sc2, n2, wab, wglu, wout)


def _ffn_kernel(hp_ref, h_ref, hn_ref, x1_ref, g2_ref, wup_ref, cw_ref, cb_ref, wd_ref, fw_ref,
                o_ref, zv_ref, zg_ref, *, tm):
    j = pl.program_id(1)
    keep_prev = (j > 0).astype(F32)
    keep_next = (j < pl.num_programs(1) - 1).astype(F32)
    hcat = jnp.concatenate([hp_ref[0], h_ref[0], hn_ref[0]], axis=0)
    row = lax.broadcasted_iota(jnp.int32, (tm + 2 * HALO, 1), 0)
    keep = jnp.where(row < HALO, keep_prev, jnp.where(row >= HALO + tm, keep_next, 1.0))

    def conv(z_ref, col):
        cw = cw_ref[:, col:col + FF_CHUNK]
        return (z_ref[pl.ds(HALO - 1, tm), :] * cw[0:1] + z_ref[pl.ds(HALO, tm), :] * cw[1:2]
                + z_ref[pl.ds(HALO + 1, tm), :] * cw[2:3] + cb_ref[:, col:col + FF_CHUNK])

    acc = jnp.zeros((tm, D_MODEL), F32)
    for f in range(N_FF_CHUNKS):
        cv = f * FF_CHUNK
        cg = D_FF + f * FF_CHUNK
        zv_ref[...] = _bdot(hcat, wup_ref[:, cv:cv + FF_CHUNK]) * keep
        zg_ref[...] = _bdot(hcat, wup_ref[:, cg:cg + FF_CHUNK]) * keep
        val = conv(zv_ref, cv)
        gate = conv(zg_ref, cg)
        act = (gate * _sigmoid(gate) * val).astype(BF16)
        acc = acc + _bdot(act, wd_ref[cv:cv + FF_CHUNK, :])
    x2 = x1_ref[0] + g2_ref[0] * acc
    o_ref[0] = _rms(x2) * fw_ref[...]


def _conv_ffn(h2, x1, g2, wup, cw, cb, wd, fw):
    b, l, _ = x1.shape
    tm = TM_FFN
    nh = tm // HALO
    last = l // HALO - 1
    tok = lambda: pl.BlockSpec((1, tm, D_MODEL), lambda i, j: (i, j, 0))
    prev = pl.BlockSpec((1, HALO, D_MODEL), lambda i, j: (i, jnp.maximum(j * nh - 1, 0), 0))
    nxt = pl.BlockSpec((1, HALO, D_MODEL), lambda i, j: (i, jnp.minimum((j + 1) * nh, last), 0))
    per_b = pl.BlockSpec((1, 1, D_MODEL), lambda i, j: (i, 0, 0))
    once = lambda shape: pl.BlockSpec(shape, lambda *_: (0,) * len(shape),
                                      pipeline_mode=pl.Buffered(1))
    return pl.pallas_call(
        functools.partial(_ffn_kernel, tm=tm),
        grid=(b, l // tm),
        in_specs=[prev, tok(), nxt, tok(), per_b, once(wup.shape), once(cw.shape), once(cb.shape),
                  once(wd.shape), _const_spec((1, D_MODEL))],
        out_specs=tok(),
        out_shape=jax.ShapeDtypeStruct((b, l, D_MODEL), F32),
        scratch_shapes=[pltpu.VMEM((tm + 2 * HALO, FF_CHUNK), F32),
                        pltpu.VMEM((tm + 2 * HALO, FF_CHUNK), F32)],
        compiler_params=_params(2),
        name="conv_ffn",
    )(h2, h2, h2, x1, g2, wup, cw, cb, wd, fw)


def _rope_tables(l):
    rows = jnp.repeat(jnp.arange(l // GRID_W, dtype=F32), GRID_W)
    cols = jnp.tile(jnp.arange(GRID_W, dtype=F32), l // GRID_W)
    inv_freq = ROPE_THETA ** (-jnp.arange(0, ROPE_AXIS_DIM, 2, dtype=F32) / ROPE_AXIS_DIM)
    ang = jnp.concatenate([rows[:, None] * inv_freq, cols[:, None] * inv_freq], axis=-1)
    cos = jnp.repeat(jnp.cos(ang), 2, axis=-1)
    sin = jnp.repeat(jnp.sin(ang), 2, axis=-1)
    odd = (jnp.arange(HEAD_DIM) % 2 == 1)[None, :]
    return cos, jnp.where(odd, sin, 0.0), jnp.where(odd, 0.0, -sin)


def kernel(x, c, ctx, c_ctx, w_mod, b_mod, norm1_w, norm2_w, w_in, q_norm_w, k_norm_w, w_attn_br,
           ssm_lambda_re, ssm_lambda_im, ssm_log_dt, ssm_b_re, ssm_b_im, ssm_c_re, ssm_c_im, ssm_d,
           w_glu, w_out, w_up, conv_w, conv_b, w_down, final_norm_w):
    b, l, d = x.shape
    lc = ctx.shape[1]
    assert w_mod.shape[0] == 1 and d == D_MODEL and l % TM_PROJ == 0
    assert lc // CHUNK * b == S5_STATE_ROWS and (l // CHUNK * b) % S5_OUT_ROWS == 0
    layer = 0

    pad = (-(b + 1)) % 8
    c_rows = jnp.concatenate([c, c_ctx[None, :], jnp.zeros((pad, d), F32)], axis=0)
    mod = _modulation(c_rows, w_mod[layer], b_mod[layer])
    sh1, sc1, g1, sh2, sc2, g2 = [m[:b, None, :] for m in jnp.split(mod, N_MOD, axis=-1)]
    csh1, csc1 = mod[b:b + 1, :d], mod[b:b + 1, d:2 * d]

    w_in_b = w_in[layer].astype(BF16)
    qn = (q_norm_w[layer] * (1.0 / math.sqrt(HEAD_DIM))).reshape(1, HEAD_DIM)
    kn = k_norm_w[layer].reshape(1, HEAD_DIM)
    n1 = norm1_w[layer].reshape(1, d)

    q, k, v, u4l, g = _inproj_latent(x, sh1, sc1, n1, w_in_b, qn, kn, _rope_tables(l))
    kc, vc, u4c = _inproj_ctx(ctx, csh1, csc1, n1, w_in_b[:, Q_END:U_END], kn)

    attn = _attention(q, kc, k, vc, v)

    m_tab, e_tab, f_tab, a_tab, d_tab = _s5_tables(
        ssm_lambda_re[layer], ssm_lambda_im[layer], ssm_log_dt[layer], ssm_b_re[layer],
        ssm_b_im[layer], ssm_c_re[layer], ssm_c_im[layer], ssm_d[layer])
    u4c = u4c.reshape(N_LANE_BLOCKS, -1, CHUNK_W)
    u4l = u4l.reshape(N_LANE_BLOCKS, -1, CHUNK_W)
    sin = _s5_states(u4c, u4l, e_tab, a_tab, b)
    y4 = _s5_outputs(u4l, sin, m_tab, f_tab, d_tab).reshape(N_LANE_BLOCKS, l // CHUNK, b * CHUNK_W)

    x1, h2 = _merge(attn, y4, g, x, g1, sh2, sc2, norm2_w[layer].reshape(1, d),
                    w_attn_br[layer].astype(BF16), w_glu[layer].astype(BF16),
                    w_out[layer].astype(BF16))

    return _conv_ffn(h2, x1, g2, w_up[layer].astype(BF16), conv_w[layer],
                     conv_b[layer].reshape(1, 2 * D_FF), w_down[layer].astype(BF16),
                     final_norm_w.reshape(1, d))
```

```python
import functools
import math

import jax
import jax.numpy as jnp
from jax import lax
from jax.experimental import pallas as pl
from jax.experimental.pallas import tpu as pltpu

F32 = jnp.float32
BF16 = jnp.bfloat16

D_MODEL = 1024
GRID_W = 64
N_HEADS = 8
N_KV_HEADS = 2
HEAD_DIM = 128
HEADS_PER_KV = N_HEADS // N_KV_HEADS
ATTN_DIM = N_HEADS * HEAD_DIM
KV_DIM = N_KV_HEADS * HEAD_DIM
ROPE_THETA = 10000.0
ROPE_AXIS_DIM = HEAD_DIM // 2
SSM_DIM = 512
SSM_GROUP = 16
N_SSM_GROUPS = SSM_DIM // SSM_GROUP
SSM_STATE = 64
Q_END = ATTN_DIM
K_END = Q_END + KV_DIM
V_END = K_END + KV_DIM
U_END = V_END + SSM_DIM
D_IN = U_END + 2 * D_MODEL
D_FF = 2816
N_MOD = 6
EPS = 1e-6

LANES = 128
CHUNK = 8
N_LANE_BLOCKS = SSM_DIM // LANES
GROUPS_PER_BLOCK = LANES // SSM_GROUP
CHUNK_W = CHUNK * LANES
STATE_W = GROUPS_PER_BLOCK * SSM_STATE

FF_CHUNK = 256
N_FF_CHUNKS = D_FF // FF_CHUNK
HALO = 16

VMEM_LIMIT = 56 * 1024 * 1024

TOK_TILE = 32
TQ_ATTN = 512
TM_FFN = 512
S5_STATE_ROWS = 512
S5_OUT_ROWS = 1024


def _sigmoid(x):
    return 1.0 / (1.0 + jnp.exp(-x))


def _gelu_tanh(x):
    return 0.5 * x * (1.0 + jnp.tanh(math.sqrt(2.0 / math.pi) * (x + 0.044715 * (x * x * x))))


def _rms(x):
    return x * lax.rsqrt(jnp.mean(x * x, axis=-1, keepdims=True) + EPS)


def _bdot(a, b):
    return jnp.dot(a, b, preferred_element_type=F32)


def _const_spec(shape):
    nd = len(shape)
    return pl.BlockSpec(shape, lambda *_: (0,) * nd)


def _params(n_axes):
    return pltpu.CompilerParams(dimension_semantics=("arbitrary",) * n_axes,
                                vmem_limit_bytes=VMEM_LIMIT)


def _mod_kernel(c_ref, w_ref, b_ref, o_ref):
    c = c_ref[...]
    a = c * _sigmoid(c)
    o_ref[...] = jnp.dot(a, w_ref[...], precision=lax.Precision.HIGHEST,
                         preferred_element_type=F32) + b_ref[...]


def _modulation(c_rows, w_mod, b_mod):
    rows = c_rows.shape[0]
    n = w_mod.shape[1]
    tn = 1536
    return pl.pallas_call(
        _mod_kernel,
        grid=(n // tn,),
        in_specs=[pl.BlockSpec((rows, D_MODEL), lambda j: (0, 0)),
                  pl.BlockSpec((D_MODEL, tn), lambda j: (0, j)),
                  pl.BlockSpec((1, tn), lambda j: (0, j))],
        out_specs=pl.BlockSpec((rows, tn), lambda j: (0, j)),
        out_shape=jax.ShapeDtypeStruct((rows, n), F32),
        compiler_params=_params(1),
        name="modulation",
    )(c_rows, w_mod, b_mod.reshape(1, n))


def _norm_rope_store(t, nw, rope, out_ref, col, nb):
    t = _rms(t) * nw
    if rope is not None:
        cos, sin_prev, sin_next = rope
        prev = pltpu.roll(t, 1, 1).reshape(nb, -1, HEAD_DIM)
        nxt = pltpu.roll(t, HEAD_DIM - 1, 1).reshape(nb, -1, HEAD_DIM)
        t3 = t.reshape(nb, -1, HEAD_DIM) * cos + prev * sin_prev + nxt * sin_next
    else:
        t3 = t.reshape(nb, -1, HEAD_DIM)
    out_ref[:, :, col:col + HEAD_DIM] = t3.astype(out_ref.dtype)


def _store_chunk_rows(u, u_scr, u4_ref, nb):
    tokens = u.shape[0] // nb
    for j in range(N_LANE_BLOCKS):
        u_scr[j] = u[:, j * LANES:(j + 1) * LANES]
        for k in range(tokens // CHUNK):
            for s in range(CHUNK):
                u4_ref[j, k * nb:(k + 1) * nb, s * LANES:(s + 1) * LANES] = (
                    u_scr[j, pl.ds(k * CHUNK + s, nb, stride=tokens), :])


def _inproj_kernel(x_ref, sh_ref, sc_ref, n1_ref, w_ref, qn_ref, kn_ref, cos_ref, sinp_ref, sinn_ref,
                   *refs, latent):
    nb, tokens, _ = x_ref.shape
    h = _rms(x_ref[...]) * n1_ref[...]
    h = h * (1.0 + sc_ref[...]) + sh_ref[...]
    hb = h.reshape(nb * tokens, D_MODEL).astype(BF16)
    if latent:
        q_ref, k_ref, v_ref, u4_ref, g_ref, u_scr = refs
        rope = (cos_ref[...], sinp_ref[...], sinn_ref[...])
        q = _bdot(hb, w_ref[:, :Q_END])
        for hd in range(N_HEADS):
            _norm_rope_store(q[:, hd * HEAD_DIM:(hd + 1) * HEAD_DIM], qn_ref[...], rope,
                             q_ref, hd * HEAD_DIM, nb)
    else:
        k_ref, v_ref, u4_ref, u_scr = refs
        rope = None
    k = _bdot(hb, w_ref[:, Q_END:K_END])
    for hd in range(N_KV_HEADS):
        _norm_rope_store(k[:, hd * HEAD_DIM:(hd + 1) * HEAD_DIM], kn_ref[...], rope,
                         k_ref, hd * HEAD_DIM, nb)
    v_ref[...] = _bdot(hb, w_ref[:, K_END:V_END]).reshape(nb, tokens, KV_DIM).astype(v_ref.dtype)
    _store_chunk_rows(_bdot(hb, w_ref[:, V_END:U_END]), u_scr, u4_ref, nb)
    if latent:
        g_ref[...] = _bdot(hb, w_ref[:, U_END:]).reshape(nb, tokens, 2 * D_MODEL)


def _inproj(x, sh1, sc1, n1, w_in_b, qn, kn, rope, latent):
    b, l, _ = x.shape
    tt = TOK_TILE
    tok = lambda w: pl.BlockSpec((b, tt, w), lambda i: (0, i, 0))
    mod_spec = _const_spec(sh1.shape)
    rope_spec = pl.BlockSpec((tt, HEAD_DIM), lambda i: (i, 0))
    u4_spec = pl.BlockSpec((N_LANE_BLOCKS, tt // CHUNK * b, CHUNK_W), lambda i: (0, i, 0))
    kv_u_specs = [tok(KV_DIM), tok(KV_DIM), u4_spec]
    kv_u_shapes = [jax.ShapeDtypeStruct((b, l, KV_DIM), BF16),
                   jax.ShapeDtypeStruct((b, l, KV_DIM), BF16),
                   jax.ShapeDtypeStruct((N_LANE_BLOCKS, l // CHUNK * b, CHUNK_W), F32)]
    if latent:
        out_specs = [tok(ATTN_DIM)] + kv_u_specs + [tok(2 * D_MODEL)]
        out_shape = ([jax.ShapeDtypeStruct((b, l, ATTN_DIM), BF16)] + kv_u_shapes
                     + [jax.ShapeDtypeStruct((b, l, 2 * D_MODEL), F32)])
    else:
        out_specs, out_shape = kv_u_specs, kv_u_shapes
    return pl.pallas_call(
        functools.partial(_inproj_kernel, latent=latent),
        grid=(l // tt,),
        in_specs=[tok(D_MODEL), mod_spec, mod_spec, _const_spec((1, D_MODEL)),
                  _const_spec(w_in_b.shape), _const_spec((1, HEAD_DIM)), _const_spec((1, HEAD_DIM)),
                  rope_spec, rope_spec, rope_spec],
        out_specs=out_specs,
        out_shape=out_shape,
        scratch_shapes=[pltpu.VMEM((N_LANE_BLOCKS, b * tt, LANES), F32)],
        compiler_params=_params(1),
        name="inproj_latent" if latent else "inproj_ctx",
    )(x, sh1, sc1, n1, w_in_b, qn, kn, *rope)


def _attn_kernel(q_ref, kc_ref, k_ref, vc_ref, v_ref, o_ref):
    kc = kc_ref[0]
    k = k_ref[0]
    vc = vc_ref[0]
    v = v_ref[0]
    nt = (((1,), (1,)), ((), ()))
    for r in range(HEADS_PER_KV):
        q = q_ref[0, :, r * HEAD_DIM:(r + 1) * HEAD_DIM]
        s_c = lax.dot_general(q, kc, nt, preferred_element_type=F32)
        s_l = lax.dot_general(q, k, nt, preferred_element_type=F32)
        m = jnp.maximum(jnp.max(s_c, axis=-1, keepdims=True), jnp.max(s_l, axis=-1, keepdims=True))
        p_c = jnp.exp(s_c - m)
        p_l = jnp.exp(s_l - m)
        denom = jnp.sum(p_c, axis=-1, keepdims=True) + jnp.sum(p_l, axis=-1, keepdims=True)
        o = _bdot(p_c.astype(BF16), vc) + _bdot(p_l.astype(BF16), v)
        o_ref[0, :, r * HEAD_DIM:(r + 1) * HEAD_DIM] = (o / denom).astype(o_ref.dtype)


def _attention(q, kc, k, vc, v):
    b, l, _ = q.shape
    lc = kc.shape[1]
    tq = TQ_ATTN
    qspec = pl.BlockSpec((1, tq, HEADS_PER_KV * HEAD_DIM), lambda i, h, j: (i, j, h))
    kv = lambda n: pl.BlockSpec((1, n, HEAD_DIM), lambda i, h, j: (i, 0, h))
    return pl.pallas_call(
        _attn_kernel,
        grid=(b, N_KV_HEADS, l // tq),
        in_specs=[qspec, kv(lc), kv(l), kv(lc), kv(l)],
        out_specs=qspec,
        out_shape=jax.ShapeDtypeStruct((b, l, ATTN_DIM), BF16),
        compiler_params=_params(3),
        name="attention",
    )(q, kc, k, vc, v)


def _s5_tables(lam_re, lam_im, log_dt, b_re, b_im, c_re, c_im, d_skip):
    hp = lax.Precision.HIGHEST
    nj, na, n_, t_ = N_LANE_BLOCKS, GROUPS_PER_BLOCK, SSM_STATE, CHUNK
    dt = jnp.exp(log_dt)[..., None]
    mag = jnp.exp(lam_re * dt)
    ang = lam_im * dt
    a_re = mag * jnp.cos(ang)
    a_im = mag * jnp.sin(ang)
    den = lam_re * lam_re + lam_im * lam_im
    nr = a_re - 1.0
    ni = a_im
    f_re = (nr * lam_re + ni * lam_im) / den
    f_im = (ni * lam_re - nr * lam_im) / den
    bb_re = f_re[..., None] * b_re - f_im[..., None] * b_im
    bb_im = f_re[..., None] * b_im + f_im[..., None] * b_re

    pw_re = [jnp.ones_like(a_re)]
    pw_im = [jnp.zeros_like(a_im)]
    for _ in range(t_):
        pr, pi = pw_re[-1], pw_im[-1]
        pw_re.append(pr * a_re - pi * a_im)
        pw_im.append(pr * a_im + pi * a_re)
    pw_re = jnp.stack(pw_re)
    pw_im = jnp.stack(pw_im)

    pb_re = pw_re[..., None] * bb_re - pw_im[..., None] * bb_im
    pb_im = pw_re[..., None] * bb_im + pw_im[..., None] * bb_re
    cp_re = c_re * pw_re[:, :, :, None, :] - c_im * pw_im[:, :, :, None, :]
    cp_im = c_re * pw_im[:, :, :, None, :] + c_im * pw_re[:, :, :, None, :]

    kern = (jnp.einsum('dgpn,tdgnq->tdgpq', c_re, pb_re[:t_], precision=hp)
            - jnp.einsum('dgpn,tdgnq->tdgpq', c_im, pb_im[:t_], precision=hp))

    src = jnp.arange(t_)[:, None]
    tgt = jnp.arange(t_)[None, :]
    fwd = jnp.where((tgt >= src)[:, :, None, None, None],
                    kern[:, 0][jnp.clip(tgt - src, 0, t_ - 1)], 0.0)
    rev = jnp.where((src >= tgt)[:, :, None, None, None],
                    kern[:, 1][jnp.clip(src - tgt, 0, t_ - 1)], 0.0)
    lag = (fwd + rev).reshape(t_, t_, nj, na, SSM_GROUP, SSM_GROUP)
    m_src = lag.transpose(2, 0, 3, 5, 1, 4).reshape(nj, CHUNK_W, LANES)

    tau_in = jnp.stack([t_ - 1 - jnp.arange(t_), jnp.arange(t_)])
    tau_out = jnp.stack([jnp.arange(t_) + 1, t_ - jnp.arange(t_)])
    dsel = jnp.arange(2)[:, None]
    e_src = jnp.stack([pb_re[tau_in, dsel], pb_im[tau_in, dsel]], axis=2)
    e_src = e_src.reshape(2, t_, 2, nj, na, n_, SSM_GROUP)
    e_src = e_src.transpose(0, 3, 1, 4, 6, 2, 5).reshape(2, nj, CHUNK_W, LANES)
    f_src = jnp.stack([cp_re[tau_out, dsel], -cp_im[tau_out, dsel]], axis=2)
    f_src = f_src.reshape(2, t_, 2, nj, na, SSM_GROUP, n_)
    f_src = f_src.transpose(0, 3, 2, 6, 1, 4, 5).reshape(2, nj, LANES, CHUNK_W)
    a_tab = jnp.stack([pw_re[t_], pw_im[t_]], axis=1).reshape(2, 2, nj, 1, STATE_W)
    a_tab = a_tab.transpose(0, 2, 1, 3, 4)
    d_tab = jnp.tile(d_skip.reshape(nj, 1, LANES), (1, t_, 1)).reshape(nj, 1, CHUNK_W)
    return m_src, e_src, f_src, a_tab, d_tab


def _s5_expand_kernel(m_src_ref, e_src_ref, f_src_ref, m_ref, e_ref, f_ref):
    hp = lax.Precision.HIGHEST
    i32 = jnp.int32
    grp_bits = SSM_GROUP.bit_length() - 1
    st_bits = SSM_STATE.bit_length() - 1
    blk_bits = GROUPS_PER_BLOCK.bit_length() - 1
    gmask = GROUPS_PER_BLOCK - 1
    r_w = lax.broadcasted_iota(i32, (CHUNK_W, 1), 0)
    c_w = lax.broadcasted_iota(i32, (1, CHUNK_W), 1)
    r_l = lax.broadcasted_iota(i32, (LANES, 1), 0)
    c_l = lax.broadcasted_iota(i32, (1, LANES), 1)
    grp_tok_r = (r_w >> grp_bits) & gmask
    grp_tok_c = (c_w >> grp_bits) & gmask
    grp_st_r = (r_w >> st_bits) & gmask
    grp_st_c = (c_w >> st_bits) & gmask
    rep_tok = (((r_l >> grp_bits) == (c_w >> (grp_bits + blk_bits)))
               & ((r_l & (SSM_GROUP - 1)) == (c_w & (SSM_GROUP - 1)))).astype(F32)
    rep_st = (((r_l >> st_bits) == (c_w >> (st_bits + blk_bits)))
              & ((r_l & (SSM_STATE - 1)) == (c_w & (SSM_STATE - 1)))).astype(F32)
    rep_st_t = (((c_l >> st_bits) == (r_w >> (st_bits + blk_bits)))
                & ((c_l & (SSM_STATE - 1)) == (r_w & (SSM_STATE - 1)))).astype(F32)
    full = jnp.dot(m_src_ref[0], rep_tok, precision=hp, preferred_element_type=F32)
    m_ref[0] = jnp.where(grp_tok_r == grp_tok_c, full, 0.0).astype(m_ref.dtype)
    for d in range(2):
        full = jnp.dot(e_src_ref[d, 0], rep_st, precision=hp, preferred_element_type=F32)
        e_ref[d, 0] = jnp.where(grp_tok_r == grp_st_c, full, 0.0).astype(e_ref.dtype)
        full = jnp.dot(rep_st_t, f_src_ref[d, 0], precision=hp, preferred_element_type=F32)
        f_ref[d, 0] = jnp.where(grp_st_r == grp_tok_c, full, 0.0).astype(f_ref.dtype)


def _s5_expand(m_src, e_src, f_src):
    nj = N_LANE_BLOCKS
    dj = lambda r, c: pl.BlockSpec((2, 1, r, c), lambda j: (0, j, 0, 0))
    return pl.pallas_call(
        _s5_expand_kernel,
        grid=(nj,),
        in_specs=[pl.BlockSpec((1, CHUNK_W, LANES), lambda j: (j, 0, 0)),
                  dj(CHUNK_W, LANES), dj(LANES, CHUNK_W)],
        out_specs=[pl.BlockSpec((1, CHUNK_W, CHUNK_W), lambda j: (j, 0, 0)),
                   dj(CHUNK_W, 2 * STATE_W), dj(2 * STATE_W, CHUNK_W)],
        out_shape=[jax.ShapeDtypeStruct((nj, CHUNK_W, CHUNK_W), BF16),
                   jax.ShapeDtypeStruct((2, nj, CHUNK_W, 2 * STATE_W), BF16),
                   jax.ShapeDtypeStruct((2, nj, 2 * STATE_W, CHUNK_W), BF16)],
        compiler_params=_params(1),
        name="s5_expand_tables",
    )(m_src, e_src, f_src)


def _s5_state_kernel(uc_ref, ul_ref, e_ref, a_ref, sin_ref, c_scr, sre_scr, sim_scr, *, nb, n_chunks):
    d = pl.program_id(1)
    r = pl.program_id(2)
    a_re = jnp.broadcast_to(a_ref[0, 0, 0], (nb, STATE_W))
    a_im = jnp.broadcast_to(a_ref[0, 0, 1], (nb, STATE_W))

    def scan(store):
        def body(i, s):
            kk = jnp.where(d == 0, i, n_chunks - 1 - i)
            row = pl.multiple_of(kk * nb, nb)
            s_re, s_im = s
            if store:
                sin_ref[0, 0, pl.ds(row, nb), 0:STATE_W] = s_re.astype(sin_ref.dtype)
                sin_ref[0, 0, pl.ds(row, nb), STATE_W:] = s_im.astype(sin_ref.dtype)
            c_re = c_scr[pl.ds(row, nb), 0:STATE_W]
            c_im = c_scr[pl.ds(row, nb), STATE_W:]
            return (s_re * a_re - s_im * a_im + c_re, s_re * a_im + s_im * a_re + c_im)

        s = lax.fori_loop(0, n_chunks, body, (sre_scr[...], sim_scr[...]))
        sre_scr[...] = s[0]
        sim_scr[...] = s[1]

    @pl.when(r == 0)
    def _():
        sre_scr[...] = jnp.zeros_like(sre_scr)
        sim_scr[...] = jnp.zeros_like(sim_scr)
        c_scr[...] = _bdot(uc_ref[0].astype(BF16), e_ref[0, 0])
        scan(False)

    @pl.when(r > 0)
    def _():
        c_scr[...] = _bdot(ul_ref[0].astype(BF16), e_ref[0, 0])
        scan(True)


def _s5_states(u4c, u4l, e_tab, a_tab, nb):
    rows = S5_STATE_ROWS
    assert u4c.shape[1] == rows
    n_tiles = u4l.shape[1] // rows

    def lat_tile(d, r):
        t = jnp.maximum(r - 1, 0)
        return jnp.where(d == 0, t, n_tiles - 1 - t)

    kern = functools.partial(_s5_state_kernel, nb=nb, n_chunks=rows // nb)
    return pl.pallas_call(
        kern,
        grid=(N_LANE_BLOCKS, 2, n_tiles + 1),
        in_specs=[pl.BlockSpec((1, rows, CHUNK_W), lambda j, d, r: (j, 0, 0)),
                  pl.BlockSpec((1, rows, CHUNK_W), lambda j, d, r: (j, lat_tile(d, r), 0)),
                  pl.BlockSpec((1, 1, CHUNK_W, 2 * STATE_W), lambda j, d, r: (d, j, 0, 0)),
                  pl.BlockSpec((1, 1, 2, 1, STATE_W), lambda j, d, r: (d, j, 0, 0, 0))],
        out_specs=pl.BlockSpec((1, 1, rows, 2 * STATE_W), lambda j, d, r: (d, j, lat_tile(d, r), 0)),
        out_shape=jax.ShapeDtypeStruct((2, N_LANE_BLOCKS, u4l.shape[1], 2 * STATE_W), BF16),
        scratch_shapes=[pltpu.VMEM((rows, 2 * STATE_W), F32),
                        pltpu.VMEM((nb, STATE_W), F32), pltpu.VMEM((nb, STATE_W), F32)],
        compiler_params=_params(3),
        name="s5_states",
    )(u4c, u4l, e_tab, a_tab)


def _s5_out_kernel(ul_ref, sin_ref, m_ref, f_ref, d_ref, y_ref):
    u = ul_ref[0]
    y_ref[0] = (u * d_ref[0] + _bdot(u.astype(BF16), m_ref[0])
                + _bdot(sin_ref[0, 0], f_ref[0, 0]) + _bdot(sin_ref[1, 0], f_ref[1, 0]))


def _s5_outputs(u4l, sin, m_tab, f_tab, d_tab):
    rows = S5_OUT_ROWS
    return pl.pallas_call(
        _s5_out_kernel,
        grid=(N_LANE_BLOCKS, u4l.shape[1] // rows),
        in_specs=[pl.BlockSpec((1, rows, CHUNK_W), lambda j, r: (j, r, 0)),
                  pl.BlockSpec((2, 1, rows, 2 * STATE_W), lambda j, r: (0, j, r, 0)),
                  pl.BlockSpec((1, CHUNK_W, CHUNK_W), lambda j, r: (j, 0, 0)),
                  pl.BlockSpec((2, 1, 2 * STATE_W, CHUNK_W), lambda j, r: (0, j, 0, 0)),
                  pl.BlockSpec((1, 1, CHUNK_W), lambda j, r: (j, 0, 0))],
        out_specs=pl.BlockSpec((1, rows, CHUNK_W), lambda j, r: (j, r, 0)),
        out_shape=jax.ShapeDtypeStruct(u4l.shape, F32),
        compiler_params=_params(2),
        name="s5_outputs",
    )(u4l, sin, m_tab, f_tab, d_tab)


def _merge_kernel(attn_ref, y4_ref, g_ref, x_ref, g1_ref, sh2_ref, sc2_ref, n2_ref,
                  wab_ref, wglu_ref, wout_ref, x1_ref, h2_ref, ys_scr):
    nb, tokens, _ = x_ref.shape
    rows = nb * tokens
    for j in range(N_LANE_BLOCKS):
        for k in range(tokens // CHUNK):
            for s in range(CHUNK):
                ys_scr[j, pl.ds(k * CHUNK + s, nb, stride=tokens), :] = (
                    y4_ref[j, k * nb:(k + 1) * nb, s * LANES:(s + 1) * LANES])
    y_ssm = jnp.concatenate([ys_scr[j] for j in range(N_LANE_BLOCKS)], axis=-1)
    p_attn = _bdot(attn_ref[...].reshape(rows, ATTN_DIM), wab_ref[...])
    glu = _bdot(_gelu_tanh(y_ssm).astype(BF16), wglu_ref[...])
    p_ssm = glu[:, :D_MODEL] * _sigmoid(glu[:, D_MODEL:])
    g = g_ref[...].reshape(rows, 2 * D_MODEL)
    mix = _sigmoid(g[:, :D_MODEL]) * p_attn + _sigmoid(g[:, D_MODEL:]) * p_ssm
    x_mix = _bdot(mix.astype(BF16), wout_ref[...]).reshape(nb, tokens, D_MODEL)
    x1 = x_ref[...] + g1_ref[...] * x_mix
    x1_ref[...] = x1
    h2 = _rms(x1) * n2_ref[...]
    h2_ref[...] = (h2 * (1.0 + sc2_ref[...]) + sh2_ref[...]).astype(h2_ref.dtype)


def _merge(attn, y4, g, x, g1, sh2, sc2, n2, wab, wglu, wout):
    b, l, _ = x.shape
    tt = TOK_TILE
    tok = lambda w: pl.BlockSpec((b, tt, w), lambda i: (0, i, 0))
    per_b = _const_spec((b, 1, D_MODEL))
    y4_spec = pl.BlockSpec((N_LANE_BLOCKS, tt // CHUNK * b, CHUNK_W), lambda i: (0, i, 0))
    return pl.pallas_call(
        _merge_kernel,
        grid=(l // tt,),
        in_specs=[tok(ATTN_DIM), y4_spec, tok(2 * D_MODEL), tok(D_MODEL),
                  per_b, per_b, per_b, _const_spec((1, D_MODEL)),
                  _const_spec(wab.shape), _const_spec(wglu.shape), _const_spec(wout.shape)],
        out_specs=[tok(D_MODEL), tok(D_MODEL)],
        out_shape=[jax.ShapeDtypeStruct((b, l, D_MODEL), F32),
                   jax.ShapeDtypeStruct((b, l, D_MODEL), BF16)],
        scratch_shapes=[pltpu.VMEM((N_LANE_BLOCKS, b * tt, LANES), F32)],
        compiler_params=_params(1),
        name="merge",
    )(attn, y4, g, x, g1, sh2, sc2, n2, wab, wglu, wout)


def _ffn_kernel(hp_ref, h_ref, hn_ref, x1_ref, g2_ref, wup_ref, cw_ref, cb_ref, wd_ref, fw_ref,
                o_ref, zv_ref, zg_ref, *, tm):
    j = pl.program_id(1)
    keep_prev = (j > 0).astype(F32)
    keep_next = (j < pl.num_programs(1) - 1).astype(F32)
    hcat = jnp.concatenate([hp_ref[0], h_ref[0], hn_ref[0]], axis=0)
    row = lax.broadcasted_iota(jnp.int32, (tm + 2 * HALO, 1), 0)
    keep = jnp.where(row < HALO, keep_prev, jnp.where(row >= HALO + tm, keep_next, 1.0))

    def conv(z_ref, col):
        cw = cw_ref[:, col:col + FF_CHUNK]
        return (z_ref[pl.ds(HALO - 1, tm), :] * cw[0:1] + z_ref[pl.ds(HALO, tm), :] * cw[1:2]
                + z_ref[pl.ds(HALO + 1, tm), :] * cw[2:3] + cb_ref[:, col:col + FF_CHUNK])

    acc = jnp.zeros((tm, D_MODEL), F32)
    for f in range(N_FF_CHUNKS):
        cv = f * FF_CHUNK
        cg = D_FF + f * FF_CHUNK
        zv_ref[...] = _bdot(hcat, wup_ref[:, cv:cv + FF_CHUNK]) * keep
        zg_ref[...] = _bdot(hcat, wup_ref[:, cg:cg + FF_CHUNK]) * keep
        val = conv(zv_ref, cv)
        gate = conv(zg_ref, cg)
        act = (gate * _sigmoid(gate) * val).astype(BF16)
        acc = acc + _bdot(act, wd_ref[cv:cv + FF_CHUNK, :])
    x2 = x1_ref[0] + g2_ref[0] * acc
    o_ref[0] = _rms(x2) * fw_ref[...]


def _conv_ffn(h2, x1, g2, wup, cw, cb, wd, fw):
    b, l, _ = x1.shape
    tm = TM_FFN
    nh = tm // HALO
    last = l // HALO - 1
    tok = lambda: pl.BlockSpec((1, tm, D_MODEL), lambda i, j: (i, j, 0))
    prev = pl.BlockSpec((1, HALO, D_MODEL), lambda i, j: (i, jnp.maximum(j * nh - 1, 0), 0))
    nxt = pl.BlockSpec((1, HALO, D_MODEL), lambda i, j: (i, jnp.minimum((j + 1) * nh, last), 0))
    per_b = pl.BlockSpec((1, 1, D_MODEL), lambda i, j: (i, 0, 0))
    once = lambda shape: pl.BlockSpec(shape, lambda *_: (0,) * len(shape),
                                      pipeline_mode=pl.Buffered(1))
    return pl.pallas_call(
        functools.partial(_ffn_kernel, tm=tm),
        grid=(b, l // tm),
        in_specs=[prev, tok(), nxt, tok(), per_b, once(wup.shape), once(cw.shape), once(cb.shape),
                  once(wd.shape), _const_spec((1, D_MODEL))],
        out_specs=tok(),
        out_shape=jax.ShapeDtypeStruct((b, l, D_MODEL), F32),
        scratch_shapes=[pltpu.VMEM((tm + 2 * HALO, FF_CHUNK), F32),
                        pltpu.VMEM((tm + 2 * HALO, FF_CHUNK), F32)],
        compiler_params=_params(2),
        name="conv_ffn",
    )(h2, h2, h2, x1, g2, wup, cw, cb, wd, fw)


def _rope_tables(l):
    rows = jnp.repeat(jnp.arange(l // GRID_W, dtype=F32), GRID_W)
    cols = jnp.tile(jnp.arange(GRID_W, dtype=F32), l // GRID_W)
    inv_freq = ROPE_THETA ** (-jnp.arange(0, ROPE_AXIS_DIM, 2, dtype=F32) / ROPE_AXIS_DIM)
    ang = jnp.concatenate([rows[:, None] * inv_freq, cols[:, None] * inv_freq], axis=-1)
    cos = jnp.repeat(jnp.cos(ang), 2, axis=-1)
    sin = jnp.repeat(jnp.sin(ang), 2, axis=-1)
    odd = (jnp.arange(HEAD_DIM) % 2 == 1)[None, :]
    return cos, jnp.where(odd, sin, 0.0), jnp.where(odd, 0.0, -sin)


def kernel(x, c, ctx, c_ctx, w_mod, b_mod, norm1_w, norm2_w, w_in, q_norm_w, k_norm_w, w_attn_br,
           ssm_lambda_re, ssm_lambda_im, ssm_log_dt, ssm_b_re, ssm_b_im, ssm_c_re, ssm_c_im, ssm_d,
           w_glu, w_out, w_up, conv_w, conv_b, w_down, final_norm_w):
    b, l, d = x.shape
    lc = ctx.shape[1]
    assert w_mod.shape[0] == 1 and d == D_MODEL and l % TOK_TILE == 0 and lc % TOK_TILE == 0
    assert lc // CHUNK * b == S5_STATE_ROWS and (l // CHUNK * b) % S5_OUT_ROWS == 0
    layer = 0

    pad = (-(b + 1)) % 8
    c_rows = jnp.concatenate([c, c_ctx[None, :], jnp.zeros((pad, d), F32)], axis=0)
    mod = _modulation(c_rows, w_mod[layer], b_mod[layer])
    sh1, sc1, g1, sh2, sc2, g2 = [m[:b, None, :] for m in jnp.split(mod, N_MOD, axis=-1)]
    csh1, csc1 = mod[b:b + 1, None, :d], mod[b:b + 1, None, d:2 * d]

    w_in_b = w_in[layer].astype(BF16)
    qn = (q_norm_w[layer] * (1.0 / math.sqrt(HEAD_DIM))).reshape(1, HEAD_DIM)
    kn = k_norm_w[layer].reshape(1, HEAD_DIM)
    n1 = norm1_w[layer].reshape(1, d)

    no_rope = (jnp.ones((lc, HEAD_DIM), F32),) + (jnp.zeros((lc, HEAD_DIM), F32),) * 2
    q, k, v, u4l, g = _inproj(x, sh1, sc1, n1, w_in_b, qn, kn, _rope_tables(l), latent=True)
    kc, vc, u4c = _inproj(ctx, csh1, csc1, n1, w_in_b, qn, kn, no_rope, latent=False)

    attn = _attention(q, kc, k, vc, v)

    m_src, e_src, f_src, a_tab, d_tab = _s5_tables(
        ssm_lambda_re[layer], ssm_lambda_im[layer], ssm_log_dt[layer], ssm_b_re[layer],
        ssm_b_im[layer], ssm_c_re[layer], ssm_c_im[layer], ssm_d[layer])
    m_tab, e_tab, f_tab = _s5_expand(m_src, e_src, f_src)
    sin = _s5_states(u4c, u4l, e_tab, a_tab, b)
    y4 = _s5_outputs(u4l, sin, m_tab, f_tab, d_tab)

    x1, h2 = _merge(attn, y4, g, x, g1, sh2, sc2, norm2_w[layer].reshape(1, d),
                    w_attn_br[layer].astype(BF16), w_glu[layer].astype(BF16),
                    w_out[layer].astype(BF16))

    return _conv_ffn(h2, x1, g2, w_up[layer].astype(BF16), conv_w[layer],
                     conv_b[layer].reshape(1, 2 * D_FF), w_down[layer].astype(BF16),
                     final_norm_w.reshape(1, d))
```

```python
import functools
import math

import jax
import jax.numpy as jnp
import numpy as np
from jax import lax
from jax.experimental import pallas as pl
from jax.experimental.pallas import tpu as pltpu

F32 = jnp.float32
BF16 = jnp.bfloat16

D_MODEL = 1024
GRID_W = 64
N_HEADS = 8
N_KV_HEADS = 2
HEAD_DIM = 128
HEADS_PER_KV = N_HEADS // N_KV_HEADS
ATTN_DIM = N_HEADS * HEAD_DIM
KV_DIM = N_KV_HEADS * HEAD_DIM
ROPE_THETA = 10000.0
ROPE_AXIS_DIM = HEAD_DIM // 2
SSM_DIM = 512
SSM_GROUP = 16
N_SSM_GROUPS = SSM_DIM // SSM_GROUP
SSM_STATE = 64
Q_END = ATTN_DIM
K_END = Q_END + KV_DIM
V_END = K_END + KV_DIM
U_END = V_END + SSM_DIM
D_IN = U_END + 2 * D_MODEL
D_FF = 2816
N_MOD = 6
EPS = 1e-6

LANES = 128
SUB = 8
CHUNK = 8
N_LANE_BLOCKS = SSM_DIM // LANES
GROUPS_PER_BLOCK = LANES // SSM_GROUP
CHUNK_W = CHUNK * LANES
STATE_W = GROUPS_PER_BLOCK * SSM_STATE

FF_CHUNK = 256
N_FF_CHUNKS = D_FF // FF_CHUNK
HALO = 16

VMEM_LIMIT = 56 * 1024 * 1024

TOK_TILE = 32
TQ_ATTN = 512
TM_FFN = 512
S5_STATE_ROWS = 512
S5_OUT_ROWS = 1024


def _sigmoid(x):
    return 1.0 / (1.0 + jnp.exp(-x))


def _gelu_tanh(x):
    return 0.5 * x * (1.0 + jnp.tanh(math.sqrt(2.0 / math.pi) * (x + 0.044715 * (x * x * x))))


def _rms(x):
    return x * lax.rsqrt(jnp.mean(x * x, axis=-1, keepdims=True) + EPS)


def _bdot(a, b):
    return jnp.dot(a, b, preferred_element_type=F32)


def _const_spec(shape):
    nd = len(shape)
    return pl.BlockSpec(shape, lambda *_: (0,) * nd)


def _params(n_axes):
    return pltpu.CompilerParams(dimension_semantics=("arbitrary",) * n_axes,
                                vmem_limit_bytes=VMEM_LIMIT)


def _mod_kernel(c_ref, w_ref, b_ref, o_ref):
    c = c_ref[...]
    a = c * _sigmoid(c)
    o_ref[...] = jnp.dot(a, w_ref[...], precision=lax.Precision.HIGHEST,
                         preferred_element_type=F32) + b_ref[...]


def _modulation(c_rows, w_mod, b_mod):
    rows = c_rows.shape[0]
    n = w_mod.shape[1]
    tn = 1536
    return pl.pallas_call(
        _mod_kernel,
        grid=(n // tn,),
        in_specs=[pl.BlockSpec((rows, D_MODEL), lambda j: (0, 0)),
                  pl.BlockSpec((D_MODEL, tn), lambda j: (0, j)),
                  pl.BlockSpec((1, tn), lambda j: (0, j))],
        out_specs=pl.BlockSpec((rows, tn), lambda j: (0, j)),
        out_shape=jax.ShapeDtypeStruct((rows, n), F32),
        compiler_params=_params(1),
        name="modulation",
    )(c_rows, w_mod, b_mod.reshape(1, n))


def _norm_rope_store(t, nw, rope, out_ref, col, nb):
    t = _rms(t) * nw
    if rope is not None:
        cos, sin_prev, sin_next = rope
        prev = pltpu.roll(t, 1, 1).reshape(nb, -1, HEAD_DIM)
        nxt = pltpu.roll(t, HEAD_DIM - 1, 1).reshape(nb, -1, HEAD_DIM)
        t3 = t.reshape(nb, -1, HEAD_DIM) * cos + prev * sin_prev + nxt * sin_next
    else:
        t3 = t.reshape(nb, -1, HEAD_DIM)
    out_ref[:, :, col:col + HEAD_DIM] = t3.astype(out_ref.dtype)


def _store_chunk_rows(u, u_scr, u4_ref, nb):
    tokens = u.shape[0] // nb
    for j in range(N_LANE_BLOCKS):
        u_scr[j] = u[:, j * LANES:(j + 1) * LANES]
        for k in range(tokens // CHUNK):
            for s in range(CHUNK):
                u4_ref[j, k * nb:(k + 1) * nb, s * LANES:(s + 1) * LANES] = (
                    u_scr[j, pl.ds(k * CHUNK + s, nb, stride=tokens), :])


def _inproj_kernel(x_ref, sh_ref, sc_ref, n1_ref, w_ref, qn_ref, kn_ref, cos_ref, sinp_ref, sinn_ref,
                   *refs, latent):
    nb, tokens, _ = x_ref.shape
    h = _rms(x_ref[...]) * n1_ref[...]
    h = h * (1.0 + sc_ref[...]) + sh_ref[...]
    hb = h.reshape(nb * tokens, D_MODEL).astype(BF16)
    if latent:
        q_ref, k_ref, v_ref, u4_ref, g_ref, u_scr = refs
        rope = (cos_ref[...], sinp_ref[...], sinn_ref[...])
        q = _bdot(hb, w_ref[:, :Q_END])
        for hd in range(N_HEADS):
            _norm_rope_store(q[:, hd * HEAD_DIM:(hd + 1) * HEAD_DIM], qn_ref[...], rope,
                             q_ref, hd * HEAD_DIM, nb)
    else:
        k_ref, v_ref, u4_ref, u_scr = refs
        rope = None
    k = _bdot(hb, w_ref[:, Q_END:K_END])
    for hd in range(N_KV_HEADS):
        _norm_rope_store(k[:, hd * HEAD_DIM:(hd + 1) * HEAD_DIM], kn_ref[...], rope,
                         k_ref, hd * HEAD_DIM, nb)
    v_ref[...] = _bdot(hb, w_ref[:, K_END:V_END]).reshape(nb, tokens, KV_DIM).astype(v_ref.dtype)
    _store_chunk_rows(_bdot(hb, w_ref[:, V_END:U_END]), u_scr, u4_ref, nb)
    if latent:
        g_ref[...] = _bdot(hb, w_ref[:, U_END:]).reshape(nb, tokens, 2 * D_MODEL)


def _inproj(x, sh1, sc1, n1, w_in_b, qn, kn, rope, latent):
    b, l, _ = x.shape
    tt = TOK_TILE
    tok = lambda w: pl.BlockSpec((b, tt, w), lambda i: (0, i, 0))
    mod_spec = _const_spec(sh1.shape)
    rope_spec = pl.BlockSpec((tt, HEAD_DIM), lambda i: (i, 0))
    u4_spec = pl.BlockSpec((N_LANE_BLOCKS, tt // CHUNK * b, CHUNK_W), lambda i: (0, i, 0))
    kv_u_specs = [tok(KV_DIM), tok(KV_DIM), u4_spec]
    kv_u_shapes = [jax.ShapeDtypeStruct((b, l, KV_DIM), BF16),
                   jax.ShapeDtypeStruct((b, l, KV_DIM), BF16),
                   jax.ShapeDtypeStruct((N_LANE_BLOCKS, l // CHUNK * b, CHUNK_W), F32)]
    if latent:
        out_specs = [tok(ATTN_DIM)] + kv_u_specs + [tok(2 * D_MODEL)]
        out_shape = ([jax.ShapeDtypeStruct((b, l, ATTN_DIM), BF16)] + kv_u_shapes
                     + [jax.ShapeDtypeStruct((b, l, 2 * D_MODEL), F32)])
    else:
        out_specs, out_shape = kv_u_specs, kv_u_shapes
    return pl.pallas_call(
        functools.partial(_inproj_kernel, latent=latent),
        grid=(l // tt,),
        in_specs=[tok(D_MODEL), mod_spec, mod_spec, _const_spec((1, D_MODEL)),
                  _const_spec(w_in_b.shape), _const_spec((1, HEAD_DIM)), _const_spec((1, HEAD_DIM)),
                  rope_spec, rope_spec, rope_spec],
        out_specs=out_specs,
        out_shape=out_shape,
        scratch_shapes=[pltpu.VMEM((N_LANE_BLOCKS, b * tt, LANES), F32)],
        compiler_params=_params(1),
        name="inproj_latent" if latent else "inproj_ctx",
    )(x, sh1, sc1, n1, w_in_b, qn, kn, *rope)


def _attn_kernel(q_ref, kc_ref, k_ref, vc_ref, v_ref, o_ref):
    kc = kc_ref[0]
    k = k_ref[0]
    vc = jnp.concatenate([vc_ref[0], jnp.ones_like(vc_ref[0])], axis=-1)
    v = jnp.concatenate([v_ref[0], jnp.ones_like(v_ref[0])], axis=-1)
    nt = (((1,), (1,)), ((), ()))

    def scores(r):
        q = q_ref[0, :, r * HEAD_DIM:(r + 1) * HEAD_DIM]
        return (lax.dot_general(q, kc, nt, preferred_element_type=F32),
                lax.dot_general(q, k, nt, preferred_element_type=F32))

    cur = scores(0)
    for r in range(HEADS_PER_KV):
        nxt = scores(r + 1) if r + 1 < HEADS_PER_KV else None
        s_c, s_l = cur
        m = jnp.maximum(jnp.max(s_c, axis=-1, keepdims=True), jnp.max(s_l, axis=-1, keepdims=True))
        p_c = jnp.exp(s_c - m).astype(BF16)
        p_l = jnp.exp(s_l - m).astype(BF16)
        o = _bdot(p_c, vc) + _bdot(p_l, v)
        o_ref[0, :, r * HEAD_DIM:(r + 1) * HEAD_DIM] = (
            o[:, :HEAD_DIM] / o[:, HEAD_DIM:]).astype(o_ref.dtype)
        cur = nxt


def _attention(q, kc, k, vc, v):
    b, l, _ = q.shape
    lc = kc.shape[1]
    tq = TQ_ATTN
    qspec = pl.BlockSpec((1, tq, HEADS_PER_KV * HEAD_DIM), lambda i, h, j: (i, j, h))
    kv = lambda n: pl.BlockSpec((1, n, HEAD_DIM), lambda i, h, j: (i, 0, h))
    return pl.pallas_call(
        _attn_kernel,
        grid=(b, N_KV_HEADS, l // tq),
        in_specs=[qspec, kv(lc), kv(l), kv(lc), kv(l)],
        out_specs=qspec,
        out_shape=jax.ShapeDtypeStruct((b, l, ATTN_DIM), BF16),
        compiler_params=_params(3),
        name="attention",
    )(q, kc, k, vc, v)


def _s5_tables(lam_re, lam_im, log_dt, b_re, b_im, c_re, c_im, d_skip):
    hp = lax.Precision.HIGHEST
    nj, na, n_, t_ = N_LANE_BLOCKS, GROUPS_PER_BLOCK, SSM_STATE, CHUNK
    dt = jnp.exp(log_dt)[..., None]
    mag = jnp.exp(lam_re * dt)
    ang = lam_im * dt
    a_re = mag * jnp.cos(ang)
    a_im = mag * jnp.sin(ang)
    den = lam_re * lam_re + lam_im * lam_im
    nr = a_re - 1.0
    ni = a_im
    f_re = (nr * lam_re + ni * lam_im) / den
    f_im = (ni * lam_re - nr * lam_im) / den
    bb_re = f_re[..., None] * b_re - f_im[..., None] * b_im
    bb_im = f_re[..., None] * b_im + f_im[..., None] * b_re

    pw_re = [jnp.ones_like(a_re)]
    pw_im = [jnp.zeros_like(a_im)]
    for _ in range(t_):
        pr, pi = pw_re[-1], pw_im[-1]
        pw_re.append(pr * a_re - pi * a_im)
        pw_im.append(pr * a_im + pi * a_re)
    pw_re = jnp.stack(pw_re)
    pw_im = jnp.stack(pw_im)

    pb_re = pw_re[..., None] * bb_re - pw_im[..., None] * bb_im
    pb_im = pw_re[..., None] * bb_im + pw_im[..., None] * bb_re
    cp_re = c_re * pw_re[:, :, :, None, :] - c_im * pw_im[:, :, :, None, :]
    cp_im = c_re * pw_im[:, :, :, None, :] + c_im * pw_re[:, :, :, None, :]

    kern = (jnp.einsum('dgpn,tdgnq->tdgpq', c_re, pb_re[:t_], precision=hp)
            - jnp.einsum('dgpn,tdgnq->tdgpq', c_im, pb_im[:t_], precision=hp))

    src = jnp.arange(t_)[:, None]
    tgt = jnp.arange(t_)[None, :]
    fwd = jnp.where((tgt >= src)[:, :, None, None, None],
                    kern[:, 0][jnp.clip(tgt - src, 0, t_ - 1)], 0.0)
    rev = jnp.where((src >= tgt)[:, :, None, None, None],
                    kern[:, 1][jnp.clip(src - tgt, 0, t_ - 1)], 0.0)
    lag = (fwd + rev).reshape(t_, t_, nj, na, SSM_GROUP, SSM_GROUP)
    m_src = lag.transpose(2, 0, 3, 5, 1, 4).reshape(nj, CHUNK_W, LANES)

    tau_in = jnp.stack([t_ - 1 - jnp.arange(t_), jnp.arange(t_)])
    tau_out = jnp.stack([jnp.arange(t_) + 1, t_ - jnp.arange(t_)])
    dsel = jnp.arange(2)[:, None]
    e_src = jnp.stack([pb_re[tau_in, dsel], pb_im[tau_in, dsel]], axis=2)
    e_src = e_src.reshape(2, t_, 2, nj, na, n_, SSM_GROUP)
    e_src = e_src.transpose(0, 3, 1, 4, 6, 2, 5).reshape(2, nj, CHUNK_W, LANES)
    f_src = jnp.stack([cp_re[tau_out, dsel], -cp_im[tau_out, dsel]], axis=2)
    f_src = f_src.reshape(2, t_, 2, nj, na, SSM_GROUP, n_)
    f_src = f_src.transpose(0, 3, 2, 6, 1, 4, 5).reshape(2, nj, LANES, CHUNK_W)
    a_tab = jnp.stack([pw_re[t_], pw_im[t_]], axis=1).reshape(2, 2, nj, 1, STATE_W)
    a_tab = a_tab.transpose(0, 2, 1, 3, 4)
    d_tab = jnp.tile(d_skip.reshape(nj, 1, LANES), (1, t_, 1)).reshape(nj, 1, CHUNK_W)
    return m_src, e_src, f_src, a_tab, d_tab


def _s5_expand_kernel(m_src_ref, e_src_ref, f_src_ref, m_ref, e_ref, f_ref):
    i32 = jnp.int32
    grp_bits = SSM_GROUP.bit_length() - 1
    st_bits = SSM_STATE.bit_length() - 1
    blk_bits = GROUPS_PER_BLOCK.bit_length() - 1
    gmask = GROUPS_PER_BLOCK - 1
    r_w = lax.broadcasted_iota(i32, (CHUNK_W, 1), 0)
    c_w = lax.broadcasted_iota(i32, (1, CHUNK_W), 1)
    r_l = lax.broadcasted_iota(i32, (LANES, 1), 0)
    c_l = lax.broadcasted_iota(i32, (1, LANES), 1)
    grp_tok_r = (r_w >> grp_bits) & gmask
    grp_tok_c = (c_w >> grp_bits) & gmask
    grp_st_r = (r_w >> st_bits) & gmask
    grp_st_c = (c_w >> st_bits) & gmask
    rep_tok = (((r_l >> grp_bits) == (c_w >> (grp_bits + blk_bits)))
               & ((r_l & (SSM_GROUP - 1)) == (c_w & (SSM_GROUP - 1)))).astype(F32).astype(BF16)
    rep_st = (((r_l >> st_bits) == (c_w >> (st_bits + blk_bits)))
              & ((r_l & (SSM_STATE - 1)) == (c_w & (SSM_STATE - 1)))).astype(F32).astype(BF16)
    rep_st_t = (((c_l >> st_bits) == (r_w >> (st_bits + blk_bits)))
                & ((c_l & (SSM_STATE - 1)) == (r_w & (SSM_STATE - 1)))).astype(F32).astype(BF16)
    full = _bdot(m_src_ref[0].astype(BF16), rep_tok)
    m_ref[0] = jnp.where(grp_tok_r == grp_tok_c, full, 0.0).astype(m_ref.dtype)
    for d in range(2):
        full = _bdot(e_src_ref[d, 0].astype(BF16), rep_st)
        e_ref[d, 0] = jnp.where(grp_tok_r == grp_st_c, full, 0.0).astype(e_ref.dtype)
        full = _bdot(rep_st_t, f_src_ref[d, 0].astype(BF16))
        f_ref[d, 0] = jnp.where(grp_st_r == grp_tok_c, full, 0.0).astype(f_ref.dtype)


def _s5_expand(m_src, e_src, f_src):
    nj = N_LANE_BLOCKS
    dj = lambda r, c: pl.BlockSpec((2, 1, r, c), lambda j: (0, j, 0, 0))
    return pl.pallas_call(
        _s5_expand_kernel,
        grid=(nj,),
        in_specs=[pl.BlockSpec((1, CHUNK_W, LANES), lambda j: (j, 0, 0)),
                  dj(CHUNK_W, LANES), dj(LANES, CHUNK_W)],
        out_specs=[pl.BlockSpec((1, CHUNK_W, CHUNK_W), lambda j: (j, 0, 0)),
                   dj(CHUNK_W, 2 * STATE_W), dj(2 * STATE_W, CHUNK_W)],
        out_shape=[jax.ShapeDtypeStruct((nj, CHUNK_W, CHUNK_W), BF16),
                   jax.ShapeDtypeStruct((2, nj, CHUNK_W, 2 * STATE_W), BF16),
                   jax.ShapeDtypeStruct((2, nj, 2 * STATE_W, CHUNK_W), BF16)],
        compiler_params=_params(1),
        name="s5_expand_tables",
    )(m_src, e_src, f_src)


def _s5_state_kernel(uc_ref, ul_ref, e_ref, a_ref, sin_ref, c_scr, sre_scr, sim_scr, *, nb, n_chunks):
    d = pl.program_id(1)
    r = pl.program_id(2)
    a_re = jnp.broadcast_to(a_ref[0, 0, 0], (nb, STATE_W))
    a_im = jnp.broadcast_to(a_ref[0, 0, 1], (nb, STATE_W))

    def scan(store):
        def body(i, s):
            kk = jnp.where(d == 0, i, n_chunks - 1 - i)
            row = pl.multiple_of(kk * nb, nb)
            s_re, s_im = s
            if store:
                sin_ref[0, 0, pl.ds(row, nb), 0:STATE_W] = s_re.astype(sin_ref.dtype)
                sin_ref[0, 0, pl.ds(row, nb), STATE_W:] = s_im.astype(sin_ref.dtype)
            c_re = c_scr[pl.ds(row, nb), 0:STATE_W]
            c_im = c_scr[pl.ds(row, nb), STATE_W:]
            return (s_re * a_re - s_im * a_im + c_re, s_re * a_im + s_im * a_re + c_im)

        s = lax.fori_loop(0, n_chunks, body, (sre_scr[...], sim_scr[...]))
        sre_scr[...] = s[0]
        sim_scr[...] = s[1]

    @pl.when(r == 0)
    def _():
        sre_scr[...] = jnp.zeros_like(sre_scr)
        sim_scr[...] = jnp.zeros_like(sim_scr)
        c_scr[...] = _bdot(uc_ref[0].astype(BF16), e_ref[0, 0])
        scan(False)

    @pl.when(r > 0)
    def _():
        c_scr[...] = _bdot(ul_ref[0].astype(BF16), e_ref[0, 0])
        scan(True)


def _s5_states(u4c, u4l, e_tab, a_tab, nb):
    rows = S5_STATE_ROWS
    assert u4c.shape[1] == rows
    n_tiles = u4l.shape[1] // rows

    def lat_tile(d, r):
        t = jnp.maximum(r - 1, 0)
        return jnp.where(d == 0, t, n_tiles - 1 - t)

    kern = functools.partial(_s5_state_kernel, nb=nb, n_chunks=rows // nb)
    return pl.pallas_call(
        kern,
        grid=(N_LANE_BLOCKS, 2, n_tiles + 1),
        in_specs=[pl.BlockSpec((1, rows, CHUNK_W), lambda j, d, r: (j, 0, 0)),
                  pl.BlockSpec((1, rows, CHUNK_W), lambda j, d, r: (j, lat_tile(d, r), 0)),
                  pl.BlockSpec((1, 1, CHUNK_W, 2 * STATE_W), lambda j, d, r: (d, j, 0, 0)),
                  pl.BlockSpec((1, 1, 2, 1, STATE_W), lambda j, d, r: (d, j, 0, 0, 0))],
        out_specs=pl.BlockSpec((1, 1, rows, 2 * STATE_W), lambda j, d, r: (d, j, lat_tile(d, r), 0)),
        out_shape=jax.ShapeDtypeStruct((2, N_LANE_BLOCKS, u4l.shape[1], 2 * STATE_W), BF16),
        scratch_shapes=[pltpu.VMEM((rows, 2 * STATE_W), F32),
                        pltpu.VMEM((nb, STATE_W), F32), pltpu.VMEM((nb, STATE_W), F32)],
        compiler_params=_params(3),
        name="s5_states",
    )(u4c, u4l, e_tab, a_tab)


def _s5_out_kernel(ul_ref, sin_ref, m_ref, f_ref, d_ref, y_ref):
    u = ul_ref[0]
    y_ref[0] = (u * d_ref[0] + _bdot(u.astype(BF16), m_ref[0])
                + _bdot(sin_ref[0, 0], f_ref[0, 0]) + _bdot(sin_ref[1, 0], f_ref[1, 0]))


def _s5_outputs(u4l, sin, m_tab, f_tab, d_tab):
    rows = S5_OUT_ROWS
    return pl.pallas_call(
        _s5_out_kernel,
        grid=(N_LANE_BLOCKS, u4l.shape[1] // rows),
        in_specs=[pl.BlockSpec((1, rows, CHUNK_W), lambda j, r: (j, r, 0)),
                  pl.BlockSpec((2, 1, rows, 2 * STATE_W), lambda j, r: (0, j, r, 0)),
                  pl.BlockSpec((1, CHUNK_W, CHUNK_W), lambda j, r: (j, 0, 0)),
                  pl.BlockSpec((2, 1, 2 * STATE_W, CHUNK_W), lambda j, r: (0, j, 0, 0)),
                  pl.BlockSpec((1, 1, CHUNK_W), lambda j, r: (j, 0, 0))],
        out_specs=pl.BlockSpec((1, rows, CHUNK_W), lambda j, r: (j, r, 0)),
        out_shape=jax.ShapeDtypeStruct(u4l.shape, F32),
        compiler_params=_params(2),
        name="s5_outputs",
    )(u4l, sin, m_tab, f_tab, d_tab)


def _merge_kernel(attn_ref, y4_ref, g_ref, x_ref, g1_ref, sh2_ref, sc2_ref, n2_ref,
                  wab_ref, wglu_ref, wout_ref, x1_ref, h2_ref, ys_scr):
    nb, tokens, _ = x_ref.shape
    rows = nb * tokens
    for j in range(N_LANE_BLOCKS):
        for k in range(tokens // CHUNK):
            for s in range(CHUNK):
                ys_scr[j, pl.ds(k * CHUNK + s, nb, stride=tokens), :] = (
                    y4_ref[j, k * nb:(k + 1) * nb, s * LANES:(s + 1) * LANES])
    y_ssm = jnp.concatenate([ys_scr[j] for j in range(N_LANE_BLOCKS)], axis=-1)
    p_attn = _bdot(attn_ref[...].reshape(rows, ATTN_DIM), wab_ref[...])
    glu = _bdot(_gelu_tanh(y_ssm).astype(BF16), wglu_ref[...])
    p_ssm = glu[:, :D_MODEL] * _sigmoid(glu[:, D_MODEL:])
    g = g_ref[...].reshape(rows, 2 * D_MODEL)
    mix = _sigmoid(g[:, :D_MODEL]) * p_attn + _sigmoid(g[:, D_MODEL:]) * p_ssm
    x_mix = _bdot(mix.astype(BF16), wout_ref[...]).reshape(nb, tokens, D_MODEL)
    x1 = x_ref[...] + g1_ref[...] * x_mix
    x1_ref[...] = x1
    h2 = _rms(x1) * n2_ref[...]
    h2_ref[...] = (h2 * (1.0 + sc2_ref[...]) + sh2_ref[...]).astype(h2_ref.dtype)


def _merge(attn, y4, g, x, g1, sh2, sc2, n2, wab, wglu, wout):
    b, l, _ = x.shape
    tt = TOK_TILE
    tok = lambda w: pl.BlockSpec((b, tt, w), lambda i: (0, i, 0))
    per_b = _const_spec((b, 1, D_MODEL))
    y4_spec = pl.BlockSpec((N_LANE_BLOCKS, tt // CHUNK * b, CHUNK_W), lambda i: (0, i, 0))
    return pl.pallas_call(
        _merge_kernel,
        grid=(l // tt,),
        in_specs=[tok(ATTN_DIM), y4_spec, tok(2 * D_MODEL), tok(D_MODEL),
                  per_b, per_b, per_b, _const_spec((1, D_MODEL)),
                  _const_spec(wab.shape), _const_spec(wglu.shape), _const_spec(wout.shape)],
        out_specs=[tok(D_MODEL), tok(D_MODEL)],
        out_shape=[jax.ShapeDtypeStruct((b, l, D_MODEL), F32),
                   jax.ShapeDtypeStruct((b, l, D_MODEL), BF16)],
        scratch_shapes=[pltpu.VMEM((N_LANE_BLOCKS, b * tt, LANES), F32)],
        compiler_params=_params(1),
        name="merge",
    )(attn, y4, g, x, g1, sh2, sc2, n2, wab, wglu, wout)


def _ffn_row_select(tm):
    seg_len = tm // SUB
    tile = np.zeros((tm, tm), np.float32)
    for seg in range(SUB):
        for i in range(seg_len):
            tile[SUB * i + seg, seg * seg_len + i] = 1.0
    halo = np.zeros((SUB, 2 * HALO), np.float32)
    halo[0, HALO - 1] = 1.0
    halo[1, HALO] = 1.0
    return jnp.asarray(tile, dtype=BF16), jnp.asarray(halo, dtype=BF16)


def _ffn_kernel(sel_ref, selh_ref, hp_ref, h_ref, hn_ref, x1_ref, g2_ref, wup_ref, cw_ref, cb_ref,
                wd_ref, fw_ref, o_ref, acc_scr, *, tm):
    j = pl.program_id(1)
    keep_prev = (j > 0).astype(F32)
    keep_next = (j < pl.num_programs(1) - 1).astype(F32)
    halo = jnp.concatenate([hp_ref[0], hn_ref[0]], axis=0)
    lhs = jnp.concatenate([_bdot(selh_ref[...], halo), _bdot(sel_ref[...], h_ref[0])],
                          axis=0).astype(BF16)
    sub = lax.broadcasted_iota(jnp.int32, (SUB, 1), 0)

    def up(f):
        return (_bdot(lhs, wup_ref[:, f * FF_CHUNK:(f + 1) * FF_CHUNK]),
                _bdot(lhs, wup_ref[:, D_FF + f * FF_CHUNK:D_FF + (f + 1) * FF_CHUNK]))

    def conv(zall, col):
        z = zall[SUB:]
        before = jnp.where(sub == 0, zall[0:1] * keep_prev, pltpu.roll(z[tm - SUB:], 1, 0))
        after = jnp.where(sub == SUB - 1, zall[1:2] * keep_next, pltpu.roll(z[:SUB], SUB - 1, 0))
        z_prev = jnp.concatenate([before, z[:tm - SUB]], axis=0)
        z_next = jnp.concatenate([z[SUB:], after], axis=0)
        cw = cw_ref[:, col:col + FF_CHUNK]
        return z_prev * cw[0:1] + z * cw[1:2] + z_next * cw[2:3] + cb_ref[:, col:col + FF_CHUNK]

    acc = jnp.zeros((tm, D_MODEL), F32)
    cur = up(0)
    for f in range(N_FF_CHUNKS):
        nxt = up(f + 1) if f + 1 < N_FF_CHUNKS else None
        val = conv(cur[0], f * FF_CHUNK)
        gate = conv(cur[1], D_FF + f * FF_CHUNK)
        act = (gate * _sigmoid(gate) * val).astype(BF16)
        acc = acc + _bdot(act, wd_ref[f * FF_CHUNK:(f + 1) * FF_CHUNK, :])
        cur = nxt
    seg_len = tm // SUB
    for k in range(D_MODEL // LANES):
        acc_scr[k] = acc[:, k * LANES:(k + 1) * LANES]
        for seg in range(SUB):
            o_ref[0, seg * seg_len:(seg + 1) * seg_len, k * LANES:(k + 1) * LANES] = (
                acc_scr[k, pl.ds(seg, seg_len, stride=SUB), :])
    x2 = x1_ref[0] + g2_ref[0] * o_ref[0]
    o_ref[0] = _rms(x2) * fw_ref[...]


def _conv_ffn(h2, x1, g2, wup, cw, cb, wd, fw):
    b, l, _ = x1.shape
    tm = TM_FFN
    nh = tm // HALO
    last = l // HALO - 1
    tok = lambda: pl.BlockSpec((1, tm, D_MODEL), lambda i, j: (i, j, 0))
    prev = pl.BlockSpec((1, HALO, D_MODEL), lambda i, j: (i, jnp.maximum(j * nh - 1, 0), 0))
    nxt = pl.BlockSpec((1, HALO, D_MODEL), lambda i, j: (i, jnp.minimum((j + 1) * nh, last), 0))
    per_b = pl.BlockSpec((1, 1, D_MODEL), lambda i, j: (i, 0, 0))
    once = lambda shape: pl.BlockSpec(shape, lambda *_: (0,) * len(shape),
                                      pipeline_mode=pl.Buffered(1))
    sel, selh = _ffn_row_select(tm)
    return pl.pallas_call(
        functools.partial(_ffn_kernel, tm=tm),
        grid=(b, l // tm),
        in_specs=[once(sel.shape), once(selh.shape), prev, tok(), nxt, tok(), per_b,
                  once(wup.shape), once(cw.shape), once(cb.shape), once(wd.shape),
                  _const_spec((1, D_MODEL))],
        out_specs=tok(),
        out_shape=jax.ShapeDtypeStruct((b, l, D_MODEL), F32),
        scratch_shapes=[pltpu.VMEM((D_MODEL // LANES, tm, LANES), F32)],
        compiler_params=_params(2),
        name="conv_ffn",
    )(sel, selh, h2, h2, h2, x1, g2, wup, cw, cb, wd, fw)


def _rope_tables(l):
    rows = jnp.repeat(jnp.arange(l // GRID_W, dtype=F32), GRID_W)
    cols = jnp.tile(jnp.arange(GRID_W, dtype=F32), l // GRID_W)
    inv_freq = ROPE_THETA ** (-jnp.arange(0, ROPE_AXIS_DIM, 2, dtype=F32) / ROPE_AXIS_DIM)
    ang = jnp.concatenate([rows[:, None] * inv_freq, cols[:, None] * inv_freq], axis=-1)
    cos = jnp.repeat(jnp.cos(ang), 2, axis=-1)
    sin = jnp.repeat(jnp.sin(ang), 2, axis=-1)
    odd = (jnp.arange(HEAD_DIM) % 2 == 1)[None, :]
    return cos, jnp.where(odd, sin, 0.0), jnp.where(odd, 0.0, -sin)


def kernel(x, c, ctx, c_ctx, w_mod, b_mod, norm1_w, norm2_w, w_in, q_norm_w, k_norm_w, w_attn_br,
           ssm_lambda_re, ssm_lambda_im, ssm_log_dt, ssm_b_re, ssm_b_im, ssm_c_re, ssm_c_im, ssm_d,
           w_glu, w_out, w_up, conv_w, conv_b, w_down, final_norm_w):
    b, l, d = x.shape
    lc = ctx.shape[1]
    assert w_mod.shape[0] == 1 and d == D_MODEL and l % TOK_TILE == 0 and lc % TOK_TILE == 0
    assert lc // CHUNK * b == S5_STATE_ROWS and (l // CHUNK * b) % S5_OUT_ROWS == 0
    layer = 0

    pad = (-(b + 1)) % 8
    c_rows = jnp.concatenate([c, c_ctx[None, :], jnp.zeros((pad, d), F32)], axis=0)
    mod = _modulation(c_rows, w_mod[layer], b_mod[layer])
    sh1, sc1, g1, sh2, sc2, g2 = [m[:b, None, :] for m in jnp.split(mod, N_MOD, axis=-1)]
    csh1, csc1 = mod[b:b + 1, None, :d], mod[b:b + 1, None, d:2 * d]

    w_in_b = w_in[layer].astype(BF16)
    qn = (q_norm_w[layer] * (1.0 / math.sqrt(HEAD_DIM))).reshape(1, HEAD_DIM)
    kn = k_norm_w[layer].reshape(1, HEAD_DIM)
    n1 = norm1_w[layer].reshape(1, d)

    no_rope = (jnp.ones((lc, HEAD_DIM), F32),) + (jnp.zeros((lc, HEAD_DIM), F32),) * 2
    q, k, v, u4l, g = _inproj(x, sh1, sc1, n1, w_in_b, qn, kn, _rope_tables(l), latent=True)
    kc, vc, u4c = _inproj(ctx, csh1, csc1, n1, w_in_b, qn, kn, no_rope, latent=False)

    attn = _attention(q, kc, k, vc, v)

    m_src, e_src, f_src, a_tab, d_tab = _s5_tables(
        ssm_lambda_re[layer], ssm_lambda_im[layer], ssm_log_dt[layer], ssm_b_re[layer],
        ssm_b_im[layer], ssm_c_re[layer], ssm_c_im[layer], ssm_d[layer])
    m_tab, e_tab, f_tab = _s5_expand(m_src, e_src, f_src)
    sin = _s5_states(u4c, u4l, e_tab, a_tab, b)
    y4 = _s5_outputs(u4l, sin, m_tab, f_tab, d_tab)

    x1, h2 = _merge(attn, y4, g, x, g1, sh2, sc2, norm2_w[layer].reshape(1, d),
                    w_attn_br[layer].astype(BF16), w_glu[layer].astype(BF16),
                    w_out[layer].astype(BF16))

    return _conv_ffn(h2, x1, g2, w_up[layer].astype(BF16), conv_w[layer],
                     conv_b[layer].reshape(1, 2 * D_FF), w_down[layer].astype(BF16),
                     final_norm_w.reshape(1, d))
```

```python
import functools
import math

import jax
import jax.numpy as jnp
import numpy as np
from jax import lax
from jax.experimental import pallas as pl
from jax.experimental.pallas import tpu as pltpu

F32 = jnp.float32
BF16 = jnp.bfloat16

D_MODEL = 1024
GRID_W = 64
N_HEADS = 8
N_KV_HEADS = 2
HEAD_DIM = 128
HEADS_PER_KV = N_HEADS // N_KV_HEADS
ATTN_DIM = N_HEADS * HEAD_DIM
KV_DIM = N_KV_HEADS * HEAD_DIM
ROPE_THETA = 10000.0
ROPE_AXIS_DIM = HEAD_DIM // 2
SSM_DIM = 512
SSM_GROUP = 16
N_SSM_GROUPS = SSM_DIM // SSM_GROUP
SSM_STATE = 64
Q_END = ATTN_DIM
K_END = Q_END + KV_DIM
V_END = K_END + KV_DIM
U_END = V_END + SSM_DIM
D_IN = U_END + 2 * D_MODEL
D_FF = 2816
N_MOD = 6
EPS = 1e-6

LANES = 128
SUB = 8
CHUNK = 8
N_LANE_BLOCKS = SSM_DIM // LANES
GROUPS_PER_BLOCK = LANES // SSM_GROUP
CHUNK_W = CHUNK * LANES
STATE_W = GROUPS_PER_BLOCK * SSM_STATE

FF_CHUNK = 256
N_FF_CHUNKS = D_FF // FF_CHUNK
HALO = 16

VMEM_LIMIT = 56 * 1024 * 1024

TOK_TILE = 32
TQ_ATTN = 1024
TM_FFN = 512
S5_STATE_ROWS = 512
S5_OUT_ROWS = 1024


def _sigmoid(x):
    return 1.0 / (1.0 + jnp.exp(-x))


def _gelu_tanh(x):
    return 0.5 * x * (1.0 + jnp.tanh(math.sqrt(2.0 / math.pi) * (x + 0.044715 * (x * x * x))))


def _rms(x):
    return x * lax.rsqrt(jnp.mean(x * x, axis=-1, keepdims=True) + EPS)


def _bdot(a, b):
    return jnp.dot(a, b, preferred_element_type=F32)


def _const_spec(shape):
    nd = len(shape)
    return pl.BlockSpec(shape, lambda *_: (0,) * nd)


def _params(n_axes):
    return pltpu.CompilerParams(dimension_semantics=("arbitrary",) * n_axes,
                                vmem_limit_bytes=VMEM_LIMIT)


def _mod_kernel(c_ref, w_ref, b_ref, o_ref):
    c = c_ref[...]
    a = c * _sigmoid(c)
    o_ref[...] = jnp.dot(a, w_ref[...], precision=lax.Precision.HIGHEST,
                         preferred_element_type=F32) + b_ref[...]


def _modulation(c_rows, w_mod, b_mod):
    rows = c_rows.shape[0]
    n = w_mod.shape[1]
    tn = 1536
    return pl.pallas_call(
        _mod_kernel,
        grid=(n // tn,),
        in_specs=[pl.BlockSpec((rows, D_MODEL), lambda j: (0, 0)),
                  pl.BlockSpec((D_MODEL, tn), lambda j: (0, j)),
                  pl.BlockSpec((1, tn), lambda j: (0, j))],
        out_specs=pl.BlockSpec((rows, tn), lambda j: (0, j)),
        out_shape=jax.ShapeDtypeStruct((rows, n), F32),
        compiler_params=_params(1),
        name="modulation",
    )(c_rows, w_mod, b_mod.reshape(1, n))


def _norm_rope_store(t, nw, rope, out_ref, col, nb):
    t = _rms(t) * nw
    if rope is not None:
        cos, sin_prev, sin_next = rope
        prev = pltpu.roll(t, 1, 1).reshape(nb, -1, HEAD_DIM)
        nxt = pltpu.roll(t, HEAD_DIM - 1, 1).reshape(nb, -1, HEAD_DIM)
        t3 = t.reshape(nb, -1, HEAD_DIM) * cos + prev * sin_prev + nxt * sin_next
    else:
        t3 = t.reshape(nb, -1, HEAD_DIM)
    out_ref[:, :, col:col + HEAD_DIM] = t3.astype(out_ref.dtype)


def _store_chunk_rows(u, u_scr, u4_ref, nb):
    tokens = u.shape[0] // nb
    for j in range(N_LANE_BLOCKS):
        u_scr[j] = u[:, j * LANES:(j + 1) * LANES]
        for k in range(tokens // CHUNK):
            for s in range(CHUNK):
                u4_ref[j, k * nb:(k + 1) * nb, s * LANES:(s + 1) * LANES] = (
                    u_scr[j, pl.ds(k * CHUNK + s, nb, stride=tokens), :])


def _inproj_kernel(x_ref, sh_ref, sc_ref, n1_ref, w_ref, qn_ref, kn_ref, cos_ref, sinp_ref, sinn_ref,
                   *refs, latent):
    nb, tokens, _ = x_ref.shape
    h = _rms(x_ref[...]) * n1_ref[...]
    h = h * (1.0 + sc_ref[...]) + sh_ref[...]
    hb = h.reshape(nb * tokens, D_MODEL).astype(BF16)
    if latent:
        q_ref, k_ref, v_ref, u4_ref, g_ref, u_scr = refs
        rope = (cos_ref[...], sinp_ref[...], sinn_ref[...])
        q = _bdot(hb, w_ref[:, :Q_END])
        for hd in range(N_HEADS):
            _norm_rope_store(q[:, hd * HEAD_DIM:(hd + 1) * HEAD_DIM], qn_ref[...], rope,
                             q_ref, hd * HEAD_DIM, nb)
    else:
        k_ref, v_ref, u4_ref, u_scr = refs
        rope = None
    k = _bdot(hb, w_ref[:, Q_END:K_END])
    for hd in range(N_KV_HEADS):
        _norm_rope_store(k[:, hd * HEAD_DIM:(hd + 1) * HEAD_DIM], kn_ref[...], rope,
                         k_ref, hd * HEAD_DIM, nb)
    v_ref[...] = _bdot(hb, w_ref[:, K_END:V_END]).reshape(nb, tokens, KV_DIM).astype(v_ref.dtype)
    _store_chunk_rows(_bdot(hb, w_ref[:, V_END:U_END]), u_scr, u4_ref, nb)
    if latent:
        g_ref[...] = _bdot(hb, w_ref[:, U_END:]).reshape(nb, tokens, 2 * D_MODEL)


def _inproj(x, sh1, sc1, n1, w_in_b, qn, kn, rope, latent):
    b, l, _ = x.shape
    tt = TOK_TILE
    tok = lambda w: pl.BlockSpec((b, tt, w), lambda i: (0, i, 0))
    mod_spec = _const_spec(sh1.shape)
    rope_spec = pl.BlockSpec((tt, HEAD_DIM), lambda i: (i, 0))
    u4_spec = pl.BlockSpec((N_LANE_BLOCKS, tt // CHUNK * b, CHUNK_W), lambda i: (0, i, 0))
    kv_u_specs = [tok(KV_DIM), tok(KV_DIM), u4_spec]
    kv_u_shapes = [jax.ShapeDtypeStruct((b, l, KV_DIM), BF16),
                   jax.ShapeDtypeStruct((b, l, KV_DIM), BF16),
                   jax.ShapeDtypeStruct((N_LANE_BLOCKS, l // CHUNK * b, CHUNK_W), F32)]
    if latent:
        out_specs = [tok(ATTN_DIM)] + kv_u_specs + [tok(2 * D_MODEL)]
        out_shape = ([jax.ShapeDtypeStruct((b, l, ATTN_DIM), BF16)] + kv_u_shapes
                     + [jax.ShapeDtypeStruct((b, l, 2 * D_MODEL), F32)])
    else:
        out_specs, out_shape = kv_u_specs, kv_u_shapes
    return pl.pallas_call(
        functools.partial(_inproj_kernel, latent=latent),
        grid=(l // tt,),
        in_specs=[tok(D_MODEL), mod_spec, mod_spec, _const_spec((1, D_MODEL)),
                  _const_spec(w_in_b.shape), _const_spec((1, HEAD_DIM)), _const_spec((1, HEAD_DIM)),
                  rope_spec, rope_spec, rope_spec],
        out_specs=out_specs,
        out_shape=out_shape,
        scratch_shapes=[pltpu.VMEM((N_LANE_BLOCKS, b * tt, LANES), F32)],
        compiler_params=_params(1),
        name="inproj_latent" if latent else "inproj_ctx",
    )(x, sh1, sc1, n1, w_in_b, qn, kn, *rope)


def _attn_kernel(q_ref, kc_ref, k_ref, vc_ref, v_ref, o_ref):
    keys = jnp.concatenate([kc_ref[0], k_ref[0]], axis=0)
    vals = jnp.concatenate([vc_ref[0], v_ref[0]], axis=0)
    vals = jnp.concatenate([vals, jnp.ones_like(vals)], axis=-1)
    nt = (((1,), (1,)), ((), ()))

    def scores(r):
        q = q_ref[0, :, r * HEAD_DIM:(r + 1) * HEAD_DIM]
        return lax.dot_general(q, keys, nt, preferred_element_type=F32)

    cur = scores(0)
    for r in range(HEADS_PER_KV):
        nxt = scores(r + 1) if r + 1 < HEADS_PER_KV else None
        p = jnp.exp(cur - jnp.max(cur, axis=-1, keepdims=True)).astype(BF16)
        o = _bdot(p, vals)
        o_ref[0, :, r * HEAD_DIM:(r + 1) * HEAD_DIM] = (
            o[:, :HEAD_DIM] / o[:, HEAD_DIM:]).astype(o_ref.dtype)
        cur = nxt


def _attention(q, kc, k, vc, v):
    b, l, _ = q.shape
    lc = kc.shape[1]
    tq = TQ_ATTN
    qspec = pl.BlockSpec((1, tq, HEADS_PER_KV * HEAD_DIM), lambda i, h, j: (i, j, h))
    kv = lambda n: pl.BlockSpec((1, n, HEAD_DIM), lambda i, h, j: (i, 0, h))
    return pl.pallas_call(
        _attn_kernel,
        grid=(b, N_KV_HEADS, l // tq),
        in_specs=[qspec, kv(lc), kv(l), kv(lc), kv(l)],
        out_specs=qspec,
        out_shape=jax.ShapeDtypeStruct((b, l, ATTN_DIM), BF16),
        compiler_params=_params(3),
        name="attention",
    )(q, kc, k, vc, v)


def _s5_tables(lam_re, lam_im, log_dt, b_re, b_im, c_re, c_im, d_skip):
    hp = lax.Precision.HIGHEST
    nj, na, n_, t_ = N_LANE_BLOCKS, GROUPS_PER_BLOCK, SSM_STATE, CHUNK
    dt = jnp.exp(log_dt)[..., None]
    mag = jnp.exp(lam_re * dt)
    ang = lam_im * dt
    a_re = mag * jnp.cos(ang)
    a_im = mag * jnp.sin(ang)
    den = lam_re * lam_re + lam_im * lam_im
    nr = a_re - 1.0
    ni = a_im
    f_re = (nr * lam_re + ni * lam_im) / den
    f_im = (ni * lam_re - nr * lam_im) / den
    bb_re = f_re[..., None] * b_re - f_im[..., None] * b_im
    bb_im = f_re[..., None] * b_im + f_im[..., None] * b_re

    pw_re = [jnp.ones_like(a_re)]
    pw_im = [jnp.zeros_like(a_im)]
    for _ in range(t_):
        pr, pi = pw_re[-1], pw_im[-1]
        pw_re.append(pr * a_re - pi * a_im)
        pw_im.append(pr * a_im + pi * a_re)
    pw_re = jnp.stack(pw_re)
    pw_im = jnp.stack(pw_im)

    pb_re = pw_re[..., None] * bb_re - pw_im[..., None] * bb_im
    pb_im = pw_re[..., None] * bb_im + pw_im[..., None] * bb_re
    cp_re = c_re * pw_re[:, :, :, None, :] - c_im * pw_im[:, :, :, None, :]
    cp_im = c_re * pw_im[:, :, :, None, :] + c_im * pw_re[:, :, :, None, :]

    kern = (jnp.einsum('dgpn,tdgnq->tdgpq', c_re, pb_re[:t_], precision=hp)
            - jnp.einsum('dgpn,tdgnq->tdgpq', c_im, pb_im[:t_], precision=hp))

    src = jnp.arange(t_)[:, None]
    tgt = jnp.arange(t_)[None, :]
    fwd = jnp.where((tgt >= src)[:, :, None, None, None],
                    kern[:, 0][jnp.clip(tgt - src, 0, t_ - 1)], 0.0)
    rev = jnp.where((src >= tgt)[:, :, None, None, None],
                    kern[:, 1][jnp.clip(src - tgt, 0, t_ - 1)], 0.0)
    lag = (fwd + rev).reshape(t_, t_, nj, na, SSM_GROUP, SSM_GROUP)
    m_src = lag.transpose(2, 0, 3, 5, 1, 4).reshape(nj, CHUNK_W, LANES)

    tau_in = jnp.stack([t_ - 1 - jnp.arange(t_), jnp.arange(t_)])
    tau_out = jnp.stack([jnp.arange(t_) + 1, t_ - jnp.arange(t_)])
    dsel = jnp.arange(2)[:, None]
    e_src = jnp.stack([pb_re[tau_in, dsel], pb_im[tau_in, dsel]], axis=2)
    e_src = e_src.reshape(2, t_, 2, nj, na, n_, SSM_GROUP)
    e_src = e_src.transpose(0, 3, 1, 4, 6, 2, 5).reshape(2, nj, CHUNK_W, LANES)
    f_src = jnp.stack([cp_re[tau_out, dsel], -cp_im[tau_out, dsel]], axis=2)
    f_src = f_src.reshape(2, t_, 2, nj, na, SSM_GROUP, n_)
    f_src = f_src.transpose(0, 3, 2, 6, 1, 4, 5).reshape(2, nj, LANES, CHUNK_W)
    a_tab = jnp.stack([pw_re[t_], pw_im[t_]], axis=1).reshape(2, 2, nj, 1, STATE_W)
    a_tab = a_tab.transpose(0, 2, 1, 3, 4)
    d_tab = jnp.tile(d_skip.reshape(nj, 1, LANES), (1, t_, 1)).reshape(nj, 1, CHUNK_W)
    return m_src, e_src, f_src, a_tab, d_tab


def _s5_expand_kernel(m_src_ref, e_src_ref, f_src_ref, m_ref, e_ref, f_ref):
    i32 = jnp.int32
    grp_bits = SSM_GROUP.bit_length() - 1
    st_bits = SSM_STATE.bit_length() - 1
    blk_bits = GROUPS_PER_BLOCK.bit_length() - 1
    gmask = GROUPS_PER_BLOCK - 1
    r_w = lax.broadcasted_iota(i32, (CHUNK_W, 1), 0)
    c_w = lax.broadcasted_iota(i32, (1, CHUNK_W), 1)
    r_l = lax.broadcasted_iota(i32, (LANES, 1), 0)
    c_l = lax.broadcasted_iota(i32, (1, LANES), 1)
    grp_tok_r = (r_w >> grp_bits) & gmask
    grp_tok_c = (c_w >> grp_bits) & gmask
    grp_st_r = (r_w >> st_bits) & gmask
    grp_st_c = (c_w >> st_bits) & gmask
    rep_tok = (((r_l >> grp_bits) == (c_w >> (grp_bits + blk_bits)))
               & ((r_l & (SSM_GROUP - 1)) == (c_w & (SSM_GROUP - 1)))).astype(F32).astype(BF16)
    rep_st = (((r_l >> st_bits) == (c_w >> (st_bits + blk_bits)))
              & ((r_l & (SSM_STATE - 1)) == (c_w & (SSM_STATE - 1)))).astype(F32).astype(BF16)
    rep_st_t = (((c_l >> st_bits) == (r_w >> (st_bits + blk_bits)))
                & ((c_l & (SSM_STATE - 1)) == (r_w & (SSM_STATE - 1)))).astype(F32).astype(BF16)
    full = _bdot(m_src_ref[0].astype(BF16), rep_tok)
    m_ref[0] = jnp.where(grp_tok_r == grp_tok_c, full, 0.0).astype(m_ref.dtype)
    for d in range(2):
        full = _bdot(e_src_ref[d, 0].astype(BF16), rep_st)
        e_ref[d, 0] = jnp.where(grp_tok_r == grp_st_c, full, 0.0).astype(e_ref.dtype)
        full = _bdot(rep_st_t, f_src_ref[d, 0].astype(BF16))
        f_ref[d, 0] = jnp.where(grp_st_r == grp_tok_c, full, 0.0).astype(f_ref.dtype)


def _s5_expand(m_src, e_src, f_src):
    nj = N_LANE_BLOCKS
    dj = lambda r, c: pl.BlockSpec((2, 1, r, c), lambda j: (0, j, 0, 0))
    return pl.pallas_call(
        _s5_expand_kernel,
        grid=(nj,),
        in_specs=[pl.BlockSpec((1, CHUNK_W, LANES), lambda j: (j, 0, 0)),
                  dj(CHUNK_W, LANES), dj(LANES, CHUNK_W)],
        out_specs=[pl.BlockSpec((1, CHUNK_W, CHUNK_W), lambda j: (j, 0, 0)),
                   dj(CHUNK_W, 2 * STATE_W), dj(2 * STATE_W, CHUNK_W)],
        out_shape=[jax.ShapeDtypeStruct((nj, CHUNK_W, CHUNK_W), BF16),
                   jax.ShapeDtypeStruct((2, nj, CHUNK_W, 2 * STATE_W), BF16),
                   jax.ShapeDtypeStruct((2, nj, 2 * STATE_W, CHUNK_W), BF16)],
        compiler_params=_params(1),
        name="s5_expand_tables",
    )(m_src, e_src, f_src)


def _s5_state_kernel(uc_ref, ul_ref, e_ref, a_ref, sin_ref, c_scr, sre_scr, sim_scr, *, nb, n_chunks):
    d = pl.program_id(1)
    r = pl.program_id(2)
    a_re = jnp.broadcast_to(a_ref[0, 0, 0], (nb, STATE_W))
    a_im = jnp.broadcast_to(a_ref[0, 0, 1], (nb, STATE_W))

    def scan(store):
        def body(i, s):
            kk = jnp.where(d == 0, i, n_chunks - 1 - i)
            row = pl.multiple_of(kk * nb, nb)
            s_re, s_im = s
            if store:
                sin_ref[0, 0, pl.ds(row, nb), 0:STATE_W] = s_re.astype(sin_ref.dtype)
                sin_ref[0, 0, pl.ds(row, nb), STATE_W:] = s_im.astype(sin_ref.dtype)
            c_re = c_scr[pl.ds(row, nb), 0:STATE_W]
            c_im = c_scr[pl.ds(row, nb), STATE_W:]
            return (s_re * a_re - s_im * a_im + c_re, s_re * a_im + s_im * a_re + c_im)

        s = lax.fori_loop(0, n_chunks, body, (sre_scr[...], sim_scr[...]))
        sre_scr[...] = s[0]
        sim_scr[...] = s[1]

    @pl.when(r == 0)
    def _():
        sre_scr[...] = jnp.zeros_like(sre_scr)
        sim_scr[...] = jnp.zeros_like(sim_scr)
        c_scr[...] = _bdot(uc_ref[0].astype(BF16), e_ref[0, 0])
        scan(False)

    @pl.when(r > 0)
    def _():
        c_scr[...] = _bdot(ul_ref[0].astype(BF16), e_ref[0, 0])
        scan(True)


def _s5_states(u4c, u4l, e_tab, a_tab, nb):
    rows = S5_STATE_ROWS
    assert u4c.shape[1] == rows
    n_tiles = u4l.shape[1] // rows

    def lat_tile(d, r):
        t = jnp.maximum(r - 1, 0)
        return jnp.where(d == 0, t, n_tiles - 1 - t)

    kern = functools.partial(_s5_state_kernel, nb=nb, n_chunks=rows // nb)
    return pl.pallas_call(
        kern,
        grid=(N_LANE_BLOCKS, 2, n_tiles + 1),
        in_specs=[pl.BlockSpec((1, rows, CHUNK_W), lambda j, d, r: (j, 0, 0)),
                  pl.BlockSpec((1, rows, CHUNK_W), lambda j, d, r: (j, lat_tile(d, r), 0)),
                  pl.BlockSpec((1, 1, CHUNK_W, 2 * STATE_W), lambda j, d, r: (d, j, 0, 0)),
                  pl.BlockSpec((1, 1, 2, 1, STATE_W), lambda j, d, r: (d, j, 0, 0, 0))],
        out_specs=pl.BlockSpec((1, 1, rows, 2 * STATE_W), lambda j, d, r: (d, j, lat_tile(d, r), 0)),
        out_shape=jax.ShapeDtypeStruct((2, N_LANE_BLOCKS, u4l.shape[1], 2 * STATE_W), BF16),
        scratch_shapes=[pltpu.VMEM((rows, 2 * STATE_W), F32),
                        pltpu.VMEM((nb, STATE_W), F32), pltpu.VMEM((nb, STATE_W), F32)],
        compiler_params=_params(3),
        name="s5_states",
    )(u4c, u4l, e_tab, a_tab)


def _s5_out_kernel(ul_ref, sin_ref, m_ref, f_ref, d_ref, y_ref):
    u = ul_ref[0]
    y_ref[0] = (u * d_ref[0] + _bdot(u.astype(BF16), m_ref[0])
                + _bdot(sin_ref[0, 0], f_ref[0, 0]) + _bdot(sin_ref[1, 0], f_ref[1, 0]))


def _s5_outputs(u4l, sin, m_tab, f_tab, d_tab):
    rows = S5_OUT_ROWS
    return pl.pallas_call(
        _s5_out_kernel,
        grid=(N_LANE_BLOCKS, u4l.shape[1] // rows),
        in_specs=[pl.BlockSpec((1, rows, CHUNK_W), lambda j, r: (j, r, 0)),
                  pl.BlockSpec((2, 1, rows, 2 * STATE_W), lambda j, r: (0, j, r, 0)),
                  pl.BlockSpec((1, CHUNK_W, CHUNK_W), lambda j, r: (j, 0, 0)),
                  pl.BlockSpec((2, 1, 2 * STATE_W, CHUNK_W), lambda j, r: (0, j, 0, 0)),
                  pl.BlockSpec((1, 1, CHUNK_W), lambda j, r: (j, 0, 0))],
        out_specs=pl.BlockSpec((1, rows, CHUNK_W), lambda j, r: (j, r, 0)),
        out_shape=jax.ShapeDtypeStruct(u4l.shape, F32),
        compiler_params=_params(2),
        name="s5_outputs",
    )(u4l, sin, m_tab, f_tab, d_tab)


def _merge_kernel(attn_ref, y4_ref, g_ref, x_ref, g1_ref, sh2_ref, sc2_ref, n2_ref,
                  wab_ref, wglu_ref, wout_ref, x1_ref, h2_ref, ys_scr):
    nb, tokens, _ = x_ref.shape
    rows = nb * tokens
    for j in range(N_LANE_BLOCKS):
        for k in range(tokens // CHUNK):
            for s in range(CHUNK):
                ys_scr[j, pl.ds(k * CHUNK + s, nb, stride=tokens), :] = (
                    y4_ref[j, k * nb:(k + 1) * nb, s * LANES:(s + 1) * LANES])
    y_ssm = jnp.concatenate([ys_scr[j] for j in range(N_LANE_BLOCKS)], axis=-1)
    p_attn = _bdot(attn_ref[...].reshape(rows, ATTN_DIM), wab_ref[...])
    glu = _bdot(_gelu_tanh(y_ssm).astype(BF16), wglu_ref[...])
    p_ssm = glu[:, :D_MODEL] * _sigmoid(glu[:, D_MODEL:])
    g = g_ref[...].reshape(rows, 2 * D_MODEL)
    mix = _sigmoid(g[:, :D_MODEL]) * p_attn + _sigmoid(g[:, D_MODEL:]) * p_ssm
    x_mix = _bdot(mix.astype(BF16), wout_ref[...]).reshape(nb, tokens, D_MODEL)
    x1 = x_ref[...] + g1_ref[...] * x_mix
    x1_ref[...] = x1
    h2 = _rms(x1) * n2_ref[...]
    h2_ref[...] = (h2 * (1.0 + sc2_ref[...]) + sh2_ref[...]).astype(h2_ref.dtype)


def _merge(attn, y4, g, x, g1, sh2, sc2, n2, wab, wglu, wout):
    b, l, _ = x.shape
    tt = TOK_TILE
    tok = lambda w: pl.BlockSpec((b, tt, w), lambda i: (0, i, 0))
    per_b = _const_spec((b, 1, D_MODEL))
    y4_spec = pl.BlockSpec((N_LANE_BLOCKS, tt // CHUNK * b, CHUNK_W), lambda i: (0, i, 0))
    return pl.pallas_call(
        _merge_kernel,
        grid=(l // tt,),
        in_specs=[tok(ATTN_DIM), y4_spec, tok(2 * D_MODEL), tok(D_MODEL),
                  per_b, per_b, per_b, _const_spec((1, D_MODEL)),
                  _const_spec(wab.shape), _const_spec(wglu.shape), _const_spec(wout.shape)],
        out_specs=[tok(D_MODEL), tok(D_MODEL)],
        out_shape=[jax.ShapeDtypeStruct((b, l, D_MODEL), F32),
                   jax.ShapeDtypeStruct((b, l, D_MODEL), BF16)],
        scratch_shapes=[pltpu.VMEM((N_LANE_BLOCKS, b * tt, LANES), F32)],
        compiler_params=_params(1),
        name="merge",
    )(attn, y4, g, x, g1, sh2, sc2, n2, wab, wglu, wout)


def _ffn_row_select(tm):
    seg_len = tm // SUB
    tile = np.zeros((tm, tm), np.float32)
    for seg in range(SUB):
        for i in range(seg_len):
            tile[SUB * i + seg, seg * seg_len + i] = 1.0
    halo = np.zeros((SUB, 2 * HALO), np.float32)
    halo[0, HALO - 1] = 1.0
    halo[1, HALO] = 1.0
    return jnp.asarray(tile, dtype=BF16), jnp.asarray(halo, dtype=BF16)


def _ffn_kernel(sel_ref, selh_ref, hp_ref, h_ref, hn_ref, x1_ref, g2_ref, wup_ref, cw_ref, cb_ref,
                wd_ref, fw_ref, o_ref, acc_scr, *, tm):
    j = pl.program_id(1)
    keep_prev = (j > 0).astype(F32)
    keep_next = (j < pl.num_programs(1) - 1).astype(F32)
    halo = jnp.concatenate([hp_ref[0], hn_ref[0]], axis=0)
    lhs = jnp.concatenate([_bdot(selh_ref[...], halo), _bdot(sel_ref[...], h_ref[0])],
                          axis=0).astype(BF16)
    sub = lax.broadcasted_iota(jnp.int32, (SUB, 1), 0)

    def up(f):
        return (_bdot(lhs, wup_ref[:, f * FF_CHUNK:(f + 1) * FF_CHUNK]),
                _bdot(lhs, wup_ref[:, D_FF + f * FF_CHUNK:D_FF + (f + 1) * FF_CHUNK]))

    def conv(zall, col):
        z = zall[SUB:]
        before = jnp.where(sub == 0, zall[0:1] * keep_prev, pltpu.roll(z[tm - SUB:], 1, 0))
        after = jnp.where(sub == SUB - 1, zall[1:2] * keep_next, pltpu.roll(z[:SUB], SUB - 1, 0))
        z_prev = jnp.concatenate([before, z[:tm - SUB]], axis=0)
        z_next = jnp.concatenate([z[SUB:], after], axis=0)
        cw = cw_ref[:, col:col + FF_CHUNK]
        return z_prev * cw[0:1] + z * cw[1:2] + z_next * cw[2:3] + cb_ref[:, col:col + FF_CHUNK]

    acts = []
    cur = up(0)
    for f in range(N_FF_CHUNKS):
        nxt = up(f + 1) if f + 1 < N_FF_CHUNKS else None
        val = conv(cur[0], f * FF_CHUNK)
        gate = conv(cur[1], D_FF + f * FF_CHUNK)
        acts.append((gate * _sigmoid(gate) * val).astype(BF16))
        cur = nxt
    acc = _bdot(jnp.concatenate(acts, axis=-1), wd_ref[...])
    seg_len = tm // SUB
    for k in range(D_MODEL // LANES):
        acc_scr[k] = acc[:, k * LANES:(k + 1) * LANES]
        for seg in range(SUB):
            o_ref[0, seg * seg_len:(seg + 1) * seg_len, k * LANES:(k + 1) * LANES] = (
                acc_scr[k, pl.ds(seg, seg_len, stride=SUB), :])
    x2 = x1_ref[0] + g2_ref[0] * o_ref[0]
    o_ref[0] = _rms(x2) * fw_ref[...]


def _conv_ffn(h2, x1, g2, wup, cw, cb, wd, fw):
    b, l, _ = x1.shape
    tm = TM_FFN
    nh = tm // HALO
    last = l // HALO - 1
    tok = lambda: pl.BlockSpec((1, tm, D_MODEL), lambda i, j: (i, j, 0))
    prev = pl.BlockSpec((1, HALO, D_MODEL), lambda i, j: (i, jnp.maximum(j * nh - 1, 0), 0))
    nxt = pl.BlockSpec((1, HALO, D_MODEL), lambda i, j: (i, jnp.minimum((j + 1) * nh, last), 0))
    per_b = pl.BlockSpec((1, 1, D_MODEL), lambda i, j: (i, 0, 0))
    once = lambda shape: pl.BlockSpec(shape, lambda *_: (0,) * len(shape),
                                      pipeline_mode=pl.Buffered(1))
    sel, selh = _ffn_row_select(tm)
    return pl.pallas_call(
        functools.partial(_ffn_kernel, tm=tm),
        grid=(b, l // tm),
        in_specs=[once(sel.shape), once(selh.shape), prev, tok(), nxt, tok(), per_b,
                  once(wup.shape), once(cw.shape), once(cb.shape), once(wd.shape),
                  _const_spec((1, D_MODEL))],
        out_specs=tok(),
        out_shape=jax.ShapeDtypeStruct((b, l, D_MODEL), F32),
        scratch_shapes=[pltpu.VMEM((D_MODEL // LANES, tm, LANES), F32)],
        compiler_params=_params(2),
        name="conv_ffn",
    )(sel, selh, h2, h2, h2, x1, g2, wup, cw, cb, wd, fw)


def _rope_tables(l):
    rows = jnp.repeat(jnp.arange(l // GRID_W, dtype=F32), GRID_W)
    cols = jnp.tile(jnp.arange(GRID_W, dtype=F32), l // GRID_W)
    inv_freq = ROPE_THETA ** (-jnp.arange(0, ROPE_AXIS_DIM, 2, dtype=F32) / ROPE_AXIS_DIM)
    ang = jnp.concatenate([rows[:, None] * inv_freq, cols[:, None] * inv_freq], axis=-1)
    cos = jnp.repeat(jnp.cos(ang), 2, axis=-1)
    sin = jnp.repeat(jnp.sin(ang), 2, axis=-1)
    odd = (jnp.arange(HEAD_DIM) % 2 == 1)[None, :]
    return cos, jnp.where(odd, sin, 0.0), jnp.where(odd, 0.0, -sin)


def kernel(x, c, ctx, c_ctx, w_mod, b_mod, norm1_w, norm2_w, w_in, q_norm_w, k_norm_w, w_attn_br,
           ssm_lambda_re, ssm_lambda_im, ssm_log_dt, ssm_b_re, ssm_b_im, ssm_c_re, ssm_c_im, ssm_d,
           w_glu, w_out, w_up, conv_w, conv_b, w_down, final_norm_w):
    b, l, d = x.shape
    lc = ctx.shape[1]
    assert w_mod.shape[0] == 1 and d == D_MODEL and l % TOK_TILE == 0 and lc % TOK_TILE == 0
    assert lc // CHUNK * b == S5_STATE_ROWS and (l // CHUNK * b) % S5_OUT_ROWS == 0
    layer = 0

    pad = (-(b + 1)) % 8
    c_rows = jnp.concatenate([c, c_ctx[None, :], jnp.zeros((pad, d), F32)], axis=0)
    mod = _modulation(c_rows, w_mod[layer], b_mod[layer])
    sh1, sc1, g1, sh2, sc2, g2 = [m[:b, None, :] for m in jnp.split(mod, N_MOD, axis=-1)]
    csh1, csc1 = mod[b:b + 1, None, :d], mod[b:b + 1, None, d:2 * d]

    w_in_b = w_in[layer].astype(BF16)
    qn = (q_norm_w[layer] * (1.0 / math.sqrt(HEAD_DIM))).reshape(1, HEAD_DIM)
    kn = k_norm_w[layer].reshape(1, HEAD_DIM)
    n1 = norm1_w[layer].reshape(1, d)

    no_rope = (jnp.ones((lc, HEAD_DIM), F32),) + (jnp.zeros((lc, HEAD_DIM), F32),) * 2
    q, k, v, u4l, g = _inproj(x, sh1, sc1, n1, w_in_b, qn, kn, _rope_tables(l), latent=True)
    kc, vc, u4c = _inproj(ctx, csh1, csc1, n1, w_in_b, qn, kn, no_rope, latent=False)

    attn = _attention(q, kc, k, vc, v)

    m_src, e_src, f_src, a_tab, d_tab = _s5_tables(
        ssm_lambda_re[layer], ssm_lambda_im[layer], ssm_log_dt[layer], ssm_b_re[layer],
        ssm_b_im[layer], ssm_c_re[layer], ssm_c_im[layer], ssm_d[layer])
    m_tab, e_tab, f_tab = _s5_expand(m_src, e_src, f_src)
    sin = _s5_states(u4c, u4l, e_tab, a_tab, b)
    y4 = _s5_outputs(u4l, sin, m_tab, f_tab, d_tab)

    x1, h2 = _merge(attn, y4, g, x, g1, sh2, sc2, norm2_w[layer].reshape(1, d),
                    w_attn_br[layer].astype(BF16), w_glu[layer].astype(BF16),
                    w_out[layer].astype(BF16))

    return _conv_ffn(h2, x1, g2, w_up[layer].astype(BF16), conv_w[layer],
                     conv_b[layer].reshape(1, 2 * D_FF), w_down[layer].astype(BF16),
                     final_norm_w.reshape(1, d))
```

```python
import functools
import math

import jax
import jax.numpy as jnp
import numpy as np
from jax import lax
from jax.experimental import pallas as pl
from jax.experimental.pallas import tpu as pltpu

F32 = jnp.float32
BF16 = jnp.bfloat16

D_MODEL = 1024
GRID_W = 64
N_HEADS = 8
N_KV_HEADS = 2
HEAD_DIM = 128
HEADS_PER_KV = N_HEADS // N_KV_HEADS
ATTN_DIM = N_HEADS * HEAD_DIM
KV_DIM = N_KV_HEADS * HEAD_DIM
ROPE_THETA = 10000.0
ROPE_AXIS_DIM = HEAD_DIM // 2
SSM_DIM = 512
SSM_GROUP = 16
N_SSM_GROUPS = SSM_DIM // SSM_GROUP
SSM_STATE = 64
Q_END = ATTN_DIM
K_END = Q_END + KV_DIM
V_END = K_END + KV_DIM
U_END = V_END + SSM_DIM
D_IN = U_END + 2 * D_MODEL
D_FF = 2816
N_MOD = 6
EPS = 1e-6

LANES = 128
SUB = 8
CHUNK = 8
N_LANE_BLOCKS = SSM_DIM // LANES
GROUPS_PER_BLOCK = LANES // SSM_GROUP
CHUNK_W = CHUNK * LANES
STATE_W = GROUPS_PER_BLOCK * SSM_STATE

FF_CHUNK = 256
N_FF_CHUNKS = D_FF // FF_CHUNK
HALO = 16

VMEM_LIMIT = 56 * 1024 * 1024

TOK_TILE = 32
TQ_ATTN = 1024
TM_FFN = 512
S5_STATE_ROWS = 512
S5_OUT_ROWS = 1024


def _sigmoid(x):
    return 1.0 / (1.0 + jnp.exp(-x))


def _gelu_tanh(x):
    return 0.5 * x * (1.0 + jnp.tanh(math.sqrt(2.0 / math.pi) * (x + 0.044715 * (x * x * x))))


def _rms(x):
    return x * lax.rsqrt(jnp.mean(x * x, axis=-1, keepdims=True) + EPS)


def _bdot(a, b):
    return jnp.dot(a, b, preferred_element_type=F32)


def _const_spec(shape):
    nd = len(shape)
    return pl.BlockSpec(shape, lambda *_: (0,) * nd)


def _params(n_axes):
    return pltpu.CompilerParams(dimension_semantics=("arbitrary",) * n_axes,
                                vmem_limit_bytes=VMEM_LIMIT)


def _mod_kernel(c_ref, w_ref, b_ref, o_ref):
    c = c_ref[...]
    a = c * _sigmoid(c)
    o_ref[...] = jnp.dot(a, w_ref[...], precision=lax.Precision.HIGHEST,
                         preferred_element_type=F32) + b_ref[...]


def _modulation(c_rows, w_mod, b_mod):
    rows = c_rows.shape[0]
    n = w_mod.shape[1]
    tn = 1536
    return pl.pallas_call(
        _mod_kernel,
        grid=(n // tn,),
        in_specs=[pl.BlockSpec((rows, D_MODEL), lambda j: (0, 0)),
                  pl.BlockSpec((D_MODEL, tn), lambda j: (0, j)),
                  pl.BlockSpec((1, tn), lambda j: (0, j))],
        out_specs=pl.BlockSpec((rows, tn), lambda j: (0, j)),
        out_shape=jax.ShapeDtypeStruct((rows, n), F32),
        compiler_params=_params(1),
        name="modulation",
    )(c_rows, w_mod, b_mod.reshape(1, n))


def _norm_rope_store(t, nw, rope, out_ref, col, nb):
    t = _rms(t) * nw
    if rope is not None:
        cos, sin_prev, sin_next = rope
        prev = pltpu.roll(t, 1, 1).reshape(nb, -1, HEAD_DIM)
        nxt = pltpu.roll(t, HEAD_DIM - 1, 1).reshape(nb, -1, HEAD_DIM)
        t3 = t.reshape(nb, -1, HEAD_DIM) * cos + prev * sin_prev + nxt * sin_next
    else:
        t3 = t.reshape(nb, -1, HEAD_DIM)
    out_ref[:, :, col:col + HEAD_DIM] = t3.astype(out_ref.dtype)


def _store_chunk_rows(u, u_scr, u4_ref, nb):
    tokens = u.shape[0] // nb
    for j in range(N_LANE_BLOCKS):
        u_scr[j] = u[:, j * LANES:(j + 1) * LANES]
        for k in range(tokens // CHUNK):
            for s in range(CHUNK):
                u4_ref[j, k * nb:(k + 1) * nb, s * LANES:(s + 1) * LANES] = (
                    u_scr[j, pl.ds(k * CHUNK + s, nb, stride=tokens), :])


def _inproj_kernel(x_ref, sh_ref, sc_ref, n1_ref, w_ref, qn_ref, kn_ref, cos_ref, sinp_ref, sinn_ref,
                   *refs, latent):
    nb, tokens, _ = x_ref.shape
    h = _rms(x_ref[...]) * n1_ref[...]
    h = h * (1.0 + sc_ref[...]) + sh_ref[...]
    hb = h.reshape(nb * tokens, D_MODEL).astype(BF16)
    if latent:
        q_ref, k_ref, v_ref, u4_ref, g_ref, u_scr = refs
        rope = (cos_ref[...], sinp_ref[...], sinn_ref[...])
        q = _bdot(hb, w_ref[:, :Q_END])
        for hd in range(N_HEADS):
            _norm_rope_store(q[:, hd * HEAD_DIM:(hd + 1) * HEAD_DIM], qn_ref[...], rope,
                             q_ref, hd * HEAD_DIM, nb)
    else:
        k_ref, v_ref, u4_ref, u_scr = refs
        rope = None
    k = _bdot(hb, w_ref[:, Q_END:K_END])
    for hd in range(N_KV_HEADS):
        _norm_rope_store(k[:, hd * HEAD_DIM:(hd + 1) * HEAD_DIM], kn_ref[...], rope,
                         k_ref, hd * HEAD_DIM, nb)
    v_ref[...] = _bdot(hb, w_ref[:, K_END:V_END]).reshape(nb, tokens, KV_DIM).astype(v_ref.dtype)
    _store_chunk_rows(_bdot(hb, w_ref[:, V_END:U_END]), u_scr, u4_ref, nb)
    if latent:
        g_ref[...] = _bdot(hb, w_ref[:, U_END:]).reshape(nb, tokens, 2 * D_MODEL)


def _inproj(x, sh1, sc1, n1, w_in_b, qn, kn, rope, latent):
    b, l, _ = x.shape
    tt = TOK_TILE
    tok = lambda w: pl.BlockSpec((b, tt, w), lambda i: (0, i, 0))
    mod_spec = _const_spec(sh1.shape)
    rope_spec = pl.BlockSpec((tt, HEAD_DIM), lambda i: (i, 0))
    u4_spec = pl.BlockSpec((N_LANE_BLOCKS, tt // CHUNK * b, CHUNK_W), lambda i: (0, i, 0))
    kv_u_specs = [tok(KV_DIM), tok(KV_DIM), u4_spec]
    kv_u_shapes = [jax.ShapeDtypeStruct((b, l, KV_DIM), BF16),
                   jax.ShapeDtypeStruct((b, l, KV_DIM), BF16),
                   jax.ShapeDtypeStruct((N_LANE_BLOCKS, l // CHUNK * b, CHUNK_W), F32)]
    if latent:
        out_specs = [tok(ATTN_DIM)] + kv_u_specs + [tok(2 * D_MODEL)]
        out_shape = ([jax.ShapeDtypeStruct((b, l, ATTN_DIM), BF16)] + kv_u_shapes
                     + [jax.ShapeDtypeStruct((b, l, 2 * D_MODEL), F32)])
    else:
        out_specs, out_shape = kv_u_specs, kv_u_shapes
    return pl.pallas_call(
        functools.partial(_inproj_kernel, latent=latent),
        grid=(l // tt,),
        in_specs=[tok(D_MODEL), mod_spec, mod_spec, _const_spec((1, D_MODEL)),
                  _const_spec(w_in_b.shape), _const_spec((1, HEAD_DIM)), _const_spec((1, HEAD_DIM)),
                  rope_spec, rope_spec, rope_spec],
        out_specs=out_specs,
        out_shape=out_shape,
        scratch_shapes=[pltpu.VMEM((N_LANE_BLOCKS, b * tt, LANES), F32)],
        compiler_params=_params(1),
        name="inproj_latent" if latent else "inproj_ctx",
    )(x, sh1, sc1, n1, w_in_b, qn, kn, *rope)


def _attn_kernel(q_ref, kc_ref, k_ref, vc_ref, v_ref, o_ref):
    keys = jnp.concatenate([kc_ref[0], k_ref[0]], axis=0)
    vals = jnp.concatenate([vc_ref[0], v_ref[0]], axis=0)
    vals = jnp.concatenate([vals, jnp.ones_like(vals)], axis=-1)
    nt = (((1,), (1,)), ((), ()))

    def scores(r):
        q = q_ref[0, :, r * HEAD_DIM:(r + 1) * HEAD_DIM]
        return lax.dot_general(q, keys, nt, preferred_element_type=F32)

    cur = scores(0)
    for r in range(HEADS_PER_KV):
        nxt = scores(r + 1) if r + 1 < HEADS_PER_KV else None
        p = jnp.exp(cur - jnp.max(cur, axis=-1, keepdims=True)).astype(BF16)
        o = _bdot(p, vals)
        o_ref[0, :, r * HEAD_DIM:(r + 1) * HEAD_DIM] = (
            o[:, :HEAD_DIM] / o[:, HEAD_DIM:]).astype(o_ref.dtype)
        cur = nxt


def _attention(q, kc, k, vc, v):
    b, l, _ = q.shape
    lc = kc.shape[1]
    tq = TQ_ATTN
    qspec = pl.BlockSpec((1, tq, HEADS_PER_KV * HEAD_DIM), lambda i, h, j: (i, j, h))
    kv = lambda n: pl.BlockSpec((1, n, HEAD_DIM), lambda i, h, j: (i, 0, h))
    return pl.pallas_call(
        _attn_kernel,
        grid=(b, N_KV_HEADS, l // tq),
        in_specs=[qspec, kv(lc), kv(l), kv(lc), kv(l)],
        out_specs=qspec,
        out_shape=jax.ShapeDtypeStruct((b, l, ATTN_DIM), BF16),
        compiler_params=_params(3),
        name="attention",
    )(q, kc, k, vc, v)


def _s5_discretise(lam_re, lam_im, log_dt, b_re, b_im):
    dt = jnp.exp(log_dt)[..., None]
    mag = jnp.exp(lam_re * dt)
    ang = lam_im * dt
    a_re = mag * jnp.cos(ang)
    a_im = mag * jnp.sin(ang)
    den = lam_re * lam_re + lam_im * lam_im
    nr = a_re - 1.0
    ni = a_im
    f_re = ((nr * lam_re + ni * lam_im) / den)[:, :, None, :]
    f_im = ((ni * lam_re - nr * lam_im) / den)[:, :, None, :]
    bt_re = b_re.transpose(0, 1, 3, 2)
    bt_im = b_im.transpose(0, 1, 3, 2)
    return (a_re[:, :, None, :], a_im[:, :, None, :],
            f_re * bt_re - f_im * bt_im, f_re * bt_im + f_im * bt_re)


def _s5_table_kernel(a_re_ref, a_im_ref, bb_re_ref, bb_im_ref, c_re_ref, c_im_ref,
                     m_ref, e_ref, f_ref, ap_ref, x_scr, e_scr, f_scr, c_scr):
    hp = lax.Precision.HIGHEST
    i32 = jnp.int32
    na, p_, t_ = GROUPS_PER_BLOCK, SSM_GROUP, CHUNK
    nt = (((1,), (1,)), ((), ()))
    for d in range(2):
        for a in range(na):
            a_re, a_im = a_re_ref[d, a], a_im_ref[d, a]
            b_re, b_im = bb_re_ref[d, a], bb_im_ref[d, a]
            c_re, c_im = c_re_ref[d, a], c_im_ref[d, a]
            c_scr[d, 0, a * p_:(a + 1) * p_, :] = c_re
            c_scr[d, 1, a * p_:(a + 1) * p_, :] = c_im
            pw_re, pw_im = jnp.ones_like(a_re), jnp.zeros_like(a_im)
            for tau in range(t_ + 1):
                if tau < t_:
                    pb_re = b_re * pw_re - b_im * pw_im
                    pb_im = b_re * pw_im + b_im * pw_re
                    r0 = (tau * na + a) * p_
                    x_scr[d, 0, r0:r0 + p_, :] = pb_re
                    x_scr[d, 1, r0:r0 + p_, :] = pb_im
                    s_in = t_ - 1 - tau if d == 0 else tau
                    r0 = (s_in * na + a) * p_
                    e_scr[d, 0, r0:r0 + p_, :] = pb_re
                    e_scr[d, 1, r0:r0 + p_, :] = pb_im
                if tau > 0:
                    t_out = tau - 1 if d == 0 else t_ - tau
                    r0 = (t_out * na + a) * p_
                    f_scr[d, 0, r0:r0 + p_, :] = c_re * pw_re - c_im * pw_im
                    f_scr[d, 1, r0:r0 + p_, :] = -(c_re * pw_im + c_im * pw_re)
                if tau == t_:
                    ap_ref[d, 0, 0, a] = pw_re
                    ap_ref[d, 0, 1, a] = pw_im
                else:
                    pw_re, pw_im = pw_re * a_re - pw_im * a_im, pw_re * a_im + pw_im * a_re

    grp_bits = p_.bit_length() - 1
    st_bits = SSM_STATE.bit_length() - 1
    blk_bits = na.bit_length() - 1
    gmask = na - 1
    r_w = lax.broadcasted_iota(i32, (CHUNK_W, 1), 0)
    c_w = lax.broadcasted_iota(i32, (1, CHUNK_W), 1)
    c_l = lax.broadcasted_iota(i32, (1, LANES), 1)
    r_n = lax.broadcasted_iota(i32, (SSM_STATE, 1), 0)
    c_n = lax.broadcasted_iota(i32, (1, SSM_STATE), 1)
    grp_tok_r = (r_w >> grp_bits) & gmask
    grp_tok_c = (c_w >> grp_bits) & gmask
    grp_st_r = (r_w >> st_bits) & gmask
    grp_st_c = (c_w >> st_bits) & gmask

    lag = []
    for d in range(2):
        full = (lax.dot_general(x_scr[d, 0], c_scr[d, 0], nt, precision=hp, preferred_element_type=F32)
                - lax.dot_general(x_scr[d, 1], c_scr[d, 1], nt, precision=hp, preferred_element_type=F32))
        lag.append(jnp.where(grp_tok_r == (c_l >> grp_bits), full, 0.0))
    for s in range(t_):
        for t in range(t_):
            if t > s:
                blk = lag[0][(t - s) * LANES:(t - s + 1) * LANES]
            elif s > t:
                blk = lag[1][(s - t) * LANES:(s - t + 1) * LANES]
            else:
                blk = lag[0][:LANES] + lag[1][:LANES]
            m_ref[0, s * LANES:(s + 1) * LANES, t * LANES:(t + 1) * LANES] = blk.astype(m_ref.dtype)

    for d in range(2):
        e_full = jnp.zeros((CHUNK_W, 2 * STATE_W), F32)
        f_full = jnp.zeros((2 * STATE_W, CHUNK_W), F32)
        for c in range(2):
            rep = (((c_w >> (st_bits + blk_bits)) == c) & ((c_w & (SSM_STATE - 1)) == r_n))
            rep = rep.astype(F32).astype(BF16)
            rep_t = (((r_w >> (st_bits + blk_bits)) == c) & ((r_w & (SSM_STATE - 1)) == c_n))
            rep_t = rep_t.astype(F32).astype(BF16)
            e_full = e_full + _bdot(e_scr[d, c].astype(BF16), rep)
            f_full = f_full + lax.dot_general(rep_t, f_scr[d, c].astype(BF16), nt,
                                              preferred_element_type=F32)
        e_ref[d, 0] = jnp.where(grp_tok_r == grp_st_c, e_full, 0.0).astype(e_ref.dtype)
        f_ref[d, 0] = jnp.where(grp_st_r == grp_tok_c, f_full, 0.0).astype(f_ref.dtype)


def _s5_tables(a_re, a_im, bb_re, bb_im, c_re, c_im):
    nj, na = N_LANE_BLOCKS, GROUPS_PER_BLOCK
    grp = lambda r: pl.BlockSpec((2, na, r, SSM_STATE), lambda j: (0, j, 0, 0))
    dj = lambda r, c: pl.BlockSpec((2, 1, r, c), lambda j: (0, j, 0, 0))
    src = lambda: pltpu.VMEM((2, 2, CHUNK_W, SSM_STATE), F32)
    return pl.pallas_call(
        _s5_table_kernel,
        grid=(nj,),
        in_specs=[grp(1), grp(1), grp(SSM_GROUP), grp(SSM_GROUP), grp(SSM_GROUP), grp(SSM_GROUP)],
        out_specs=[pl.BlockSpec((1, CHUNK_W, CHUNK_W), lambda j: (j, 0, 0)),
                   dj(CHUNK_W, 2 * STATE_W), dj(2 * STATE_W, CHUNK_W),
                   pl.BlockSpec((2, 1, 2, na, 1, SSM_STATE), lambda j: (0, j, 0, 0, 0, 0))],
        out_shape=[jax.ShapeDtypeStruct((nj, CHUNK_W, CHUNK_W), BF16),
                   jax.ShapeDtypeStruct((2, nj, CHUNK_W, 2 * STATE_W), BF16),
                   jax.ShapeDtypeStruct((2, nj, 2 * STATE_W, CHUNK_W), BF16),
                   jax.ShapeDtypeStruct((2, nj, 2, na, 1, SSM_STATE), F32)],
        scratch_shapes=[src(), src(), src(), pltpu.VMEM((2, 2, LANES, SSM_STATE), F32)],
        compiler_params=_params(1),
        name="s5_tables",
    )(a_re, a_im, bb_re, bb_im, c_re, c_im)


def _s5_state_kernel(uc_ref, ul_ref, e_ref, a_ref, sin_ref, c_scr, sre_scr, sim_scr, *, nb, n_chunks):
    d = pl.program_id(1)
    r = pl.program_id(2)
    a_re = jnp.broadcast_to(a_ref[0, 0, 0], (nb, STATE_W))
    a_im = jnp.broadcast_to(a_ref[0, 0, 1], (nb, STATE_W))

    def scan(store):
        def body(i, s):
            kk = jnp.where(d == 0, i, n_chunks - 1 - i)
            row = pl.multiple_of(kk * nb, nb)
            s_re, s_im = s
            if store:
                sin_ref[0, 0, pl.ds(row, nb), 0:STATE_W] = s_re.astype(sin_ref.dtype)
                sin_ref[0, 0, pl.ds(row, nb), STATE_W:] = s_im.astype(sin_ref.dtype)
            c_re = c_scr[pl.ds(row, nb), 0:STATE_W]
            c_im = c_scr[pl.ds(row, nb), STATE_W:]
            return (s_re * a_re - s_im * a_im + c_re, s_re * a_im + s_im * a_re + c_im)

        s = lax.fori_loop(0, n_chunks, body, (sre_scr[...], sim_scr[...]))
        sre_scr[...] = s[0]
        sim_scr[...] = s[1]

    @pl.when(r == 0)
    def _():
        sre_scr[...] = jnp.zeros_like(sre_scr)
        sim_scr[...] = jnp.zeros_like(sim_scr)
        c_scr[...] = _bdot(uc_ref[0].astype(BF16), e_ref[0, 0])
        scan(False)

    @pl.when(r > 0)
    def _():
        c_scr[...] = _bdot(ul_ref[0].astype(BF16), e_ref[0, 0])
        scan(True)


def _s5_states(u4c, u4l, e_tab, a_tab, nb):
    rows = S5_STATE_ROWS
    assert u4c.shape[1] == rows
    n_tiles = u4l.shape[1] // rows

    def lat_tile(d, r):
        t = jnp.maximum(r - 1, 0)
        return jnp.where(d == 0, t, n_tiles - 1 - t)

    kern = functools.partial(_s5_state_kernel, nb=nb, n_chunks=rows // nb)
    return pl.pallas_call(
        kern,
        grid=(N_LANE_BLOCKS, 2, n_tiles + 1),
        in_specs=[pl.BlockSpec((1, rows, CHUNK_W), lambda j, d, r: (j, 0, 0)),
                  pl.BlockSpec((1, rows, CHUNK_W), lambda j, d, r: (j, lat_tile(d, r), 0)),
                  pl.BlockSpec((1, 1, CHUNK_W, 2 * STATE_W), lambda j, d, r: (d, j, 0, 0)),
                  pl.BlockSpec((1, 1, 2, 1, STATE_W), lambda j, d, r: (d, j, 0, 0, 0))],
        out_specs=pl.BlockSpec((1, 1, rows, 2 * STATE_W), lambda j, d, r: (d, j, lat_tile(d, r), 0)),
        out_shape=jax.ShapeDtypeStruct((2, N_LANE_BLOCKS, u4l.shape[1], 2 * STATE_W), BF16),
        scratch_shapes=[pltpu.VMEM((rows, 2 * STATE_W), F32),
                        pltpu.VMEM((nb, STATE_W), F32), pltpu.VMEM((nb, STATE_W), F32)],
        compiler_params=_params(3),
        name="s5_states",
    )(u4c, u4l, e_tab, a_tab)


def _s5_out_kernel(ul_ref, sin_ref, m_ref, f_ref, d_ref, y_ref):
    u = ul_ref[0]
    y_ref[0] = (u * d_ref[0] + _bdot(u.astype(BF16), m_ref[0])
                + _bdot(sin_ref[0, 0], f_ref[0, 0]) + _bdot(sin_ref[1, 0], f_ref[1, 0]))


def _s5_outputs(u4l, sin, m_tab, f_tab, d_tab):
    rows = S5_OUT_ROWS
    return pl.pallas_call(
        _s5_out_kernel,
        grid=(N_LANE_BLOCKS, u4l.shape[1] // rows),
        in_specs=[pl.BlockSpec((1, rows, CHUNK_W), lambda j, r: (j, r, 0)),
                  pl.BlockSpec((2, 1, rows, 2 * STATE_W), lambda j, r: (0, j, r, 0)),
                  pl.BlockSpec((1, CHUNK_W, CHUNK_W), lambda j, r: (j, 0, 0)),
                  pl.BlockSpec((2, 1, 2 * STATE_W, CHUNK_W), lambda j, r: (0, j, 0, 0)),
                  pl.BlockSpec((1, 1, CHUNK_W), lambda j, r: (j, 0, 0))],
        out_specs=pl.BlockSpec((1, rows, CHUNK_W), lambda j, r: (j, r, 0)),
        out_shape=jax.ShapeDtypeStruct(u4l.shape, F32),
        compiler_params=_params(2),
        name="s5_outputs",
    )(u4l, sin, m_tab, f_tab, d_tab)


def _merge_kernel(attn_ref, y4_ref, g_ref, x_ref, g1_ref, sh2_ref, sc2_ref, n2_ref,
                  wab_ref, wglu_ref, wout_ref, x1_ref, h2_ref, ys_scr):
    nb, tokens, _ = x_ref.shape
    rows = nb * tokens
    for j in range(N_LANE_BLOCKS):
        for k in range(tokens // CHUNK):
            for s in range(CHUNK):
                ys_scr[j, pl.ds(k * CHUNK + s, nb, stride=tokens), :] = (
                    y4_ref[j, k * nb:(k + 1) * nb, s * LANES:(s + 1) * LANES])
    y_ssm = jnp.concatenate([ys_scr[j] for j in range(N_LANE_BLOCKS)], axis=-1)
    p_attn = _bdot(attn_ref[...].reshape(rows, ATTN_DIM), wab_ref[...])
    glu = _bdot(_gelu_tanh(y_ssm).astype(BF16), wglu_ref[...])
    p_ssm = glu[:, :D_MODEL] * _sigmoid(glu[:, D_MODEL:])
    g = g_ref[...].reshape(rows, 2 * D_MODEL)
    mix = _sigmoid(g[:, :D_MODEL]) * p_attn + _sigmoid(g[:, D_MODEL:]) * p_ssm
    x_mix = _bdot(mix.astype(BF16), wout_ref[...]).reshape(nb, tokens, D_MODEL)
    x1 = x_ref[...] + g1_ref[...] * x_mix
    x1_ref[...] = x1
    h2 = _rms(x1) * n2_ref[...]
    h2_ref[...] = (h2 * (1.0 + sc2_ref[...]) + sh2_ref[...]).astype(h2_ref.dtype)


def _merge(attn, y4, g, x, g1, sh2, sc2, n2, wab, wglu, wout):
    b, l, _ = x.shape
    tt = TOK_TILE
    tok = lambda w: pl.BlockSpec((b, tt, w), lambda i: (0, i, 0))
    per_b = _const_spec((b, 1, D_MODEL))
    y4_spec = pl.BlockSpec((N_LANE_BLOCKS, tt // CHUNK * b, CHUNK_W), lambda i: (0, i, 0))
    return pl.pallas_call(
        _merge_kernel,
        grid=(l // tt,),
        in_specs=[tok(ATTN_DIM), y4_spec, tok(2 * D_MODEL), tok(D_MODEL),
                  per_b, per_b, per_b, _const_spec((1, D_MODEL)),
                  _const_spec(wab.shape), _const_spec(wglu.shape), _const_spec(wout.shape)],
        out_specs=[tok(D_MODEL), tok(D_MODEL)],
        out_shape=[jax.ShapeDtypeStruct((b, l, D_MODEL), F32),
                   jax.ShapeDtypeStruct((b, l, D_MODEL), BF16)],
        scratch_shapes=[pltpu.VMEM((N_LANE_BLOCKS, b * tt, LANES), F32)],
        compiler_params=_params(1),
        name="merge",
    )(attn, y4, g, x, g1, sh2, sc2, n2, wab, wglu, wout)


def _ffn_row_select(tm):
    seg_len = tm // SUB
    tile = np.zeros((tm, tm), np.float32)
    for seg in range(SUB):
        for i in range(seg_len):
            tile[SUB * i + seg, seg * seg_len + i] = 1.0
    halo = np.zeros((SUB, 2 * HALO), np.float32)
    halo[0, HALO - 1] = 1.0
    halo[1, HALO] = 1.0
    return jnp.asarray(tile, dtype=BF16), jnp.asarray(halo, dtype=BF16)


def _ffn_kernel(sel_ref, selh_ref, hp_ref, h_ref, hn_ref, x1_ref, g2_ref, wup_ref, cw_ref, cb_ref,
                wd_ref, fw_ref, o_ref, acc_scr, *, tm):
    j = pl.program_id(1)
    keep_prev = (j > 0).astype(F32)
    keep_next = (j < pl.num_programs(1) - 1).astype(F32)
    halo = jnp.concatenate([hp_ref[0], hn_ref[0]], axis=0)
    lhs = jnp.concatenate([_bdot(selh_ref[...], halo), _bdot(sel_ref[...], h_ref[0])],
                          axis=0).astype(BF16)
    sub = lax.broadcasted_iota(jnp.int32, (SUB, 1), 0)

    def up(f):
        return (_bdot(lhs, wup_ref[:, f * FF_CHUNK:(f + 1) * FF_CHUNK]),
                _bdot(lhs, wup_ref[:, D_FF + f * FF_CHUNK:D_FF + (f + 1) * FF_CHUNK]))

    def conv(zall, col):
        z = zall[SUB:]
        before = jnp.where(sub == 0, zall[0:1] * keep_prev, pltpu.roll(z[tm - SUB:], 1, 0))
        after = jnp.where(sub == SUB - 1, zall[1:2] * keep_next, pltpu.roll(z[:SUB], SUB - 1, 0))
        z_prev = jnp.concatenate([before, z[:tm - SUB]], axis=0)
        z_next = jnp.concatenate([z[SUB:], after], axis=0)
        cw = cw_ref[:, col:col + FF_CHUNK]
        return z_prev * cw[0:1] + z * cw[1:2] + z_next * cw[2:3] + cb_ref[:, col:col + FF_CHUNK]

    acts = []
    cur = up(0)
    for f in range(N_FF_CHUNKS):
        nxt = up(f + 1) if f + 1 < N_FF_CHUNKS else None
        val = conv(cur[0], f * FF_CHUNK)
        gate = conv(cur[1], D_FF + f * FF_CHUNK)
        acts.append((gate * _sigmoid(gate) * val).astype(BF16))
        cur = nxt
    acc = _bdot(jnp.concatenate(acts, axis=-1), wd_ref[...])
    seg_len = tm // SUB
    for k in range(D_MODEL // LANES):
        acc_scr[k] = acc[:, k * LANES:(k + 1) * LANES]
        for seg in range(SUB):
            o_ref[0, seg * seg_len:(seg + 1) * seg_len, k * LANES:(k + 1) * LANES] = (
                acc_scr[k, pl.ds(seg, seg_len, stride=SUB), :])
    x2 = x1_ref[0] + g2_ref[0] * o_ref[0]
    o_ref[0] = _rms(x2) * fw_ref[...]


def _conv_ffn(h2, x1, g2, wup, cw, cb, wd, fw):
    b, l, _ = x1.shape
    tm = TM_FFN
    nh = tm // HALO
    last = l // HALO - 1
    tok = lambda: pl.BlockSpec((1, tm, D_MODEL), lambda i, j: (i, j, 0))
    prev = pl.BlockSpec((1, HALO, D_MODEL), lambda i, j: (i, jnp.maximum(j * nh - 1, 0), 0))
    nxt = pl.BlockSpec((1, HALO, D_MODEL), lambda i, j: (i, jnp.minimum((j + 1) * nh, last), 0))
    per_b = pl.BlockSpec((1, 1, D_MODEL), lambda i, j: (i, 0, 0))
    once = lambda shape: pl.BlockSpec(shape, lambda *_: (0,) * len(shape),
                                      pipeline_mode=pl.Buffered(1))
    sel, selh = _ffn_row_select(tm)
    return pl.pallas_call(
        functools.partial(_ffn_kernel, tm=tm),
        grid=(b, l // tm),
        in_specs=[once(sel.shape), once(selh.shape), prev, tok(), nxt, tok(), per_b,
                  once(wup.shape), once(cw.shape), once(cb.shape), once(wd.shape),
                  _const_spec((1, D_MODEL))],
        out_specs=tok(),
        out_shape=jax.ShapeDtypeStruct((b, l, D_MODEL), F32),
        scratch_shapes=[pltpu.VMEM((D_MODEL // LANES, tm, LANES), F32)],
        compiler_params=_params(2),
        name="conv_ffn",
    )(sel, selh, h2, h2, h2, x1, g2, wup, cw, cb, wd, fw)


def _rope_tables(l):
    rows = jnp.repeat(jnp.arange(l // GRID_W, dtype=F32), GRID_W)
    cols = jnp.tile(jnp.arange(GRID_W, dtype=F32), l // GRID_W)
    inv_freq = ROPE_THETA ** (-jnp.arange(0, ROPE_AXIS_DIM, 2, dtype=F32) / ROPE_AXIS_DIM)
    ang = jnp.concatenate([rows[:, None] * inv_freq, cols[:, None] * inv_freq], axis=-1)
    cos = jnp.repeat(jnp.cos(ang), 2, axis=-1)
    sin = jnp.repeat(jnp.sin(ang), 2, axis=-1)
    odd = (jnp.arange(HEAD_DIM) % 2 == 1)[None, :]
    return cos, jnp.where(odd, sin, 0.0), jnp.where(odd, 0.0, -sin)


def kernel(x, c, ctx, c_ctx, w_mod, b_mod, norm1_w, norm2_w, w_in, q_norm_w, k_norm_w, w_attn_br,
           ssm_lambda_re, ssm_lambda_im, ssm_log_dt, ssm_b_re, ssm_b_im, ssm_c_re, ssm_c_im, ssm_d,
           w_glu, w_out, w_up, conv_w, conv_b, w_down, final_norm_w):
    b, l, d = x.shape
    lc = ctx.shape[1]
    assert w_mod.shape[0] == 1 and d == D_MODEL and l % TOK_TILE == 0 and lc % TOK_TILE == 0
    assert lc // CHUNK * b == S5_STATE_ROWS and (l // CHUNK * b) % S5_OUT_ROWS == 0
    layer = 0

    pad = (-(b + 1)) % 8
    c_rows = jnp.concatenate([c, c_ctx[None, :], jnp.zeros((pad, d), F32)], axis=0)
    mod = _modulation(c_rows, w_mod[layer], b_mod[layer])
    sh1, sc1, g1, sh2, sc2, g2 = [m[:b, None, :] for m in jnp.split(mod, N_MOD, axis=-1)]
    csh1, csc1 = mod[b:b + 1, None, :d], mod[b:b + 1, None, d:2 * d]

    w_in_b = w_in[layer].astype(BF16)
    qn = (q_norm_w[layer] * (1.0 / math.sqrt(HEAD_DIM))).reshape(1, HEAD_DIM)
    kn = k_norm_w[layer].reshape(1, HEAD_DIM)
    n1 = norm1_w[layer].reshape(1, d)

    no_rope = (jnp.ones((lc, HEAD_DIM), F32),) + (jnp.zeros((lc, HEAD_DIM), F32),) * 2
    q, k, v, u4l, g = _inproj(x, sh1, sc1, n1, w_in_b, qn, kn, _rope_tables(l), latent=True)
    kc, vc, u4c = _inproj(ctx, csh1, csc1, n1, w_in_b, qn, kn, no_rope, latent=False)

    attn = _attention(q, kc, k, vc, v)

    disc = _s5_discretise(ssm_lambda_re[layer], ssm_lambda_im[layer], ssm_log_dt[layer],
                          ssm_b_re[layer], ssm_b_im[layer])
    m_tab, e_tab, f_tab, a_pow = _s5_tables(*disc, ssm_c_re[layer], ssm_c_im[layer])
    a_tab = a_pow.reshape(2, N_LANE_BLOCKS, 2, 1, STATE_W)
    d_tab = jnp.tile(ssm_d[layer].reshape(N_LANE_BLOCKS, 1, LANES), (1, CHUNK, 1))
    d_tab = d_tab.reshape(N_LANE_BLOCKS, 1, CHUNK_W)
    sin = _s5_states(u4c, u4l, e_tab, a_tab, b)
    y4 = _s5_outputs(u4l, sin, m_tab, f_tab, d_tab)

    x1, h2 = _merge(attn, y4, g, x, g1, sh2, sc2, norm2_w[layer].reshape(1, d),
                    w_attn_br[layer].astype(BF16), w_glu[layer].astype(BF16),
                    w_out[layer].astype(BF16))

    return _conv_ffn(h2, x1, g2, w_up[layer].astype(BF16), conv_w[layer],
                     conv_b[layer].reshape(1, 2 * D_FF), w_down[layer].astype(BF16),
                     final_norm_w.reshape(1, d))
```

```python
import functools
import math

import jax
import jax.numpy as jnp
import numpy as np
from jax import lax
from jax.experimental import pallas as pl
from jax.experimental.pallas import tpu as pltpu

F32 = jnp.float32
BF16 = jnp.bfloat16

D_MODEL = 1024
GRID_W = 64
N_HEADS = 8
N_KV_HEADS = 2
HEAD_DIM = 128
HEADS_PER_KV = N_HEADS // N_KV_HEADS
ATTN_DIM = N_HEADS * HEAD_DIM
KV_DIM = N_KV_HEADS * HEAD_DIM
ROPE_THETA = 10000.0
ROPE_AXIS_DIM = HEAD_DIM // 2
SSM_DIM = 512
SSM_GROUP = 16
N_SSM_GROUPS = SSM_DIM // SSM_GROUP
SSM_STATE = 64
Q_END = ATTN_DIM
K_END = Q_END + KV_DIM
V_END = K_END + KV_DIM
U_END = V_END + SSM_DIM
D_IN = U_END + 2 * D_MODEL
D_FF = 2816
N_MOD = 6
EPS = 1e-6

LANES = 128
SUB = 8
CHUNK = 8
N_LANE_BLOCKS = SSM_DIM // LANES
GROUPS_PER_BLOCK = LANES // SSM_GROUP
CHUNK_W = CHUNK * LANES
STATE_W = GROUPS_PER_BLOCK * SSM_STATE

FF_CHUNK = 256
N_FF_CHUNKS = D_FF // FF_CHUNK
HALO = 16

VMEM_LIMIT = 56 * 1024 * 1024

TOK_TILE = 32
TQ_ATTN = 1024
MERGE_SPLIT = 2
TM_FFN = 512
S5_STATE_ROWS = 512
S5_OUT_ROWS = 1024


def _sigmoid(x):
    return 1.0 / (1.0 + jnp.exp(-x))


def _gelu_tanh(x):
    return 0.5 * x * (1.0 + jnp.tanh(math.sqrt(2.0 / math.pi) * (x + 0.044715 * (x * x * x))))


def _rms(x):
    return x * lax.rsqrt(jnp.mean(x * x, axis=-1, keepdims=True) + EPS)


def _bdot(a, b):
    return jnp.dot(a, b, preferred_element_type=F32)


def _const_spec(shape):
    nd = len(shape)
    return pl.BlockSpec(shape, lambda *_: (0,) * nd)


def _params(n_axes):
    return pltpu.CompilerParams(dimension_semantics=("arbitrary",) * n_axes,
                                vmem_limit_bytes=VMEM_LIMIT)


def _mod_kernel(c_ref, w_ref, b_ref, o_ref):
    c = c_ref[...]
    a = c * _sigmoid(c)
    o_ref[...] = jnp.dot(a, w_ref[...], precision=lax.Precision.HIGHEST,
                         preferred_element_type=F32) + b_ref[...]


def _modulation(c_rows, w_mod, b_mod):
    rows = c_rows.shape[0]
    n = w_mod.shape[1]
    tn = 1536
    return pl.pallas_call(
        _mod_kernel,
        grid=(n // tn,),
        in_specs=[pl.BlockSpec((rows, D_MODEL), lambda j: (0, 0)),
                  pl.BlockSpec((D_MODEL, tn), lambda j: (0, j)),
                  pl.BlockSpec((1, tn), lambda j: (0, j))],
        out_specs=pl.BlockSpec((rows, tn), lambda j: (0, j)),
        out_shape=jax.ShapeDtypeStruct((rows, n), F32),
        compiler_params=_params(1),
        name="modulation",
    )(c_rows, w_mod, b_mod.reshape(1, n))


def _norm_rope_store(t, nw, rope, out_ref, col, nb):
    t = _rms(t) * nw
    if rope is not None:
        cos, sin_prev, sin_next = rope
        prev = pltpu.roll(t, 1, 1).reshape(nb, -1, HEAD_DIM)
        nxt = pltpu.roll(t, HEAD_DIM - 1, 1).reshape(nb, -1, HEAD_DIM)
        t3 = t.reshape(nb, -1, HEAD_DIM) * cos + prev * sin_prev + nxt * sin_next
    else:
        t3 = t.reshape(nb, -1, HEAD_DIM)
    out_ref[:, :, col:col + HEAD_DIM] = t3.astype(out_ref.dtype)


def _chunk_row_select(nb, tokens, inverse=False):
    sel = np.zeros((nb * tokens, nb * tokens), np.float32)
    for b in range(nb):
        for t in range(tokens):
            k, s = divmod(t, CHUNK)
            sel[(s * (tokens // CHUNK) + k) * nb + b, b * tokens + t] = 1.0
    return jnp.asarray(sel.T if inverse else sel, dtype=BF16)


def _store_chunk_rows(u_perm, u4_ref):
    rows = u_perm.shape[0] // CHUNK
    for j in range(N_LANE_BLOCKS):
        for s in range(CHUNK):
            u4_ref[j, :, s * LANES:(s + 1) * LANES] = (
                u_perm[s * rows:(s + 1) * rows, j * LANES:(j + 1) * LANES])


def _inproj_kernel(x_ref, sh_ref, sc_ref, n1_ref, w_ref, sel_ref, qn_ref, kn_ref,
                   cos_ref, sinp_ref, sinn_ref, *refs, latent):
    nb, tokens, _ = x_ref.shape
    h = _rms(x_ref[...]) * n1_ref[...]
    h = h * (1.0 + sc_ref[...]) + sh_ref[...]
    hb = h.reshape(nb * tokens, D_MODEL).astype(BF16)
    if latent:
        q_ref, k_ref, v_ref, u4_ref, g_ref = refs
        rope = (cos_ref[...], sinp_ref[...], sinn_ref[...])
        q = _bdot(hb, w_ref[:, :Q_END])
        for hd in range(N_HEADS):
            _norm_rope_store(q[:, hd * HEAD_DIM:(hd + 1) * HEAD_DIM], qn_ref[...], rope,
                             q_ref, hd * HEAD_DIM, nb)
    else:
        k_ref, v_ref, u4_ref = refs
        rope = None
    k = _bdot(hb, w_ref[:, Q_END:K_END])
    for hd in range(N_KV_HEADS):
        _norm_rope_store(k[:, hd * HEAD_DIM:(hd + 1) * HEAD_DIM], kn_ref[...], rope,
                         k_ref, hd * HEAD_DIM, nb)
    v_ref[...] = _bdot(hb, w_ref[:, K_END:V_END]).reshape(nb, tokens, KV_DIM).astype(v_ref.dtype)
    hb_perm = _bdot(sel_ref[...], hb).astype(BF16)
    _store_chunk_rows(_bdot(hb_perm, w_ref[:, V_END:U_END]), u4_ref)
    if latent:
        g_ref[...] = _bdot(hb, w_ref[:, U_END:]).reshape(nb, tokens, 2 * D_MODEL)


def _inproj(x, sh1, sc1, n1, w_in_b, qn, kn, rope, latent):
    b, l, _ = x.shape
    tt = TOK_TILE
    tok = lambda w: pl.BlockSpec((b, tt, w), lambda i: (0, i, 0))
    mod_spec = _const_spec(sh1.shape)
    rope_spec = pl.BlockSpec((tt, HEAD_DIM), lambda i: (i, 0))
    u4_spec = pl.BlockSpec((N_LANE_BLOCKS, tt // CHUNK * b, CHUNK_W), lambda i: (0, i, 0))
    kv_u_specs = [tok(KV_DIM), tok(KV_DIM), u4_spec]
    sel = _chunk_row_select(b, tt)
    kv_u_shapes = [jax.ShapeDtypeStruct((b, l, KV_DIM), BF16),
                   jax.ShapeDtypeStruct((b, l, KV_DIM), BF16),
                   jax.ShapeDtypeStruct((N_LANE_BLOCKS, l // CHUNK * b, CHUNK_W), F32)]
    if latent:
        out_specs = [tok(ATTN_DIM)] + kv_u_specs + [tok(2 * D_MODEL)]
        out_shape = ([jax.ShapeDtypeStruct((b, l, ATTN_DIM), BF16)] + kv_u_shapes
                     + [jax.ShapeDtypeStruct((b, l, 2 * D_MODEL), F32)])
    else:
        out_specs, out_shape = kv_u_specs, kv_u_shapes
    return pl.pallas_call(
        functools.partial(_inproj_kernel, latent=latent),
        grid=(l // tt,),
        in_specs=[tok(D_MODEL), mod_spec, mod_spec, _const_spec((1, D_MODEL)),
                  _const_spec(w_in_b.shape), _const_spec(sel.shape),
                  _const_spec((1, HEAD_DIM)), _const_spec((1, HEAD_DIM)),
                  rope_spec, rope_spec, rope_spec],
        out_specs=out_specs,
        out_shape=out_shape,
        compiler_params=_params(1),
        name="inproj_latent" if latent else "inproj_ctx",
    )(x, sh1, sc1, n1, w_in_b, sel, qn, kn, *rope)


def _attn_kernel(q_ref, kc_ref, k_ref, vc_ref, v_ref, o_ref):
    keys = jnp.concatenate([kc_ref[0], k_ref[0]], axis=0)
    vals = jnp.concatenate([vc_ref[0], v_ref[0]], axis=0)
    vals = jnp.concatenate([vals, jnp.ones_like(vals)], axis=-1)
    nt = (((1,), (1,)), ((), ()))

    def scores(r):
        q = q_ref[0, :, r * HEAD_DIM:(r + 1) * HEAD_DIM]
        return lax.dot_general(q, keys, nt, preferred_element_type=F32)

    cur = scores(0)
    for r in range(HEADS_PER_KV):
        nxt = scores(r + 1) if r + 1 < HEADS_PER_KV else None
        p = jnp.exp(cur - jnp.max(cur, axis=-1, keepdims=True)).astype(BF16)
        o = _bdot(p, vals)
        o_ref[0, :, r * HEAD_DIM:(r + 1) * HEAD_DIM] = (
            o[:, :HEAD_DIM] / o[:, HEAD_DIM:]).astype(o_ref.dtype)
        cur = nxt


def _attention(q, kc, k, vc, v):
    b, l, _ = q.shape
    lc = kc.shape[1]
    tq = TQ_ATTN
    qspec = pl.BlockSpec((1, tq, HEADS_PER_KV * HEAD_DIM), lambda i, h, j: (i, j, h))
    kv = lambda n: pl.BlockSpec((1, n, HEAD_DIM), lambda i, h, j: (i, 0, h))
    return pl.pallas_call(
        _attn_kernel,
        grid=(b, N_KV_HEADS, l // tq),
        in_specs=[qspec, kv(lc), kv(l), kv(lc), kv(l)],
        out_specs=qspec,
        out_shape=jax.ShapeDtypeStruct((b, l, ATTN_DIM), BF16),
        compiler_params=_params(3),
        name="attention",
    )(q, kc, k, vc, v)


def _s5_discretise(lam_re, lam_im, log_dt, b_re, b_im):
    dt = jnp.exp(log_dt)[..., None]
    mag = jnp.exp(lam_re * dt)
    ang = lam_im * dt
    a_re = mag * jnp.cos(ang)
    a_im = mag * jnp.sin(ang)
    den = lam_re * lam_re + lam_im * lam_im
    nr = a_re - 1.0
    ni = a_im
    f_re = ((nr * lam_re + ni * lam_im) / den)[:, :, None, :]
    f_im = ((ni * lam_re - nr * lam_im) / den)[:, :, None, :]
    bt_re = b_re.transpose(0, 1, 3, 2)
    bt_im = b_im.transpose(0, 1, 3, 2)
    return (a_re[:, :, None, :], a_im[:, :, None, :],
            f_re * bt_re - f_im * bt_im, f_re * bt_im + f_im * bt_re)


def _s5_table_kernel(a_re_ref, a_im_ref, bb_re_ref, bb_im_ref, c_re_ref, c_im_ref,
                     m_ref, e_ref, f_ref, ap_ref, x_scr, e_scr, f_scr, c_scr):
    hp = lax.Precision.HIGHEST
    i32 = jnp.int32
    na, p_, t_ = GROUPS_PER_BLOCK, SSM_GROUP, CHUNK
    nt = (((1,), (1,)), ((), ()))
    for d in range(2):
        for a in range(na):
            a_re, a_im = a_re_ref[d, a], a_im_ref[d, a]
            b_re, b_im = bb_re_ref[d, a], bb_im_ref[d, a]
            c_re, c_im = c_re_ref[d, a], c_im_ref[d, a]
            c_scr[d, 0, a * p_:(a + 1) * p_, :] = c_re
            c_scr[d, 1, a * p_:(a + 1) * p_, :] = c_im
            pw_re, pw_im = jnp.ones_like(a_re), jnp.zeros_like(a_im)
            for tau in range(t_ + 1):
                if tau < t_:
                    pb_re = b_re * pw_re - b_im * pw_im
                    pb_im = b_re * pw_im + b_im * pw_re
                    r0 = (tau * na + a) * p_
                    x_scr[d, 0, r0:r0 + p_, :] = pb_re
                    x_scr[d, 1, r0:r0 + p_, :] = pb_im
                    s_in = t_ - 1 - tau if d == 0 else tau
                    r0 = (s_in * na + a) * p_
                    e_scr[d, 0, r0:r0 + p_, :] = pb_re
                    e_scr[d, 1, r0:r0 + p_, :] = pb_im
                if tau > 0:
                    t_out = tau - 1 if d == 0 else t_ - tau
                    r0 = (t_out * na + a) * p_
                    f_scr[d, 0, r0:r0 + p_, :] = c_re * pw_re - c_im * pw_im
                    f_scr[d, 1, r0:r0 + p_, :] = -(c_re * pw_im + c_im * pw_re)
                if tau == t_:
                    ap_ref[d, 0, 0, a] = pw_re
                    ap_ref[d, 0, 1, a] = pw_im
                else:
                    pw_re, pw_im = pw_re * a_re - pw_im * a_im, pw_re * a_im + pw_im * a_re

    grp_bits = p_.bit_length() - 1
    st_bits = SSM_STATE.bit_length() - 1
    blk_bits = na.bit_length() - 1
    gmask = na - 1
    r_w = lax.broadcasted_iota(i32, (CHUNK_W, 1), 0)
    c_w = lax.broadcasted_iota(i32, (1, CHUNK_W), 1)
    c_l = lax.broadcasted_iota(i32, (1, LANES), 1)
    r_n = lax.broadcasted_iota(i32, (SSM_STATE, 1), 0)
    c_n = lax.broadcasted_iota(i32, (1, SSM_STATE), 1)
    grp_tok_r = (r_w >> grp_bits) & gmask
    grp_tok_c = (c_w >> grp_bits) & gmask
    grp_st_r = (r_w >> st_bits) & gmask
    grp_st_c = (c_w >> st_bits) & gmask

    lag = []
    for d in range(2):
        full = (lax.dot_general(x_scr[d, 0], c_scr[d, 0], nt, precision=hp, preferred_element_type=F32)
                - lax.dot_general(x_scr[d, 1], c_scr[d, 1], nt, precision=hp, preferred_element_type=F32))
        lag.append(jnp.where(grp_tok_r == (c_l >> grp_bits), full, 0.0))
    for s in range(t_):
        for t in range(t_):
            if t > s:
                blk = lag[0][(t - s) * LANES:(t - s + 1) * LANES]
            elif s > t:
                blk = lag[1][(s - t) * LANES:(s - t + 1) * LANES]
            else:
                blk = lag[0][:LANES] + lag[1][:LANES]
            m_ref[0, s * LANES:(s + 1) * LANES, t * LANES:(t + 1) * LANES] = blk.astype(m_ref.dtype)

    for d in range(2):
        e_full = jnp.zeros((CHUNK_W, 2 * STATE_W), F32)
        f_full = jnp.zeros((2 * STATE_W, CHUNK_W), F32)
        for c in range(2):
            rep = (((c_w >> (st_bits + blk_bits)) == c) & ((c_w & (SSM_STATE - 1)) == r_n))
            rep = rep.astype(F32).astype(BF16)
            rep_t = (((r_w >> (st_bits + blk_bits)) == c) & ((r_w & (SSM_STATE - 1)) == c_n))
            rep_t = rep_t.astype(F32).astype(BF16)
            e_full = e_full + _bdot(e_scr[d, c].astype(BF16), rep)
            f_full = f_full + lax.dot_general(rep_t, f_scr[d, c].astype(BF16), nt,
                                              preferred_element_type=F32)
        e_ref[d, 0] = jnp.where(grp_tok_r == grp_st_c, e_full, 0.0).astype(e_ref.dtype)
        f_ref[d, 0] = jnp.where(grp_st_r == grp_tok_c, f_full, 0.0).astype(f_ref.dtype)


def _s5_tables(a_re, a_im, bb_re, bb_im, c_re, c_im):
    nj, na = N_LANE_BLOCKS, GROUPS_PER_BLOCK
    grp = lambda r: pl.BlockSpec((2, na, r, SSM_STATE), lambda j: (0, j, 0, 0))
    dj = lambda r, c: pl.BlockSpec((2, 1, r, c), lambda j: (0, j, 0, 0))
    src = lambda: pltpu.VMEM((2, 2, CHUNK_W, SSM_STATE), F32)
    return pl.pallas_call(
        _s5_table_kernel,
        grid=(nj,),
        in_specs=[grp(1), grp(1), grp(SSM_GROUP), grp(SSM_GROUP), grp(SSM_GROUP), grp(SSM_GROUP)],
        out_specs=[pl.BlockSpec((1, CHUNK_W, CHUNK_W), lambda j: (j, 0, 0)),
                   dj(CHUNK_W, 2 * STATE_W), dj(2 * STATE_W, CHUNK_W),
                   pl.BlockSpec((2, 1, 2, na, 1, SSM_STATE), lambda j: (0, j, 0, 0, 0, 0))],
        out_shape=[jax.ShapeDtypeStruct((nj, CHUNK_W, CHUNK_W), BF16),
                   jax.ShapeDtypeStruct((2, nj, CHUNK_W, 2 * STATE_W), BF16),
                   jax.ShapeDtypeStruct((2, nj, 2 * STATE_W, CHUNK_W), BF16),
                   jax.ShapeDtypeStruct((2, nj, 2, na, 1, SSM_STATE), F32)],
        scratch_shapes=[src(), src(), src(), pltpu.VMEM((2, 2, LANES, SSM_STATE), F32)],
        compiler_params=_params(1),
        name="s5_tables",
    )(a_re, a_im, bb_re, bb_im, c_re, c_im)


def _s5_state_kernel(uc_ref, ul_ref, e_ref, a_ref, sin_ref, c_scr, sre_scr, sim_scr, *, nb, n_chunks):
    d = pl.program_id(1)
    r = pl.program_id(2)
    a_re = jnp.broadcast_to(a_ref[0, 0, 0], (nb, STATE_W))
    a_im = jnp.broadcast_to(a_ref[0, 0, 1], (nb, STATE_W))

    def scan(store):
        def body(i, s):
            kk = jnp.where(d == 0, i, n_chunks - 1 - i)
            row = pl.multiple_of(kk * nb, nb)
            s_re, s_im = s
            if store:
                sin_ref[0, 0, pl.ds(row, nb), 0:STATE_W] = s_re.astype(sin_ref.dtype)
                sin_ref[0, 0, pl.ds(row, nb), STATE_W:] = s_im.astype(sin_ref.dtype)
            c_re = c_scr[pl.ds(row, nb), 0:STATE_W]
            c_im = c_scr[pl.ds(row, nb), STATE_W:]
            return (s_re * a_re - s_im * a_im + c_re, s_re * a_im + s_im * a_re + c_im)

        s = lax.fori_loop(0, n_chunks, body, (sre_scr[...], sim_scr[...]))
        sre_scr[...] = s[0]
        sim_scr[...] = s[1]

    @pl.when(r == 0)
    def _():
        sre_scr[...] = jnp.zeros_like(sre_scr)
        sim_scr[...] = jnp.zeros_like(sim_scr)
        c_scr[...] = _bdot(uc_ref[0].astype(BF16), e_ref[0, 0])
        scan(False)

    @pl.when(r > 0)
    def _():
        c_scr[...] = _bdot(ul_ref[0].astype(BF16), e_ref[0, 0])
        scan(True)


def _s5_states(u4c, u4l, e_tab, a_tab, nb):
    rows = S5_STATE_ROWS
    assert u4c.shape[1] == rows
    n_tiles = u4l.shape[1] // rows

    def lat_tile(d, r):
        t = jnp.maximum(r - 1, 0)
        return jnp.where(d == 0, t, n_tiles - 1 - t)

    kern = functools.partial(_s5_state_kernel, nb=nb, n_chunks=rows // nb)
    return pl.pallas_call(
        kern,
        grid=(N_LANE_BLOCKS, 2, n_tiles + 1),
        in_specs=[pl.BlockSpec((1, rows, CHUNK_W), lambda j, d, r: (j, 0, 0)),
                  pl.BlockSpec((1, rows, CHUNK_W), lambda j, d, r: (j, lat_tile(d, r), 0)),
                  pl.BlockSpec((1, 1, CHUNK_W, 2 * STATE_W), lambda j, d, r: (d, j, 0, 0)),
                  pl.BlockSpec((1, 1, 2, 1, STATE_W), lambda j, d, r: (d, j, 0, 0, 0))],
        out_specs=pl.BlockSpec((1, 1, rows, 2 * STATE_W), lambda j, d, r: (d, j, lat_tile(d, r), 0)),
        out_shape=jax.ShapeDtypeStruct((2, N_LANE_BLOCKS, u4l.shape[1], 2 * STATE_W), BF16),
        scratch_shapes=[pltpu.VMEM((rows, 2 * STATE_W), F32),
                        pltpu.VMEM((nb, STATE_W), F32), pltpu.VMEM((nb, STATE_W), F32)],
        compiler_params=_params(3),
        name="s5_states",
    )(u4c, u4l, e_tab, a_tab)


def _s5_out_kernel(ul_ref, sin_ref, m_ref, f_ref, d_ref, y_ref):
    u = ul_ref[0]
    y_ref[0] = (u * d_ref[0] + _bdot(u.astype(BF16), m_ref[0])
                + _bdot(sin_ref[0, 0], f_ref[0, 0]) + _bdot(sin_ref[1, 0], f_ref[1, 0]))


def _s5_outputs(u4l, sin, m_tab, f_tab, d_tab):
    rows = S5_OUT_ROWS
    return pl.pallas_call(
        _s5_out_kernel,
        grid=(N_LANE_BLOCKS, u4l.shape[1] // rows),
        in_specs=[pl.BlockSpec((1, rows, CHUNK_W), lambda j, r: (j, r, 0)),
                  pl.BlockSpec((2, 1, rows, 2 * STATE_W), lambda j, r: (0, j, r, 0)),
                  pl.BlockSpec((1, CHUNK_W, CHUNK_W), lambda j, r: (j, 0, 0)),
                  pl.BlockSpec((2, 1, 2 * STATE_W, CHUNK_W), lambda j, r: (0, j, 0, 0)),
                  pl.BlockSpec((1, 1, CHUNK_W), lambda j, r: (j, 0, 0))],
        out_specs=pl.BlockSpec((1, rows, CHUNK_W), lambda j, r: (j, r, 0)),
        out_shape=jax.ShapeDtypeStruct(u4l.shape, F32),
        compiler_params=_params(2),
        name="s5_outputs",
    )(u4l, sin, m_tab, f_tab, d_tab)


def _merge_kernel(attn_ref, y4_ref, g_ref, x_ref, g1_ref, sh2_ref, sc2_ref, n2_ref,
                  sel_ref, wab_ref, wglu_ref, wout_ref, x1_ref, h2_ref):
    nb, tokens_all, _ = x_ref.shape
    tokens = tokens_all // MERGE_SPLIT
    rows = nb * tokens
    crow = tokens // CHUNK * nb

    def branch_dots(i):
        ts = slice(i * tokens, (i + 1) * tokens)
        act = [_gelu_tanh(y4_ref[j, i * crow:(i + 1) * crow]).astype(BF16)
               for j in range(N_LANE_BLOCKS)]
        act = jnp.concatenate(
            [jnp.concatenate([act[j][:, s * LANES:(s + 1) * LANES] for s in range(CHUNK)], axis=0)
             for j in range(N_LANE_BLOCKS)], axis=-1)
        act = _bdot(sel_ref[...], act).astype(BF16)
        p_attn = _bdot(attn_ref[:, ts].reshape(rows, ATTN_DIM), wab_ref[...])
        return p_attn, _bdot(act, wglu_ref[...])

    def mix_dot(i, p_attn, glu):
        ts = slice(i * tokens, (i + 1) * tokens)
        p_ssm = glu[:, :D_MODEL] * _sigmoid(glu[:, D_MODEL:])
        g = g_ref[:, ts].reshape(rows, 2 * D_MODEL)
        mix = _sigmoid(g[:, :D_MODEL]) * p_attn + _sigmoid(g[:, D_MODEL:]) * p_ssm
        return _bdot(mix.astype(BF16), wout_ref[...]).reshape(nb, tokens, D_MODEL)

    def finish(i, x_mix):
        ts = slice(i * tokens, (i + 1) * tokens)
        x1 = x_ref[:, ts] + g1_ref[...] * x_mix
        x1_ref[:, ts] = x1
        h2 = _rms(x1) * n2_ref[...]
        h2_ref[:, ts] = (h2 * (1.0 + sc2_ref[...]) + sh2_ref[...]).astype(h2_ref.dtype)

    dots = [None] * MERGE_SPLIT
    mixed = [None] * MERGE_SPLIT
    dots[0] = branch_dots(0)
    for i in range(MERGE_SPLIT):
        if i + 1 < MERGE_SPLIT:
            dots[i + 1] = branch_dots(i + 1)
        mixed[i] = mix_dot(i, *dots[i])
        if i > 0:
            finish(i - 1, mixed[i - 1])
    finish(MERGE_SPLIT - 1, mixed[MERGE_SPLIT - 1])


def _merge(attn, y4, g, x, g1, sh2, sc2, n2, wab, wglu, wout):
    b, l, _ = x.shape
    tt = TOK_TILE
    tok = lambda w: pl.BlockSpec((b, tt, w), lambda i: (0, i, 0))
    per_b = _const_spec((b, 1, D_MODEL))
    y4_spec = pl.BlockSpec((N_LANE_BLOCKS, tt // CHUNK * b, CHUNK_W), lambda i: (0, i, 0))
    sel = _chunk_row_select(b, tt // MERGE_SPLIT, inverse=True)
    return pl.pallas_call(
        _merge_kernel,
        grid=(l // tt,),
        in_specs=[tok(ATTN_DIM), y4_spec, tok(2 * D_MODEL), tok(D_MODEL),
                  per_b, per_b, per_b, _const_spec((1, D_MODEL)), _const_spec(sel.shape),
                  _const_spec(wab.shape), _const_spec(wglu.shape), _const_spec(wout.shape)],
        out_specs=[tok(D_MODEL), tok(D_MODEL)],
        out_shape=[jax.ShapeDtypeStruct((b, l, D_MODEL), F32),
                   jax.ShapeDtypeStruct((b, l, D_MODEL), BF16)],
        compiler_params=_params(1),
        name="merge",
    )(attn, y4, g, x, g1, sh2, sc2, n2, sel, wab, wglu, wout)


def _ffn_row_select(tm):
    seg_len = tm // SUB
    tile = np.zeros((tm, tm), np.float32)
    for seg in range(SUB):
        for i in range(seg_len):
            tile[SUB * i + seg, seg * seg_len + i] = 1.0
    halo = np.zeros((SUB, 2 * HALO), np.float32)
    halo[0, HALO - 1] = 1.0
    halo[1, HALO] = 1.0
    return jnp.asarray(tile, dtype=BF16), jnp.asarray(halo, dtype=BF16)


def _ffn_kernel(sel_ref, selh_ref, hp_ref, h_ref, hn_ref, x1_ref, g2_ref, wup_ref, cw_ref, cb_ref,
                wd_ref, fw_ref, o_ref, acc_scr, *, tm):
    j = pl.program_id(1)
    keep_prev = (j > 0).astype(F32)
    keep_next = (j < pl.num_programs(1) - 1).astype(F32)
    halo = jnp.concatenate([hp_ref[0], hn_ref[0]], axis=0)
    lhs = jnp.concatenate([_bdot(selh_ref[...], halo), _bdot(sel_ref[...], h_ref[0])],
                          axis=0).astype(BF16)
    sub = lax.broadcasted_iota(jnp.int32, (SUB, 1), 0)

    def up(f):
        return (_bdot(lhs, wup_ref[:, f * FF_CHUNK:(f + 1) * FF_CHUNK]),
                _bdot(lhs, wup_ref[:, D_FF + f * FF_CHUNK:D_FF + (f + 1) * FF_CHUNK]))

    def conv(zall, col):
        z = zall[SUB:]
        before = jnp.where(sub == 0, zall[0:1] * keep_prev, pltpu.roll(z[tm - SUB:], 1, 0))
        after = jnp.where(sub == SUB - 1, zall[1:2] * keep_next, pltpu.roll(z[:SUB], SUB - 1, 0))
        z_prev = jnp.concatenate([before, z[:tm - SUB]], axis=0)
        z_next = jnp.concatenate([z[SUB:], after], axis=0)
        cw = cw_ref[:, col:col + FF_CHUNK]
        return z_prev * cw[0:1] + z * cw[1:2] + z_next * cw[2:3] + cb_ref[:, col:col + FF_CHUNK]

    acts = []
    cur = up(0)
    for f in range(N_FF_CHUNKS):
        nxt = up(f + 1) if f + 1 < N_FF_CHUNKS else None
        val = conv(cur[0], f * FF_CHUNK)
        gate = conv(cur[1], D_FF + f * FF_CHUNK)
        acts.append((gate * _sigmoid(gate) * val).astype(BF16))
        cur = nxt
    acc = _bdot(jnp.concatenate(acts, axis=-1), wd_ref[...])
    seg_len = tm // SUB
    for k in range(D_MODEL // LANES):
        acc_scr[k] = acc[:, k * LANES:(k + 1) * LANES]
        for seg in range(SUB):
            o_ref[0, seg * seg_len:(seg + 1) * seg_len, k * LANES:(k + 1) * LANES] = (
                acc_scr[k, pl.ds(seg, seg_len, stride=SUB), :])
    x2 = x1_ref[0] + g2_ref[0] * o_ref[0]
    o_ref[0] = _rms(x2) * fw_ref[...]


def _conv_ffn(h2, x1, g2, wup, cw, cb, wd, fw):
    b, l, _ = x1.shape
    tm = TM_FFN
    nh = tm // HALO
    last = l // HALO - 1
    tok = lambda: pl.BlockSpec((1, tm, D_MODEL), lambda i, j: (i, j, 0))
    prev = pl.BlockSpec((1, HALO, D_MODEL), lambda i, j: (i, jnp.maximum(j * nh - 1, 0), 0))
    nxt = pl.BlockSpec((1, HALO, D_MODEL), lambda i, j: (i, jnp.minimum((j + 1) * nh, last), 0))
    per_b = pl.BlockSpec((1, 1, D_MODEL), lambda i, j: (i, 0, 0))
    once = lambda shape: pl.BlockSpec(shape, lambda *_: (0,) * len(shape),
                                      pipeline_mode=pl.Buffered(1))
    sel, selh = _ffn_row_select(tm)
    return pl.pallas_call(
        functools.partial(_ffn_kernel, tm=tm),
        grid=(b, l // tm),
        in_specs=[once(sel.shape), once(selh.shape), prev, tok(), nxt, tok(), per_b,
                  once(wup.shape), once(cw.shape), once(cb.shape), once(wd.shape),
                  _const_spec((1, D_MODEL))],
        out_specs=tok(),
        out_shape=jax.ShapeDtypeStruct((b, l, D_MODEL), F32),
        scratch_shapes=[pltpu.VMEM((D_MODEL // LANES, tm, LANES), F32)],
        compiler_params=_params(2),
        name="conv_ffn",
    )(sel, selh, h2, h2, h2, x1, g2, wup, cw, cb, wd, fw)


def _rope_tables(l):
    rows = jnp.repeat(jnp.arange(l // GRID_W, dtype=F32), GRID_W)
    cols = jnp.tile(jnp.arange(GRID_W, dtype=F32), l // GRID_W)
    inv_freq = ROPE_THETA ** (-jnp.arange(0, ROPE_AXIS_DIM, 2, dtype=F32) / ROPE_AXIS_DIM)
    ang = jnp.concatenate([rows[:, None] * inv_freq, cols[:, None] * inv_freq], axis=-1)
    cos = jnp.repeat(jnp.cos(ang), 2, axis=-1)
    sin = jnp.repeat(jnp.sin(ang), 2, axis=-1)
    odd = (jnp.arange(HEAD_DIM) % 2 == 1)[None, :]
    return cos, jnp.where(odd, sin, 0.0), jnp.where(odd, 0.0, -sin)


def kernel(x, c, ctx, c_ctx, w_mod, b_mod, norm1_w, norm2_w, w_in, q_norm_w, k_norm_w, w_attn_br,
           ssm_lambda_re, ssm_lambda_im, ssm_log_dt, ssm_b_re, ssm_b_im, ssm_c_re, ssm_c_im, ssm_d,
           w_glu, w_out, w_up, conv_w, conv_b, w_down, final_norm_w):
    b, l, d = x.shape
    lc = ctx.shape[1]
    assert w_mod.shape[0] == 1 and d == D_MODEL and l % TOK_TILE == 0 and lc % TOK_TILE == 0
    assert lc // CHUNK * b == S5_STATE_ROWS and (l // CHUNK * b) % S5_OUT_ROWS == 0
    layer = 0

    pad = (-(b + 1)) % 8
    c_rows = jnp.concatenate([c, c_ctx[None, :], jnp.zeros((pad, d), F32)], axis=0)
    mod = _modulation(c_rows, w_mod[layer], b_mod[layer])
    sh1, sc1, g1, sh2, sc2, g2 = [m[:b, None, :] for m in jnp.split(mod, N_MOD, axis=-1)]
    csh1, csc1 = mod[b:b + 1, None, :d], mod[b:b + 1, None, d:2 * d]

    w_in_b = w_in[layer].astype(BF16)
    qn = (q_norm_w[layer] * (1.0 / math.sqrt(HEAD_DIM))).reshape(1, HEAD_DIM)
    kn = k_norm_w[layer].reshape(1, HEAD_DIM)
    n1 = norm1_w[layer].reshape(1, d)

    no_rope = (jnp.ones((lc, HEAD_DIM), F32),) + (jnp.zeros((lc, HEAD_DIM), F32),) * 2
    q, k, v, u4l, g = _inproj(x, sh1, sc1, n1, w_in_b, qn, kn, _rope_tables(l), latent=True)
    kc, vc, u4c = _inproj(ctx, csh1, csc1, n1, w_in_b, qn, kn, no_rope, latent=False)

    attn = _attention(q, kc, k, vc, v)

    disc = _s5_discretise(ssm_lambda_re[layer], ssm_lambda_im[layer], ssm_log_dt[layer],
                          ssm_b_re[layer], ssm_b_im[layer])
    m_tab, e_tab, f_tab, a_pow = _s5_tables(*disc, ssm_c_re[layer], ssm_c_im[layer])
    a_tab = a_pow.reshape(2, N_LANE_BLOCKS, 2, 1, STATE_W)
    d_tab = jnp.tile(ssm_d[layer].reshape(N_LANE_BLOCKS, 1, LANES), (1, CHUNK, 1))
    d_tab = d_tab.reshape(N_LANE_BLOCKS, 1, CHUNK_W)
    sin = _s5_states(u4c, u4l, e_tab, a_tab, b)
    y4 = _s5_outputs(u4l, sin, m_tab, f_tab, d_tab)

    x1, h2 = _merge(attn, y4, g, x, g1, sh2, sc2, norm2_w[layer].reshape(1, d),
                    w_attn_br[layer].astype(BF16), w_glu[layer].astype(BF16),
                    w_out[layer].astype(BF16))

    return _conv_ffn(h2, x1, g2, w_up[layer].astype(BF16), conv_w[layer],
                     conv_b[layer].reshape(1, 2 * D_FF), w_down[layer].astype(BF16),
                     final_norm_w.reshape(1, d))
```

```python
import functools
import math

import jax
import jax.numpy as jnp
import numpy as np
from jax import lax
from jax.experimental import pallas as pl
from jax.experimental.pallas import tpu as pltpu

F32 = jnp.float32
BF16 = jnp.bfloat16

D_MODEL = 1024
GRID_W = 64
N_HEADS = 8
N_KV_HEADS = 2
HEAD_DIM = 128
HEADS_PER_KV = N_HEADS // N_KV_HEADS
ATTN_DIM = N_HEADS * HEAD_DIM
KV_DIM = N_KV_HEADS * HEAD_DIM
ROPE_THETA = 10000.0
ROPE_AXIS_DIM = HEAD_DIM // 2
SSM_DIM = 512
SSM_GROUP = 16
N_SSM_GROUPS = SSM_DIM // SSM_GROUP
SSM_STATE = 64
Q_END = ATTN_DIM
K_END = Q_END + KV_DIM
V_END = K_END + KV_DIM
U_END = V_END + SSM_DIM
D_IN = U_END + 2 * D_MODEL
D_FF = 2816
N_MOD = 6
EPS = 1e-6

LANES = 128
SUB = 8
CHUNK = 8
N_LANE_BLOCKS = SSM_DIM // LANES
GROUPS_PER_BLOCK = LANES // SSM_GROUP
CHUNK_W = CHUNK * LANES
STATE_W = GROUPS_PER_BLOCK * SSM_STATE

FF_CHUNK = 256
N_FF_CHUNKS = D_FF // FF_CHUNK
HALO = 16

VMEM_LIMIT = 56 * 1024 * 1024

TOK_TILE = 32
TQ_ATTN = 1024
PROJ_SPLIT = 2
MERGE_SPLIT = 2
TM_FFN = 512
S5_STATE_ROWS = 512
S5_OUT_ROWS = 1024


def _sigmoid(x):
    return 1.0 / (1.0 + jnp.exp(-x))


def _gelu_tanh(x):
    return 0.5 * x * (1.0 + jnp.tanh(math.sqrt(2.0 / math.pi) * (x + 0.044715 * (x * x * x))))


def _rms(x):
    return x * lax.rsqrt(jnp.mean(x * x, axis=-1, keepdims=True) + EPS)


def _bdot(a, b):
    return jnp.dot(a, b, preferred_element_type=F32)


def _const_spec(shape):
    nd = len(shape)
    return pl.BlockSpec(shape, lambda *_: (0,) * nd)


def _params(n_axes):
    return pltpu.CompilerParams(dimension_semantics=("arbitrary",) * n_axes,
                                vmem_limit_bytes=VMEM_LIMIT)


def _mod_kernel(c_ref, w_ref, b_ref, o_ref):
    c = c_ref[...]
    a = c * _sigmoid(c)
    o_ref[...] = jnp.dot(a, w_ref[...], precision=lax.Precision.HIGHEST,
                         preferred_element_type=F32) + b_ref[...]


def _modulation(c_rows, w_mod, b_mod):
    rows = c_rows.shape[0]
    n = w_mod.shape[1]
    tn = 1536
    return pl.pallas_call(
        _mod_kernel,
        grid=(n // tn,),
        in_specs=[pl.BlockSpec((rows, D_MODEL), lambda j: (0, 0)),
                  pl.BlockSpec((D_MODEL, tn), lambda j: (0, j)),
                  pl.BlockSpec((1, tn), lambda j: (0, j))],
        out_specs=pl.BlockSpec((rows, tn), lambda j: (0, j)),
        out_shape=jax.ShapeDtypeStruct((rows, n), F32),
        compiler_params=_params(1),
        name="modulation",
    )(c_rows, w_mod, b_mod.reshape(1, n))


def _norm_rope_store(t, nw, rope, out_ref, ts, col, nb):
    t = _rms(t) * nw
    if rope is not None:
        cos, sin_prev, sin_next = rope
        prev = pltpu.roll(t, 1, 1).reshape(nb, -1, HEAD_DIM)
        nxt = pltpu.roll(t, HEAD_DIM - 1, 1).reshape(nb, -1, HEAD_DIM)
        t3 = t.reshape(nb, -1, HEAD_DIM) * cos + prev * sin_prev + nxt * sin_next
    else:
        t3 = t.reshape(nb, -1, HEAD_DIM)
    out_ref[:, ts, col:col + HEAD_DIM] = t3.astype(out_ref.dtype)


def _chunk_row_select(nb, tokens, inverse=False):
    sel = np.zeros((nb * tokens, nb * tokens), np.float32)
    for b in range(nb):
        for t in range(tokens):
            k, s = divmod(t, CHUNK)
            sel[(s * (tokens // CHUNK) + k) * nb + b, b * tokens + t] = 1.0
    return jnp.asarray(sel.T if inverse else sel, dtype=BF16)


def _store_chunk_rows(u_perm, u4_ref, row0):
    rows = u_perm.shape[0] // CHUNK
    for j in range(N_LANE_BLOCKS):
        for s in range(CHUNK):
            u4_ref[j, row0:row0 + rows, s * LANES:(s + 1) * LANES] = (
                u_perm[s * rows:(s + 1) * rows, j * LANES:(j + 1) * LANES])


def _inproj_kernel(x_ref, sh_ref, sc_ref, n1_ref, w_ref, sel_ref, qn_ref, kn_ref,
                   cos_ref, sinp_ref, sinn_ref, *refs, latent):
    nb, tokens_all, _ = x_ref.shape
    tokens = tokens_all // PROJ_SPLIT
    rows = nb * tokens
    if latent:
        q_ref, k_ref, v_ref, u4_ref, g_ref = refs
    else:
        k_ref, v_ref, u4_ref = refs

    def lhs(i):
        ts = slice(i * tokens, (i + 1) * tokens)
        h = _rms(x_ref[:, ts]) * n1_ref[...]
        h = h * (1.0 + sc_ref[...]) + sh_ref[...]
        return h.reshape(rows, D_MODEL).astype(BF16)

    def project(i, hb):
        ts = slice(i * tokens, (i + 1) * tokens)
        rope = (cos_ref[ts], sinp_ref[ts], sinn_ref[ts]) if latent else None
        if latent:
            q = _bdot(hb, w_ref[:, :Q_END])
            for hd in range(N_HEADS):
                _norm_rope_store(q[:, hd * HEAD_DIM:(hd + 1) * HEAD_DIM], qn_ref[...], rope,
                                 q_ref, ts, hd * HEAD_DIM, nb)
        k = _bdot(hb, w_ref[:, Q_END:K_END])
        for hd in range(N_KV_HEADS):
            _norm_rope_store(k[:, hd * HEAD_DIM:(hd + 1) * HEAD_DIM], kn_ref[...], rope,
                             k_ref, ts, hd * HEAD_DIM, nb)
        v_ref[:, ts] = _bdot(hb, w_ref[:, K_END:V_END]).reshape(nb, tokens, KV_DIM).astype(v_ref.dtype)
        hb_perm = _bdot(sel_ref[...], hb).astype(BF16)
        _store_chunk_rows(_bdot(hb_perm, w_ref[:, V_END:U_END]), u4_ref, i * (rows // CHUNK))
        if latent:
            g_ref[:, ts] = _bdot(hb, w_ref[:, U_END:]).reshape(nb, tokens, 2 * D_MODEL)

    cur = lhs(0)
    for i in range(PROJ_SPLIT):
        nxt = lhs(i + 1) if i + 1 < PROJ_SPLIT else None
        project(i, cur)
        cur = nxt


def _inproj(x, sh1, sc1, n1, w_in_b, qn, kn, rope, latent):
    b, l, _ = x.shape
    tt = TOK_TILE
    tok = lambda w: pl.BlockSpec((b, tt, w), lambda i: (0, i, 0))
    mod_spec = _const_spec(sh1.shape)
    rope_spec = pl.BlockSpec((tt, HEAD_DIM), lambda i: (i, 0))
    u4_spec = pl.BlockSpec((N_LANE_BLOCKS, tt // CHUNK * b, CHUNK_W), lambda i: (0, i, 0))
    kv_u_specs = [tok(KV_DIM), tok(KV_DIM), u4_spec]
    sel = _chunk_row_select(b, tt // PROJ_SPLIT)
    kv_u_shapes = [jax.ShapeDtypeStruct((b, l, KV_DIM), BF16),
                   jax.ShapeDtypeStruct((b, l, KV_DIM), BF16),
                   jax.ShapeDtypeStruct((N_LANE_BLOCKS, l // CHUNK * b, CHUNK_W), F32)]
    if latent:
        out_specs = [tok(ATTN_DIM)] + kv_u_specs + [tok(2 * D_MODEL)]
        out_shape = ([jax.ShapeDtypeStruct((b, l, ATTN_DIM), BF16)] + kv_u_shapes
                     + [jax.ShapeDtypeStruct((b, l, 2 * D_MODEL), F32)])
    else:
        out_specs, out_shape = kv_u_specs, kv_u_shapes
    return pl.pallas_call(
        functools.partial(_inproj_kernel, latent=latent),
        grid=(l // tt,),
        in_specs=[tok(D_MODEL), mod_spec, mod_spec, _const_spec((1, D_MODEL)),
                  _const_spec(w_in_b.shape), _const_spec(sel.shape),
                  _const_spec((1, HEAD_DIM)), _const_spec((1, HEAD_DIM)),
                  rope_spec, rope_spec, rope_spec],
        out_specs=out_specs,
        out_shape=out_shape,
        compiler_params=_params(1),
        name="inproj_latent" if latent else "inproj_ctx",
    )(x, sh1, sc1, n1, w_in_b, sel, qn, kn, *rope)


def _attn_kernel(q_ref, kc_ref, k_ref, vc_ref, v_ref, o_ref):
    keys = jnp.concatenate([kc_ref[0], k_ref[0]], axis=0)
    vals = jnp.concatenate([vc_ref[0], v_ref[0]], axis=0)
    vals = jnp.concatenate([vals, jnp.ones_like(vals)], axis=-1)
    nt = (((1,), (1,)), ((), ()))

    def scores(r):
        q = q_ref[0, :, r * HEAD_DIM:(r + 1) * HEAD_DIM]
        return lax.dot_general(q, keys, nt, preferred_element_type=F32)

    cur = scores(0)
    for r in range(HEADS_PER_KV):
        nxt = scores(r + 1) if r + 1 < HEADS_PER_KV else None
        p = jnp.exp(cur - jnp.max(cur, axis=-1, keepdims=True)).astype(BF16)
        o = _bdot(p, vals)
        o_ref[0, :, r * HEAD_DIM:(r + 1) * HEAD_DIM] = (
            o[:, :HEAD_DIM] / o[:, HEAD_DIM:]).astype(o_ref.dtype)
        cur = nxt


def _attention(q, kc, k, vc, v):
    b, l, _ = q.shape
    lc = kc.shape[1]
    tq = TQ_ATTN
    qspec = pl.BlockSpec((1, tq, HEADS_PER_KV * HEAD_DIM), lambda i, h, j: (i, j, h))
    kv = lambda n: pl.BlockSpec((1, n, HEAD_DIM), lambda i, h, j: (i, 0, h))
    return pl.pallas_call(
        _attn_kernel,
        grid=(b, N_KV_HEADS, l // tq),
        in_specs=[qspec, kv(lc), kv(l), kv(lc), kv(l)],
        out_specs=qspec,
        out_shape=jax.ShapeDtypeStruct((b, l, ATTN_DIM), BF16),
        compiler_params=_params(3),
        name="attention",
    )(q, kc, k, vc, v)


def _s5_discretise(lam_re, lam_im, log_dt, b_re, b_im):
    dt = jnp.exp(log_dt)[..., None]
    mag = jnp.exp(lam_re * dt)
    ang = lam_im * dt
    a_re = mag * jnp.cos(ang)
    a_im = mag * jnp.sin(ang)
    den = lam_re * lam_re + lam_im * lam_im
    nr = a_re - 1.0
    ni = a_im
    f_re = ((nr * lam_re + ni * lam_im) / den)[:, :, None, :]
    f_im = ((ni * lam_re - nr * lam_im) / den)[:, :, None, :]
    bt_re = b_re.transpose(0, 1, 3, 2)
    bt_im = b_im.transpose(0, 1, 3, 2)
    return (a_re[:, :, None, :], a_im[:, :, None, :],
            f_re * bt_re - f_im * bt_im, f_re * bt_im + f_im * bt_re)


def _s5_table_kernel(a_re_ref, a_im_ref, bb_re_ref, bb_im_ref, c_re_ref, c_im_ref,
                     m_ref, e_ref, f_ref, ap_ref, x_scr, e_scr, f_scr, c_scr):
    hp = lax.Precision.HIGHEST
    i32 = jnp.int32
    na, p_, t_ = GROUPS_PER_BLOCK, SSM_GROUP, CHUNK
    nt = (((1,), (1,)), ((), ()))
    for d in range(2):
        for a in range(na):
            a_re, a_im = a_re_ref[d, a], a_im_ref[d, a]
            b_re, b_im = bb_re_ref[d, a], bb_im_ref[d, a]
            c_re, c_im = c_re_ref[d, a], c_im_ref[d, a]
            c_scr[d, 0, a * p_:(a + 1) * p_, :] = c_re
            c_scr[d, 1, a * p_:(a + 1) * p_, :] = c_im
            pw_re, pw_im = jnp.ones_like(a_re), jnp.zeros_like(a_im)
            for tau in range(t_ + 1):
                if tau < t_:
                    pb_re = b_re * pw_re - b_im * pw_im
                    pb_im = b_re * pw_im + b_im * pw_re
                    r0 = (tau * na + a) * p_
                    x_scr[d, 0, r0:r0 + p_, :] = pb_re
                    x_scr[d, 1, r0:r0 + p_, :] = pb_im
                    s_in = t_ - 1 - tau if d == 0 else tau
                    r0 = (s_in * na + a) * p_
                    e_scr[d, 0, r0:r0 + p_, :] = pb_re
                    e_scr[d, 1, r0:r0 + p_, :] = pb_im
                if tau > 0:
                    t_out = tau - 1 if d == 0 else t_ - tau
                    r0 = (t_out * na + a) * p_
                    f_scr[d, 0, r0:r0 + p_, :] = c_re * pw_re - c_im * pw_im
                    f_scr[d, 1, r0:r0 + p_, :] = -(c_re * pw_im + c_im * pw_re)
                if tau == t_:
                    ap_ref[d, 0, 0, a] = pw_re
                    ap_ref[d, 0, 1, a] = pw_im
                else:
                    pw_re, pw_im = pw_re * a_re - pw_im * a_im, pw_re * a_im + pw_im * a_re

    grp_bits = p_.bit_length() - 1
    st_bits = SSM_STATE.bit_length() - 1
    blk_bits = na.bit_length() - 1
    gmask = na - 1
    r_w = lax.broadcasted_iota(i32, (CHUNK_W, 1), 0)
    c_w = lax.broadcasted_iota(i32, (1, CHUNK_W), 1)
    c_l = lax.broadcasted_iota(i32, (1, LANES), 1)
    r_n = lax.broadcasted_iota(i32, (SSM_STATE, 1), 0)
    c_n = lax.broadcasted_iota(i32, (1, SSM_STATE), 1)
    grp_tok_r = (r_w >> grp_bits) & gmask
    grp_tok_c = (c_w >> grp_bits) & gmask
    grp_st_r = (r_w >> st_bits) & gmask
    grp_st_c = (c_w >> st_bits) & gmask

    lag = []
    for d in range(2):
        full = (lax.dot_general(x_scr[d, 0], c_scr[d, 0], nt, precision=hp, preferred_element_type=F32)
                - lax.dot_general(x_scr[d, 1], c_scr[d, 1], nt, precision=hp, preferred_element_type=F32))
        lag.append(jnp.where(grp_tok_r == (c_l >> grp_bits), full, 0.0))
    for s in range(t_):
        for t in range(t_):
            if t > s:
                blk = lag[0][(t - s) * LANES:(t - s + 1) * LANES]
            elif s > t:
                blk = lag[1][(s - t) * LANES:(s - t + 1) * LANES]
            else:
                blk = lag[0][:LANES] + lag[1][:LANES]
            m_ref[0, s * LANES:(s + 1) * LANES, t * LANES:(t + 1) * LANES] = blk.astype(m_ref.dtype)

    for d in range(2):
        e_full = jnp.zeros((CHUNK_W, 2 * STATE_W), F32)
        f_full = jnp.zeros((2 * STATE_W, CHUNK_W), F32)
        for c in range(2):
            rep = (((c_w >> (st_bits + blk_bits)) == c) & ((c_w & (SSM_STATE - 1)) == r_n))
            rep = rep.astype(F32).astype(BF16)
            rep_t = (((r_w >> (st_bits + blk_bits)) == c) & ((r_w & (SSM_STATE - 1)) == c_n))
            rep_t = rep_t.astype(F32).astype(BF16)
            e_full = e_full + _bdot(e_scr[d, c].astype(BF16), rep)
            f_full = f_full + lax.dot_general(rep_t, f_scr[d, c].astype(BF16), nt,
                                              preferred_element_type=F32)
        e_ref[d, 0] = jnp.where(grp_tok_r == grp_st_c, e_full, 0.0).astype(e_ref.dtype)
        f_ref[d, 0] = jnp.where(grp_st_r == grp_tok_c, f_full, 0.0).astype(f_ref.dtype)


def _s5_tables(a_re, a_im, bb_re, bb_im, c_re, c_im):
    nj, na = N_LANE_BLOCKS, GROUPS_PER_BLOCK
    grp = lambda r: pl.BlockSpec((2, na, r, SSM_STATE), lambda j: (0, j, 0, 0))
    dj = lambda r, c: pl.BlockSpec((2, 1, r, c), lambda j: (0, j, 0, 0))
    src = lambda: pltpu.VMEM((2, 2, CHUNK_W, SSM_STATE), F32)
    return pl.pallas_call(
        _s5_table_kernel,
        grid=(nj,),
        in_specs=[grp(1), grp(1), grp(SSM_GROUP), grp(SSM_GROUP), grp(SSM_GROUP), grp(SSM_GROUP)],
        out_specs=[pl.BlockSpec((1, CHUNK_W, CHUNK_W), lambda j: (j, 0, 0)),
                   dj(CHUNK_W, 2 * STATE_W), dj(2 * STATE_W, CHUNK_W),
                   pl.BlockSpec((2, 1, 2, na, 1, SSM_STATE), lambda j: (0, j, 0, 0, 0, 0))],
        out_shape=[jax.ShapeDtypeStruct((nj, CHUNK_W, CHUNK_W), BF16),
                   jax.ShapeDtypeStruct((2, nj, CHUNK_W, 2 * STATE_W), BF16),
                   jax.ShapeDtypeStruct((2, nj, 2 * STATE_W, CHUNK_W), BF16),
                   jax.ShapeDtypeStruct((2, nj, 2, na, 1, SSM_STATE), F32)],
        scratch_shapes=[src(), src(), src(), pltpu.VMEM((2, 2, LANES, SSM_STATE), F32)],
        compiler_params=_params(1),
        name="s5_tables",
    )(a_re, a_im, bb_re, bb_im, c_re, c_im)


def _s5_state_kernel(uc_ref, ul_ref, e_ref, a_ref, sin_ref, c_scr, sre_scr, sim_scr, *, nb, n_chunks):
    d = pl.program_id(1)
    r = pl.program_id(2)
    a_re = jnp.broadcast_to(a_ref[0, 0, 0], (nb, STATE_W))
    a_im = jnp.broadcast_to(a_ref[0, 0, 1], (nb, STATE_W))

    def scan(store):
        def body(i, s):
            kk = jnp.where(d == 0, i, n_chunks - 1 - i)
            row = pl.multiple_of(kk * nb, nb)
            s_re, s_im = s
            if store:
                sin_ref[0, 0, pl.ds(row, nb), 0:STATE_W] = s_re.astype(sin_ref.dtype)
                sin_ref[0, 0, pl.ds(row, nb), STATE_W:] = s_im.astype(sin_ref.dtype)
            c_re = c_scr[pl.ds(row, nb), 0:STATE_W]
            c_im = c_scr[pl.ds(row, nb), STATE_W:]
            return (s_re * a_re - s_im * a_im + c_re, s_re * a_im + s_im * a_re + c_im)

        s = lax.fori_loop(0, n_chunks, body, (sre_scr[...], sim_scr[...]))
        sre_scr[...] = s[0]
        sim_scr[...] = s[1]

    @pl.when(r == 0)
    def _():
        sre_scr[...] = jnp.zeros_like(sre_scr)
        sim_scr[...] = jnp.zeros_like(sim_scr)
        c_scr[...] = _bdot(uc_ref[0].astype(BF16), e_ref[0, 0])
        scan(False)

    @pl.when(r > 0)
    def _():
        c_scr[...] = _bdot(ul_ref[0].astype(BF16), e_ref[0, 0])
        scan(True)


def _s5_states(u4c, u4l, e_tab, a_tab, nb):
    rows = S5_STATE_ROWS
    assert u4c.shape[1] == rows
    n_tiles = u4l.shape[1] // rows

    def lat_tile(d, r):
        t = jnp.maximum(r - 1, 0)
        return jnp.where(d == 0, t, n_tiles - 1 - t)

    kern = functools.partial(_s5_state_kernel, nb=nb, n_chunks=rows // nb)
    return pl.pallas_call(
        kern,
        grid=(N_LANE_BLOCKS, 2, n_tiles + 1),
        in_specs=[pl.BlockSpec((1, rows, CHUNK_W), lambda j, d, r: (j, 0, 0)),
                  pl.BlockSpec((1, rows, CHUNK_W), lambda j, d, r: (j, lat_tile(d, r), 0)),
                  pl.BlockSpec((1, 1, CHUNK_W, 2 * STATE_W), lambda j, d, r: (d, j, 0, 0)),
                  pl.BlockSpec((1, 1, 2, 1, STATE_W), lambda j, d, r: (d, j, 0, 0, 0))],
        out_specs=pl.BlockSpec((1, 1, rows, 2 * STATE_W), lambda j, d, r: (d, j, lat_tile(d, r), 0)),
        out_shape=jax.ShapeDtypeStruct((2, N_LANE_BLOCKS, u4l.shape[1], 2 * STATE_W), BF16),
        scratch_shapes=[pltpu.VMEM((rows, 2 * STATE_W), F32),
                        pltpu.VMEM((nb, STATE_W), F32), pltpu.VMEM((nb, STATE_W), F32)],
        compiler_params=_params(3),
        name="s5_states",
    )(u4c, u4l, e_tab, a_tab)


def _s5_out_kernel(ul_ref, sin_ref, m_ref, f_ref, d_ref, y_ref):
    u = ul_ref[0]
    y_ref[0] = (u * d_ref[0] + _bdot(u.astype(BF16), m_ref[0])
                + _bdot(sin_ref[0, 0], f_ref[0, 0]) + _bdot(sin_ref[1, 0], f_ref[1, 0]))


def _s5_outputs(u4l, sin, m_tab, f_tab, d_tab):
    rows = S5_OUT_ROWS
    return pl.pallas_call(
        _s5_out_kernel,
        grid=(N_LANE_BLOCKS, u4l.shape[1] // rows),
        in_specs=[pl.BlockSpec((1, rows, CHUNK_W), lambda j, r: (j, r, 0)),
                  pl.BlockSpec((2, 1, rows, 2 * STATE_W), lambda j, r: (0, j, r, 0)),
                  pl.BlockSpec((1, CHUNK_W, CHUNK_W), lambda j, r: (j, 0, 0)),
                  pl.BlockSpec((2, 1, 2 * STATE_W, CHUNK_W), lambda j, r: (0, j, 0, 0)),
                  pl.BlockSpec((1, 1, CHUNK_W), lambda j, r: (j, 0, 0))],
        out_specs=pl.BlockSpec((1, rows, CHUNK_W), lambda j, r: (j, r, 0)),
        out_shape=jax.ShapeDtypeStruct(u4l.shape, F32),
        compiler_params=_params(2),
        name="s5_outputs",
    )(u4l, sin, m_tab, f_tab, d_tab)


def _merge_kernel(attn_ref, y4_ref, g_ref, x_ref, g1_ref, sh2_ref, sc2_ref, n2_ref,
                  sel_ref, wab_ref, wglu_ref, wout_ref, x1_ref, h2_ref):
    nb, tokens_all, _ = x_ref.shape
    tokens = tokens_all // MERGE_SPLIT
    rows = nb * tokens
    crow = tokens // CHUNK * nb

    def branch_dots(i):
        ts = slice(i * tokens, (i + 1) * tokens)
        act = [_gelu_tanh(y4_ref[j, i * crow:(i + 1) * crow]).astype(BF16)
               for j in range(N_LANE_BLOCKS)]
        act = jnp.concatenate(
            [jnp.concatenate([act[j][:, s * LANES:(s + 1) * LANES] for s in range(CHUNK)], axis=0)
             for j in range(N_LANE_BLOCKS)], axis=-1)
        act = _bdot(sel_ref[...], act).astype(BF16)
        p_attn = _bdot(attn_ref[:, ts].reshape(rows, ATTN_DIM), wab_ref[...])
        return p_attn, _bdot(act, wglu_ref[...])

    def mix_dot(i, p_attn, glu):
        ts = slice(i * tokens, (i + 1) * tokens)
        p_ssm = glu[:, :D_MODEL] * _sigmoid(glu[:, D_MODEL:])
        g = g_ref[:, ts].reshape(rows, 2 * D_MODEL)
        mix = _sigmoid(g[:, :D_MODEL]) * p_attn + _sigmoid(g[:, D_MODEL:]) * p_ssm
        return _bdot(mix.astype(BF16), wout_ref[...]).reshape(nb, tokens, D_MODEL)

    def finish(i, x_mix):
        ts = slice(i * tokens, (i + 1) * tokens)
        x1 = x_ref[:, ts] + g1_ref[...] * x_mix
        x1_ref[:, ts] = x1
        h2 = _rms(x1) * n2_ref[...]
        h2_ref[:, ts] = (h2 * (1.0 + sc2_ref[...]) + sh2_ref[...]).astype(h2_ref.dtype)

    dots = [None] * MERGE_SPLIT
    mixed = [None] * MERGE_SPLIT
    dots[0] = branch_dots(0)
    for i in range(MERGE_SPLIT):
        if i + 1 < MERGE_SPLIT:
            dots[i + 1] = branch_dots(i + 1)
        mixed[i] = mix_dot(i, *dots[i])
        if i > 0:
            finish(i - 1, mixed[i - 1])
    finish(MERGE_SPLIT - 1, mixed[MERGE_SPLIT - 1])


def _merge(attn, y4, g, x, g1, sh2, sc2, n2, wab, wglu, wout):
    b, l, _ = x.shape
    tt = TOK_TILE
    tok = lambda w: pl.BlockSpec((b, tt, w), lambda i: (0, i, 0))
    per_b = _const_spec((b, 1, D_MODEL))
    y4_spec = pl.BlockSpec((N_LANE_BLOCKS, tt // CHUNK * b, CHUNK_W), lambda i: (0, i, 0))
    sel = _chunk_row_select(b, tt // MERGE_SPLIT, inverse=True)
    return pl.pallas_call(
        _merge_kernel,
        grid=(l // tt,),
        in_specs=[tok(ATTN_DIM), y4_spec, tok(2 * D_MODEL), tok(D_MODEL),
                  per_b, per_b, per_b, _const_spec((1, D_MODEL)), _const_spec(sel.shape),
                  _const_spec(wab.shape), _const_spec(wglu.shape), _const_spec(wout.shape)],
        out_specs=[tok(D_MODEL), tok(D_MODEL)],
        out_shape=[jax.ShapeDtypeStruct((b, l, D_MODEL), F32),
                   jax.ShapeDtypeStruct((b, l, D_MODEL), BF16)],
        compiler_params=_params(1),
        name="merge",
    )(attn, y4, g, x, g1, sh2, sc2, n2, sel, wab, wglu, wout)


def _ffn_row_select(tm):
    seg_len = tm // SUB
    tile = np.zeros((tm, tm), np.float32)
    for seg in range(SUB):
        for i in range(seg_len):
            tile[SUB * i + seg, seg * seg_len + i] = 1.0
    halo = np.zeros((SUB, 2 * HALO), np.float32)
    halo[0, HALO - 1] = 1.0
    halo[1, HALO] = 1.0
    return jnp.asarray(tile, dtype=BF16), jnp.asarray(halo, dtype=BF16)


def _ffn_kernel(sel_ref, selh_ref, hp_ref, h_ref, hn_ref, x1_ref, g2_ref, wup_ref, cw_ref, cb_ref,
                wd_ref, fw_ref, o_ref, acc_scr, *, tm):
    j = pl.program_id(1)
    keep_prev = (j > 0).astype(F32)
    keep_next = (j < pl.num_programs(1) - 1).astype(F32)
    halo = jnp.concatenate([hp_ref[0], hn_ref[0]], axis=0)
    lhs = jnp.concatenate([_bdot(selh_ref[...], halo), _bdot(sel_ref[...], h_ref[0])],
                          axis=0).astype(BF16)
    sub = lax.broadcasted_iota(jnp.int32, (SUB, 1), 0)

    def up(f):
        return (_bdot(lhs, wup_ref[:, f * FF_CHUNK:(f + 1) * FF_CHUNK]),
                _bdot(lhs, wup_ref[:, D_FF + f * FF_CHUNK:D_FF + (f + 1) * FF_CHUNK]))

    def conv(zall, col):
        z = zall[SUB:]
        before = jnp.where(sub == 0, zall[0:1] * keep_prev, pltpu.roll(z[tm - SUB:], 1, 0))
        after = jnp.where(sub == SUB - 1, zall[1:2] * keep_next, pltpu.roll(z[:SUB], SUB - 1, 0))
        z_prev = jnp.concatenate([before, z[:tm - SUB]], axis=0)
        z_next = jnp.concatenate([z[SUB:], after], axis=0)
        cw = cw_ref[:, col:col + FF_CHUNK]
        return z_prev * cw[0:1] + z * cw[1:2] + z_next * cw[2:3] + cb_ref[:, col:col + FF_CHUNK]

    acts = []
    cur = up(0)
    for f in range(N_FF_CHUNKS):
        nxt = up(f + 1) if f + 1 < N_FF_CHUNKS else None
        val = conv(cur[0], f * FF_CHUNK)
        gate = conv(cur[1], D_FF + f * FF_CHUNK)
        acts.append((gate * _sigmoid(gate) * val).astype(BF16))
        cur = nxt
    acc = _bdot(jnp.concatenate(acts, axis=-1), wd_ref[...])
    seg_len = tm // SUB
    for k in range(D_MODEL // LANES):
        acc_scr[k] = acc[:, k * LANES:(k + 1) * LANES]
        for seg in range(SUB):
            o_ref[0, seg * seg_len:(seg + 1) * seg_len, k * LANES:(k + 1) * LANES] = (
                acc_scr[k, pl.ds(seg, seg_len, stride=SUB), :])
    x2 = x1_ref[0] + g2_ref[0] * o_ref[0]
    o_ref[0] = _rms(x2) * fw_ref[...]


def _conv_ffn(h2, x1, g2, wup, cw, cb, wd, fw):
    b, l, _ = x1.shape
    tm = TM_FFN
    nh = tm // HALO
    last = l // HALO - 1
    tok = lambda: pl.BlockSpec((1, tm, D_MODEL), lambda i, j: (i, j, 0))
    prev = pl.BlockSpec((1, HALO, D_MODEL), lambda i, j: (i, jnp.maximum(j * nh - 1, 0), 0))
    nxt = pl.BlockSpec((1, HALO, D_MODEL), lambda i, j: (i, jnp.minimum((j + 1) * nh, last), 0))
    per_b = pl.BlockSpec((1, 1, D_MODEL), lambda i, j: (i, 0, 0))
    once = lambda shape: pl.BlockSpec(shape, lambda *_: (0,) * len(shape),
                                      pipeline_mode=pl.Buffered(1))
    sel, selh = _ffn_row_select(tm)
    return pl.pallas_call(
        functools.partial(_ffn_kernel, tm=tm),
        grid=(b, l // tm),
        in_specs=[once(sel.shape), once(selh.shape), prev, tok(), nxt, tok(), per_b,
                  once(wup.shape), once(cw.shape), once(cb.shape), once(wd.shape),
                  _const_spec((1, D_MODEL))],
        out_specs=tok(),
        out_shape=jax.ShapeDtypeStruct((b, l, D_MODEL), F32),
        scratch_shapes=[pltpu.VMEM((D_MODEL // LANES, tm, LANES), F32)],
        compiler_params=_params(2),
        name="conv_ffn",
    )(sel, selh, h2, h2, h2, x1, g2, wup, cw, cb, wd, fw)


def _rope_tables(l):
    rows = jnp.repeat(jnp.arange(l // GRID_W, dtype=F32), GRID_W)
    cols = jnp.tile(jnp.arange(GRID_W, dtype=F32), l // GRID_W)
    inv_freq = ROPE_THETA ** (-jnp.arange(0, ROPE_AXIS_DIM, 2, dtype=F32) / ROPE_AXIS_DIM)
    ang = jnp.concatenate([rows[:, None] * inv_freq, cols[:, None] * inv_freq], axis=-1)
    cos = jnp.repeat(jnp.cos(ang), 2, axis=-1)
    sin = jnp.repeat(jnp.sin(ang), 2, axis=-1)
    odd = (jnp.arange(HEAD_DIM) % 2 == 1)[None, :]
    return cos, jnp.where(odd, sin, 0.0), jnp.where(odd, 0.0, -sin)


def kernel(x, c, ctx, c_ctx, w_mod, b_mod, norm1_w, norm2_w, w_in, q_norm_w, k_norm_w, w_attn_br,
           ssm_lambda_re, ssm_lambda_im, ssm_log_dt, ssm_b_re, ssm_b_im, ssm_c_re, ssm_c_im, ssm_d,
           w_glu, w_out, w_up, conv_w, conv_b, w_down, final_norm_w):
    b, l, d = x.shape
    lc = ctx.shape[1]
    assert w_mod.shape[0] == 1 and d == D_MODEL and l % TOK_TILE == 0 and lc % TOK_TILE == 0
    assert lc // CHUNK * b == S5_STATE_ROWS and (l // CHUNK * b) % S5_OUT_ROWS == 0
    layer = 0

    pad = (-(b + 1)) % 8
    c_rows = jnp.concatenate([c, c_ctx[None, :], jnp.zeros((pad, d), F32)], axis=0)
    mod = _modulation(c_rows, w_mod[layer], b_mod[layer])
    sh1, sc1, g1, sh2, sc2, g2 = [m[:b, None, :] for m in jnp.split(mod, N_MOD, axis=-1)]
    csh1, csc1 = mod[b:b + 1, None, :d], mod[b:b + 1, None, d:2 * d]

    w_in_b = w_in[layer].astype(BF16)
    qn = (q_norm_w[layer] * (1.0 / math.sqrt(HEAD_DIM))).reshape(1, HEAD_DIM)
    kn = k_norm_w[layer].reshape(1, HEAD_DIM)
    n1 = norm1_w[layer].reshape(1, d)

    no_rope = (jnp.ones((lc, HEAD_DIM), F32),) + (jnp.zeros((lc, HEAD_DIM), F32),) * 2
    q, k, v, u4l, g = _inproj(x, sh1, sc1, n1, w_in_b, qn, kn, _rope_tables(l), latent=True)
    kc, vc, u4c = _inproj(ctx, csh1, csc1, n1, w_in_b, qn, kn, no_rope, latent=False)

    attn = _attention(q, kc, k, vc, v)

    disc = _s5_discretise(ssm_lambda_re[layer], ssm_lambda_im[layer], ssm_log_dt[layer],
                          ssm_b_re[layer], ssm_b_im[layer])
    m_tab, e_tab, f_tab, a_pow = _s5_tables(*disc, ssm_c_re[layer], ssm_c_im[layer])
    a_tab = a_pow.reshape(2, N_LANE_BLOCKS, 2, 1, STATE_W)
    d_tab = jnp.tile(ssm_d[layer].reshape(N_LANE_BLOCKS, 1, LANES), (1, CHUNK, 1))
    d_tab = d_tab.reshape(N_LANE_BLOCKS, 1, CHUNK_W)
    sin = _s5_states(u4c, u4l, e_tab, a_tab, b)
    y4 = _s5_outputs(u4l, sin, m_tab, f_tab, d_tab)

    x1, h2 = _merge(attn, y4, g, x, g1, sh2, sc2, norm2_w[layer].reshape(1, d),
                    w_attn_br[layer].astype(BF16), w_glu[layer].astype(BF16),
                    w_out[layer].astype(BF16))

    return _conv_ffn(h2, x1, g2, w_up[layer].astype(BF16), conv_w[layer],
                     conv_b[layer].reshape(1, 2 * D_FF), w_down[layer].astype(BF16),
                     final_norm_w.reshape(1, d))
```

```python
import functools
import math

import jax
import jax.numpy as jnp
import numpy as np
from jax import lax
from jax.experimental import pallas as pl
from jax.experimental.pallas import tpu as pltpu

F32 = jnp.float32
BF16 = jnp.bfloat16

D_MODEL = 1024
GRID_W = 64
N_HEADS = 8
N_KV_HEADS = 2
HEAD_DIM = 128
HEADS_PER_KV = N_HEADS // N_KV_HEADS
ATTN_DIM = N_HEADS * HEAD_DIM
KV_DIM = N_KV_HEADS * HEAD_DIM
ROPE_THETA = 10000.0
ROPE_AXIS_DIM = HEAD_DIM // 2
SSM_DIM = 512
SSM_GROUP = 16
N_SSM_GROUPS = SSM_DIM // SSM_GROUP
SSM_STATE = 64
Q_END = ATTN_DIM
K_END = Q_END + KV_DIM
V_END = K_END + KV_DIM
U_END = V_END + SSM_DIM
D_IN = U_END + 2 * D_MODEL
D_FF = 2816
N_MOD = 6
EPS = 1e-6

LANES = 128
SUB = 8
CHUNK = 8
N_LANE_BLOCKS = SSM_DIM // LANES
GROUPS_PER_BLOCK = LANES // SSM_GROUP
CHUNK_W = CHUNK * LANES
STATE_W = GROUPS_PER_BLOCK * SSM_STATE

FF_CHUNK = 256
N_FF_CHUNKS = D_FF // FF_CHUNK
HALO = 16

VMEM_LIMIT = 56 * 1024 * 1024

TOK_TILE = 64
TQ_ATTN = 1024
PROJ_SPLIT = 4
MERGE_SPLIT = 4
TM_FFN = 512
S5_STATE_ROWS = 512
S5_OUT_ROWS = 1024


def _sigmoid(x):
    return 1.0 / (1.0 + jnp.exp(-x))


def _gelu_tanh(x):
    return 0.5 * x * (1.0 + jnp.tanh(math.sqrt(2.0 / math.pi) * (x + 0.044715 * (x * x * x))))


def _rms(x):
    return x * lax.rsqrt(jnp.mean(x * x, axis=-1, keepdims=True) + EPS)


def _bdot(a, b):
    return jnp.dot(a, b, preferred_element_type=F32)


def _const_spec(shape):
    nd = len(shape)
    return pl.BlockSpec(shape, lambda *_: (0,) * nd)


def _params(n_axes):
    return pltpu.CompilerParams(dimension_semantics=("arbitrary",) * n_axes,
                                vmem_limit_bytes=VMEM_LIMIT)


def _mod_kernel(c_ref, w_ref, b_ref, o_ref):
    c = c_ref[...]
    a = c * _sigmoid(c)
    o_ref[...] = jnp.dot(a, w_ref[...], precision=lax.Precision.HIGHEST,
                         preferred_element_type=F32) + b_ref[...]


def _modulation(c_rows, w_mod, b_mod):
    rows = c_rows.shape[0]
    n = w_mod.shape[1]
    tn = 1536
    return pl.pallas_call(
        _mod_kernel,
        grid=(n // tn,),
        in_specs=[pl.BlockSpec((rows, D_MODEL), lambda j: (0, 0)),
                  pl.BlockSpec((D_MODEL, tn), lambda j: (0, j)),
                  pl.BlockSpec((1, tn), lambda j: (0, j))],
        out_specs=pl.BlockSpec((rows, tn), lambda j: (0, j)),
        out_shape=jax.ShapeDtypeStruct((rows, n), F32),
        compiler_params=_params(1),
        name="modulation",
    )(c_rows, w_mod, b_mod.reshape(1, n))


def _norm_rope_store(t, nw, rope, out_ref, ts, col, nb):
    t = _rms(t) * nw
    if rope is not None:
        cos, sin_prev, sin_next = rope
        prev = pltpu.roll(t, 1, 1).reshape(nb, -1, HEAD_DIM)
        nxt = pltpu.roll(t, HEAD_DIM - 1, 1).reshape(nb, -1, HEAD_DIM)
        t3 = t.reshape(nb, -1, HEAD_DIM) * cos + prev * sin_prev + nxt * sin_next
    else:
        t3 = t.reshape(nb, -1, HEAD_DIM)
    out_ref[:, ts, col:col + HEAD_DIM] = t3.astype(out_ref.dtype)


def _chunk_row_select(nb, tokens, inverse=False):
    sel = np.zeros((nb * tokens, nb * tokens), np.float32)
    for b in range(nb):
        for t in range(tokens):
            k, s = divmod(t, CHUNK)
            sel[(s * (tokens // CHUNK) + k) * nb + b, b * tokens + t] = 1.0
    return jnp.asarray(sel.T if inverse else sel, dtype=BF16)


def _store_chunk_rows(u_perm, u4_ref, row0):
    rows = u_perm.shape[0] // CHUNK
    for j in range(N_LANE_BLOCKS):
        for s in range(CHUNK):
            u4_ref[j, row0:row0 + rows, s * LANES:(s + 1) * LANES] = (
                u_perm[s * rows:(s + 1) * rows, j * LANES:(j + 1) * LANES])


def _inproj_kernel(x_ref, sh_ref, sc_ref, n1_ref, w_ref, sel_ref, qn_ref, kn_ref,
                   cos_ref, sinp_ref, sinn_ref, *refs, latent):
    nb, tokens_all, _ = x_ref.shape
    tokens = tokens_all // PROJ_SPLIT
    rows = nb * tokens
    if latent:
        q_ref, k_ref, v_ref, u4_ref, g_ref = refs
    else:
        k_ref, v_ref, u4_ref = refs

    def lhs(i):
        ts = slice(i * tokens, (i + 1) * tokens)
        h = _rms(x_ref[:, ts]) * n1_ref[...]
        h = h * (1.0 + sc_ref[...]) + sh_ref[...]
        return h.reshape(rows, D_MODEL).astype(BF16)

    def project(i, hb):
        ts = slice(i * tokens, (i + 1) * tokens)
        rope = (cos_ref[ts], sinp_ref[ts], sinn_ref[ts]) if latent else None
        if latent:
            q = _bdot(hb, w_ref[:, :Q_END])
            for hd in range(N_HEADS):
                _norm_rope_store(q[:, hd * HEAD_DIM:(hd + 1) * HEAD_DIM], qn_ref[...], rope,
                                 q_ref, ts, hd * HEAD_DIM, nb)
        k = _bdot(hb, w_ref[:, Q_END:K_END])
        for hd in range(N_KV_HEADS):
            _norm_rope_store(k[:, hd * HEAD_DIM:(hd + 1) * HEAD_DIM], kn_ref[...], rope,
                             k_ref, ts, hd * HEAD_DIM, nb)
        v_ref[:, ts] = _bdot(hb, w_ref[:, K_END:V_END]).reshape(nb, tokens, KV_DIM).astype(v_ref.dtype)
        hb_perm = _bdot(sel_ref[...], hb).astype(BF16)
        _store_chunk_rows(_bdot(hb_perm, w_ref[:, V_END:U_END]), u4_ref, i * (rows // CHUNK))
        if latent:
            g_ref[:, ts] = _bdot(hb, w_ref[:, U_END:]).reshape(
                nb, tokens, 2 * D_MODEL).astype(g_ref.dtype)

    cur = lhs(0)
    for i in range(PROJ_SPLIT):
        nxt = lhs(i + 1) if i + 1 < PROJ_SPLIT else None
        project(i, cur)
        cur = nxt


def _inproj(x, sh1, sc1, n1, w_in_b, qn, kn, rope, latent):
    b, l, _ = x.shape
    tt = TOK_TILE
    tok = lambda w: pl.BlockSpec((b, tt, w), lambda i: (0, i, 0))
    mod_spec = _const_spec(sh1.shape)
    rope_spec = pl.BlockSpec((tt, HEAD_DIM), lambda i: (i, 0))
    u4_spec = pl.BlockSpec((N_LANE_BLOCKS, tt // CHUNK * b, CHUNK_W), lambda i: (0, i, 0))
    kv_u_specs = [tok(KV_DIM), tok(KV_DIM), u4_spec]
    sel = _chunk_row_select(b, tt // PROJ_SPLIT)
    kv_u_shapes = [jax.ShapeDtypeStruct((b, l, KV_DIM), BF16),
                   jax.ShapeDtypeStruct((b, l, KV_DIM), BF16),
                   jax.ShapeDtypeStruct((N_LANE_BLOCKS, l // CHUNK * b, CHUNK_W), F32)]
    if latent:
        out_specs = [tok(ATTN_DIM)] + kv_u_specs + [tok(2 * D_MODEL)]
        out_shape = ([jax.ShapeDtypeStruct((b, l, ATTN_DIM), BF16)] + kv_u_shapes
                     + [jax.ShapeDtypeStruct((b, l, 2 * D_MODEL), BF16)])
    else:
        out_specs, out_shape = kv_u_specs, kv_u_shapes
    return pl.pallas_call(
        functools.partial(_inproj_kernel, latent=latent),
        grid=(l // tt,),
        in_specs=[tok(D_MODEL), mod_spec, mod_spec, _const_spec((1, D_MODEL)),
                  _const_spec(w_in_b.shape), _const_spec(sel.shape),
                  _const_spec((1, HEAD_DIM)), _const_spec((1, HEAD_DIM)),
                  rope_spec, rope_spec, rope_spec],
        out_specs=out_specs,
        out_shape=out_shape,
        compiler_params=_params(1),
        name="inproj_latent" if latent else "inproj_ctx",
    )(x, sh1, sc1, n1, w_in_b, sel, qn, kn, *rope)


def _attn_kernel(q_ref, kc_ref, k_ref, vc_ref, v_ref, o_ref):
    keys = jnp.concatenate([kc_ref[0], k_ref[0]], axis=0)
    vals = jnp.concatenate([vc_ref[0], v_ref[0]], axis=0)
    vals = jnp.concatenate([vals, jnp.ones_like(vals)], axis=-1)
    nt = (((1,), (1,)), ((), ()))

    def scores(r):
        q = q_ref[0, :, r * HEAD_DIM:(r + 1) * HEAD_DIM]
        return lax.dot_general(q, keys, nt, preferred_element_type=F32)

    cur = scores(0)
    for r in range(HEADS_PER_KV):
        nxt = scores(r + 1) if r + 1 < HEADS_PER_KV else None
        p = jnp.exp(cur - jnp.max(cur, axis=-1, keepdims=True)).astype(BF16)
        o = _bdot(p, vals)
        o_ref[0, :, r * HEAD_DIM:(r + 1) * HEAD_DIM] = (
            o[:, :HEAD_DIM] / o[:, HEAD_DIM:]).astype(o_ref.dtype)
        cur = nxt


def _attention(q, kc, k, vc, v):
    b, l, _ = q.shape
    lc = kc.shape[1]
    tq = TQ_ATTN
    qspec = pl.BlockSpec((1, tq, HEADS_PER_KV * HEAD_DIM), lambda i, h, j: (i, j, h))
    kv = lambda n: pl.BlockSpec((1, n, HEAD_DIM), lambda i, h, j: (i, 0, h))
    return pl.pallas_call(
        _attn_kernel,
        grid=(b, N_KV_HEADS, l // tq),
        in_specs=[qspec, kv(lc), kv(l), kv(lc), kv(l)],
        out_specs=qspec,
        out_shape=jax.ShapeDtypeStruct((b, l, ATTN_DIM), BF16),
        compiler_params=_params(3),
        name="attention",
    )(q, kc, k, vc, v)


def _s5_discretise(lam_re, lam_im, log_dt, b_re, b_im):
    dt = jnp.exp(log_dt)[..., None]
    mag = jnp.exp(lam_re * dt)
    ang = lam_im * dt
    a_re = mag * jnp.cos(ang)
    a_im = mag * jnp.sin(ang)
    den = lam_re * lam_re + lam_im * lam_im
    nr = a_re - 1.0
    ni = a_im
    f_re = ((nr * lam_re + ni * lam_im) / den)[:, :, None, :]
    f_im = ((ni * lam_re - nr * lam_im) / den)[:, :, None, :]
    bt_re = b_re.transpose(0, 1, 3, 2)
    bt_im = b_im.transpose(0, 1, 3, 2)
    return (a_re[:, :, None, :], a_im[:, :, None, :],
            f_re * bt_re - f_im * bt_im, f_re * bt_im + f_im * bt_re)


def _s5_table_kernel(a_re_ref, a_im_ref, bb_re_ref, bb_im_ref, c_re_ref, c_im_ref,
                     m_ref, e_ref, f_ref, ap_ref, x_scr, e_scr, f_scr, c_scr):
    hp = lax.Precision.HIGHEST
    i32 = jnp.int32
    na, p_, t_ = GROUPS_PER_BLOCK, SSM_GROUP, CHUNK
    nt = (((1,), (1,)), ((), ()))
    for d in range(2):
        for a in range(na):
            a_re, a_im = a_re_ref[d, a], a_im_ref[d, a]
            b_re, b_im = bb_re_ref[d, a], bb_im_ref[d, a]
            c_re, c_im = c_re_ref[d, a], c_im_ref[d, a]
            c_scr[d, 0, a * p_:(a + 1) * p_, :] = c_re
            c_scr[d, 1, a * p_:(a + 1) * p_, :] = c_im
            pw_re, pw_im = jnp.ones_like(a_re), jnp.zeros_like(a_im)
            for tau in range(t_ + 1):
                if tau < t_:
                    pb_re = b_re * pw_re - b_im * pw_im
                    pb_im = b_re * pw_im + b_im * pw_re
                    r0 = (tau * na + a) * p_
                    x_scr[d, 0, r0:r0 + p_, :] = pb_re
                    x_scr[d, 1, r0:r0 + p_, :] = pb_im
                    s_in = t_ - 1 - tau if d == 0 else tau
                    r0 = (s_in * na + a) * p_
                    e_scr[d, 0, r0:r0 + p_, :] = pb_re
                    e_scr[d, 1, r0:r0 + p_, :] = pb_im
                if tau > 0:
                    t_out = tau - 1 if d == 0 else t_ - tau
                    r0 = (t_out * na + a) * p_
                    f_scr[d, 0, r0:r0 + p_, :] = c_re * pw_re - c_im * pw_im
                    f_scr[d, 1, r0:r0 + p_, :] = -(c_re * pw_im + c_im * pw_re)
                if tau == t_:
                    ap_ref[d, 0, 0, a] = pw_re
                    ap_ref[d, 0, 1, a] = pw_im
                else:
                    pw_re, pw_im = pw_re * a_re - pw_im * a_im, pw_re * a_im + pw_im * a_re

    grp_bits = p_.bit_length() - 1
    st_bits = SSM_STATE.bit_length() - 1
    blk_bits = na.bit_length() - 1
    gmask = na - 1
    r_w = lax.broadcasted_iota(i32, (CHUNK_W, 1), 0)
    c_w = lax.broadcasted_iota(i32, (1, CHUNK_W), 1)
    c_l = lax.broadcasted_iota(i32, (1, LANES), 1)
    r_n = lax.broadcasted_iota(i32, (SSM_STATE, 1), 0)
    c_n = lax.broadcasted_iota(i32, (1, SSM_STATE), 1)
    grp_tok_r = (r_w >> grp_bits) & gmask
    grp_tok_c = (c_w >> grp_bits) & gmask
    grp_st_r = (r_w >> st_bits) & gmask
    grp_st_c = (c_w >> st_bits) & gmask

    lag = []
    for d in range(2):
        full = (lax.dot_general(x_scr[d, 0], c_scr[d, 0], nt, precision=hp, preferred_element_type=F32)
                - lax.dot_general(x_scr[d, 1], c_scr[d, 1], nt, precision=hp, preferred_element_type=F32))
        lag.append(jnp.where(grp_tok_r == (c_l >> grp_bits), full, 0.0))
    for s in range(t_):
        for t in range(t_):
            if t > s:
                blk = lag[0][(t - s) * LANES:(t - s + 1) * LANES]
            elif s > t:
                blk = lag[1][(s - t) * LANES:(s - t + 1) * LANES]
            else:
                blk = lag[0][:LANES] + lag[1][:LANES]
            m_ref[0, s * LANES:(s + 1) * LANES, t * LANES:(t + 1) * LANES] = blk.astype(m_ref.dtype)

    for d in range(2):
        e_full = jnp.zeros((CHUNK_W, 2 * STATE_W), F32)
        f_full = jnp.zeros((2 * STATE_W, CHUNK_W), F32)
        for c in range(2):
            rep = (((c_w >> (st_bits + blk_bits)) == c) & ((c_w & (SSM_STATE - 1)) == r_n))
            rep = rep.astype(F32).astype(BF16)
            rep_t = (((r_w >> (st_bits + blk_bits)) == c) & ((r_w & (SSM_STATE - 1)) == c_n))
            rep_t = rep_t.astype(F32).astype(BF16)
            e_full = e_full + _bdot(e_scr[d, c].astype(BF16), rep)
            f_full = f_full + lax.dot_general(rep_t, f_scr[d, c].astype(BF16), nt,
                                              preferred_element_type=F32)
        e_ref[d, 0] = jnp.where(grp_tok_r == grp_st_c, e_full, 0.0).astype(e_ref.dtype)
        f_ref[d, 0] = jnp.where(grp_st_r == grp_tok_c, f_full, 0.0).astype(f_ref.dtype)


def _s5_tables(a_re, a_im, bb_re, bb_im, c_re, c_im):
    nj, na = N_LANE_BLOCKS, GROUPS_PER_BLOCK
    grp = lambda r: pl.BlockSpec((2, na, r, SSM_STATE), lambda j: (0, j, 0, 0))
    dj = lambda r, c: pl.BlockSpec((2, 1, r, c), lambda j: (0, j, 0, 0))
    src = lambda: pltpu.VMEM((2, 2, CHUNK_W, SSM_STATE), F32)
    return pl.pallas_call(
        _s5_table_kernel,
        grid=(nj,),
        in_specs=[grp(1), grp(1), grp(SSM_GROUP), grp(SSM_GROUP), grp(SSM_GROUP), grp(SSM_GROUP)],
        out_specs=[pl.BlockSpec((1, CHUNK_W, CHUNK_W), lambda j: (j, 0, 0)),
                   dj(CHUNK_W, 2 * STATE_W), dj(2 * STATE_W, CHUNK_W),
                   pl.BlockSpec((2, 1, 2, na, 1, SSM_STATE), lambda j: (0, j, 0, 0, 0, 0))],
        out_shape=[jax.ShapeDtypeStruct((nj, CHUNK_W, CHUNK_W), BF16),
                   jax.ShapeDtypeStruct((2, nj, CHUNK_W, 2 * STATE_W), BF16),
                   jax.ShapeDtypeStruct((2, nj, 2 * STATE_W, CHUNK_W), BF16),
                   jax.ShapeDtypeStruct((2, nj, 2, na, 1, SSM_STATE), F32)],
        scratch_shapes=[src(), src(), src(), pltpu.VMEM((2, 2, LANES, SSM_STATE), F32)],
        compiler_params=_params(1),
        name="s5_tables",
    )(a_re, a_im, bb_re, bb_im, c_re, c_im)


def _s5_state_kernel(uc_ref, ul_ref, e_ref, a_ref, sin_ref, c_scr, sre_scr, sim_scr, *, nb, n_chunks):
    d = pl.program_id(1)
    r = pl.program_id(2)
    a_re = jnp.broadcast_to(a_ref[0, 0, 0], (nb, STATE_W))
    a_im = jnp.broadcast_to(a_ref[0, 0, 1], (nb, STATE_W))

    def scan(store):
        def body(i, s):
            kk = jnp.where(d == 0, i, n_chunks - 1 - i)
            row = pl.multiple_of(kk * nb, nb)
            s_re, s_im = s
            if store:
                sin_ref[0, 0, pl.ds(row, nb), 0:STATE_W] = s_re.astype(sin_ref.dtype)
                sin_ref[0, 0, pl.ds(row, nb), STATE_W:] = s_im.astype(sin_ref.dtype)
            c_re = c_scr[pl.ds(row, nb), 0:STATE_W]
            c_im = c_scr[pl.ds(row, nb), STATE_W:]
            return (s_re * a_re - s_im * a_im + c_re, s_re * a_im + s_im * a_re + c_im)

        s = lax.fori_loop(0, n_chunks, body, (sre_scr[...], sim_scr[...]))
        sre_scr[...] = s[0]
        sim_scr[...] = s[1]

    @pl.when(r == 0)
    def _():
        sre_scr[...] = jnp.zeros_like(sre_scr)
        sim_scr[...] = jnp.zeros_like(sim_scr)
        c_scr[...] = _bdot(uc_ref[0].astype(BF16), e_ref[0, 0])
        scan(False)

    @pl.when(r > 0)
    def _():
        c_scr[...] = _bdot(ul_ref[0].astype(BF16), e_ref[0, 0])
        scan(True)


def _s5_states(u4c, u4l, e_tab, a_tab, nb):
    rows = S5_STATE_ROWS
    assert u4c.shape[1] == rows
    n_tiles = u4l.shape[1] // rows

    def lat_tile(d, r):
        t = jnp.maximum(r - 1, 0)
        return jnp.where(d == 0, t, n_tiles - 1 - t)

    kern = functools.partial(_s5_state_kernel, nb=nb, n_chunks=rows // nb)
    return pl.pallas_call(
        kern,
        grid=(N_LANE_BLOCKS, 2, n_tiles + 1),
        in_specs=[pl.BlockSpec((1, rows, CHUNK_W), lambda j, d, r: (j, 0, 0)),
                  pl.BlockSpec((1, rows, CHUNK_W), lambda j, d, r: (j, lat_tile(d, r), 0)),
                  pl.BlockSpec((1, 1, CHUNK_W, 2 * STATE_W), lambda j, d, r: (d, j, 0, 0)),
                  pl.BlockSpec((1, 1, 2, 1, STATE_W), lambda j, d, r: (d, j, 0, 0, 0))],
        out_specs=pl.BlockSpec((1, 1, rows, 2 * STATE_W), lambda j, d, r: (d, j, lat_tile(d, r), 0)),
        out_shape=jax.ShapeDtypeStruct((2, N_LANE_BLOCKS, u4l.shape[1], 2 * STATE_W), BF16),
        scratch_shapes=[pltpu.VMEM((rows, 2 * STATE_W), F32),
                        pltpu.VMEM((nb, STATE_W), F32), pltpu.VMEM((nb, STATE_W), F32)],
        compiler_params=_params(3),
        name="s5_states",
    )(u4c, u4l, e_tab, a_tab)


def _s5_out_kernel(ul_ref, sin_ref, m_ref, f_ref, d_ref, y_ref):
    u = ul_ref[0]
    y_ref[0] = (u * d_ref[0] + _bdot(u.astype(BF16), m_ref[0])
                + _bdot(sin_ref[0, 0], f_ref[0, 0]) + _bdot(sin_ref[1, 0], f_ref[1, 0]))


def _s5_outputs(u4l, sin, m_tab, f_tab, d_tab):
    rows = S5_OUT_ROWS
    return pl.pallas_call(
        _s5_out_kernel,
        grid=(N_LANE_BLOCKS, u4l.shape[1] // rows),
        in_specs=[pl.BlockSpec((1, rows, CHUNK_W), lambda j, r: (j, r, 0)),
                  pl.BlockSpec((2, 1, rows, 2 * STATE_W), lambda j, r: (0, j, r, 0)),
                  pl.BlockSpec((1, CHUNK_W, CHUNK_W), lambda j, r: (j, 0, 0)),
                  pl.BlockSpec((2, 1, 2 * STATE_W, CHUNK_W), lambda j, r: (0, j, 0, 0)),
                  pl.BlockSpec((1, 1, CHUNK_W), lambda j, r: (j, 0, 0))],
        out_specs=pl.BlockSpec((1, rows, CHUNK_W), lambda j, r: (j, r, 0)),
        out_shape=jax.ShapeDtypeStruct(u4l.shape, F32),
        compiler_params=_params(2),
        name="s5_outputs",
    )(u4l, sin, m_tab, f_tab, d_tab)


def _merge_kernel(attn_ref, y4_ref, g_ref, x_ref, g1_ref, sh2_ref, sc2_ref, n2_ref,
                  sel_ref, wab_ref, wglu_ref, wout_ref, x1_ref, h2_ref):
    nb, tokens_all, _ = x_ref.shape
    tokens = tokens_all // MERGE_SPLIT
    rows = nb * tokens
    crow = tokens // CHUNK * nb

    def branch_dots(i):
        ts = slice(i * tokens, (i + 1) * tokens)
        act = [_gelu_tanh(y4_ref[j, i * crow:(i + 1) * crow]).astype(BF16)
               for j in range(N_LANE_BLOCKS)]
        act = jnp.concatenate(
            [jnp.concatenate([act[j][:, s * LANES:(s + 1) * LANES] for s in range(CHUNK)], axis=0)
             for j in range(N_LANE_BLOCKS)], axis=-1)
        act = _bdot(sel_ref[...], act).astype(BF16)
        p_attn = _bdot(attn_ref[:, ts].reshape(rows, ATTN_DIM), wab_ref[...])
        return p_attn, _bdot(act, wglu_ref[...])

    def mix_dot(i, p_attn, glu):
        ts = slice(i * tokens, (i + 1) * tokens)
        p_ssm = glu[:, :D_MODEL] * _sigmoid(glu[:, D_MODEL:])
        g = g_ref[:, ts].reshape(rows, 2 * D_MODEL).astype(F32)
        mix = _sigmoid(g[:, :D_MODEL]) * p_attn + _sigmoid(g[:, D_MODEL:]) * p_ssm
        return _bdot(mix.astype(BF16), wout_ref[...]).reshape(nb, tokens, D_MODEL)

    def finish(i, x_mix):
        ts = slice(i * tokens, (i + 1) * tokens)
        x1 = x_ref[:, ts] + g1_ref[...] * x_mix
        x1_ref[:, ts] = x1
        h2 = _rms(x1) * n2_ref[...]
        h2_ref[:, ts] = (h2 * (1.0 + sc2_ref[...]) + sh2_ref[...]).astype(h2_ref.dtype)

    dots = [None] * MERGE_SPLIT
    mixed = [None] * MERGE_SPLIT
    dots[0] = branch_dots(0)
    for i in range(MERGE_SPLIT):
        if i + 1 < MERGE_SPLIT:
            dots[i + 1] = branch_dots(i + 1)
        mixed[i] = mix_dot(i, *dots[i])
        if i > 0:
            finish(i - 1, mixed[i - 1])
    finish(MERGE_SPLIT - 1, mixed[MERGE_SPLIT - 1])


def _merge(attn, y4, g, x, g1, sh2, sc2, n2, wab, wglu, wout):
    b, l, _ = x.shape
    tt = TOK_TILE
    tok = lambda w: pl.BlockSpec((b, tt, w), lambda i: (0, i, 0))
    per_b = _const_spec((b, 1, D_MODEL))
    y4_spec = pl.BlockSpec((N_LANE_BLOCKS, tt // CHUNK * b, CHUNK_W), lambda i: (0, i, 0))
    sel = _chunk_row_select(b, tt // MERGE_SPLIT, inverse=True)
    return pl.pallas_call(
        _merge_kernel,
        grid=(l // tt,),
        in_specs=[tok(ATTN_DIM), y4_spec, tok(2 * D_MODEL), tok(D_MODEL),
                  per_b, per_b, per_b, _const_spec((1, D_MODEL)), _const_spec(sel.shape),
                  _const_spec(wab.shape), _const_spec(wglu.shape), _const_spec(wout.shape)],
        out_specs=[tok(D_MODEL), tok(D_MODEL)],
        out_shape=[jax.ShapeDtypeStruct((b, l, D_MODEL), F32),
                   jax.ShapeDtypeStruct((b, l, D_MODEL), BF16)],
        compiler_params=_params(1),
        name="merge",
    )(attn, y4, g, x, g1, sh2, sc2, n2, sel, wab, wglu, wout)


def _ffn_row_select(tm):
    seg_len = tm // SUB
    tile = np.zeros((tm, tm), np.float32)
    for seg in range(SUB):
        for i in range(seg_len):
            tile[SUB * i + seg, seg * seg_len + i] = 1.0
    halo = np.zeros((SUB, 2 * HALO), np.float32)
    halo[0, HALO - 1] = 1.0
    halo[1, HALO] = 1.0
    return jnp.asarray(tile, dtype=BF16), jnp.asarray(halo, dtype=BF16)


def _ffn_kernel(sel_ref, selh_ref, hp_ref, h_ref, hn_ref, x1_ref, g2_ref, wup_ref, cw_ref, cb_ref,
                wd_ref, fw_ref, o_ref, acc_scr, *, tm):
    j = pl.program_id(1)
    keep_prev = (j > 0).astype(F32)
    keep_next = (j < pl.num_programs(1) - 1).astype(F32)
    halo = jnp.concatenate([hp_ref[0], hn_ref[0]], axis=0)
    lhs = jnp.concatenate([_bdot(selh_ref[...], halo), _bdot(sel_ref[...], h_ref[0])],
                          axis=0).astype(BF16)
    sub = lax.broadcasted_iota(jnp.int32, (SUB, 1), 0)

    def up(f):
        return (_bdot(lhs, wup_ref[:, f * FF_CHUNK:(f + 1) * FF_CHUNK]),
                _bdot(lhs, wup_ref[:, D_FF + f * FF_CHUNK:D_FF + (f + 1) * FF_CHUNK]))

    def conv(zall, col):
        z = zall[SUB:]
        before = jnp.where(sub == 0, zall[0:1] * keep_prev, pltpu.roll(z[tm - SUB:], 1, 0))
        after = jnp.where(sub == SUB - 1, zall[1:2] * keep_next, pltpu.roll(z[:SUB], SUB - 1, 0))
        z_prev = jnp.concatenate([before, z[:tm - SUB]], axis=0)
        z_next = jnp.concatenate([z[SUB:], after], axis=0)
        cw = cw_ref[:, col:col + FF_CHUNK]
        return z_prev * cw[0:1] + z * cw[1:2] + z_next * cw[2:3] + cb_ref[:, col:col + FF_CHUNK]

    acts = []
    cur = up(0)
    for f in range(N_FF_CHUNKS):
        nxt = up(f + 1) if f + 1 < N_FF_CHUNKS else None
        val = conv(cur[0], f * FF_CHUNK)
        gate = conv(cur[1], D_FF + f * FF_CHUNK)
        acts.append((gate * _sigmoid(gate) * val).astype(BF16))
        cur = nxt
    acc = _bdot(jnp.concatenate(acts, axis=-1), wd_ref[...])
    seg_len = tm // SUB
    for k in range(D_MODEL // LANES):
        acc_scr[k] = acc[:, k * LANES:(k + 1) * LANES]
        for seg in range(SUB):
            o_ref[0, seg * seg_len:(seg + 1) * seg_len, k * LANES:(k + 1) * LANES] = (
                acc_scr[k, pl.ds(seg, seg_len, stride=SUB), :])
    x2 = x1_ref[0] + g2_ref[0] * o_ref[0]
    o_ref[0] = _rms(x2) * fw_ref[...]


def _conv_ffn(h2, x1, g2, wup, cw, cb, wd, fw):
    b, l, _ = x1.shape
    tm = TM_FFN
    nh = tm // HALO
    last = l // HALO - 1
    tok = lambda: pl.BlockSpec((1, tm, D_MODEL), lambda i, j: (i, j, 0))
    prev = pl.BlockSpec((1, HALO, D_MODEL), lambda i, j: (i, jnp.maximum(j * nh - 1, 0), 0))
    nxt = pl.BlockSpec((1, HALO, D_MODEL), lambda i, j: (i, jnp.minimum((j + 1) * nh, last), 0))
    per_b = pl.BlockSpec((1, 1, D_MODEL), lambda i, j: (i, 0, 0))
    once = lambda shape: pl.BlockSpec(shape, lambda *_: (0,) * len(shape),
                                      pipeline_mode=pl.Buffered(1))
    sel, selh = _ffn_row_select(tm)
    return pl.pallas_call(
        functools.partial(_ffn_kernel, tm=tm),
        grid=(b, l // tm),
        in_specs=[once(sel.shape), once(selh.shape), prev, tok(), nxt, tok(), per_b,
                  once(wup.shape), once(cw.shape), once(cb.shape), once(wd.shape),
                  _const_spec((1, D_MODEL))],
        out_specs=tok(),
        out_shape=jax.ShapeDtypeStruct((b, l, D_MODEL), F32),
        scratch_shapes=[pltpu.VMEM((D_MODEL // LANES, tm, LANES), F32)],
        compiler_params=_params(2),
        name="conv_ffn",
    )(sel, selh, h2, h2, h2, x1, g2, wup, cw, cb, wd, fw)


def _rope_tables(l):
    rows = jnp.repeat(jnp.arange(l // GRID_W, dtype=F32), GRID_W)
    cols = jnp.tile(jnp.arange(GRID_W, dtype=F32), l // GRID_W)
    inv_freq = ROPE_THETA ** (-jnp.arange(0, ROPE_AXIS_DIM, 2, dtype=F32) / ROPE_AXIS_DIM)
    ang = jnp.concatenate([rows[:, None] * inv_freq, cols[:, None] * inv_freq], axis=-1)
    cos = jnp.repeat(jnp.cos(ang), 2, axis=-1)
    sin = jnp.repeat(jnp.sin(ang), 2, axis=-1)
    odd = (jnp.arange(HEAD_DIM) % 2 == 1)[None, :]
    return cos, jnp.where(odd, sin, 0.0), jnp.where(odd, 0.0, -sin)


def kernel(x, c, ctx, c_ctx, w_mod, b_mod, norm1_w, norm2_w, w_in, q_norm_w, k_norm_w, w_attn_br,
           ssm_lambda_re, ssm_lambda_im, ssm_log_dt, ssm_b_re, ssm_b_im, ssm_c_re, ssm_c_im, ssm_d,
           w_glu, w_out, w_up, conv_w, conv_b, w_down, final_norm_w):
    b, l, d = x.shape
    lc = ctx.shape[1]
    assert w_mod.shape[0] == 1 and d == D_MODEL and l % TOK_TILE == 0 and lc % TOK_TILE == 0
    assert lc // CHUNK * b == S5_STATE_ROWS and (l // CHUNK * b) % S5_OUT_ROWS == 0
    layer = 0

    pad = (-(b + 1)) % 8
    c_rows = jnp.concatenate([c, c_ctx[None, :], jnp.zeros((pad, d), F32)], axis=0)
    mod = _modulation(c_rows, w_mod[layer], b_mod[layer])
    sh1, sc1, g1, sh2, sc2, g2 = [m[:b, None, :] for m in jnp.split(mod, N_MOD, axis=-1)]
    csh1, csc1 = mod[b:b + 1, None, :d], mod[b:b + 1, None, d:2 * d]

    w_in_b = w_in[layer].astype(BF16)
    qn = (q_norm_w[layer] * (1.0 / math.sqrt(HEAD_DIM))).reshape(1, HEAD_DIM)
    kn = k_norm_w[layer].reshape(1, HEAD_DIM)
    n1 = norm1_w[layer].reshape(1, d)

    no_rope = (jnp.ones((lc, HEAD_DIM), F32),) + (jnp.zeros((lc, HEAD_DIM), F32),) * 2
    q, k, v, u4l, g = _inproj(x, sh1, sc1, n1, w_in_b, qn, kn, _rope_tables(l), latent=True)
    kc, vc, u4c = _inproj(ctx, csh1, csc1, n1, w_in_b, qn, kn, no_rope, latent=False)

    attn = _attention(q, kc, k, vc, v)

    disc = _s5_discretise(ssm_lambda_re[layer], ssm_lambda_im[layer], ssm_log_dt[layer],
                          ssm_b_re[layer], ssm_b_im[layer])
    m_tab, e_tab, f_tab, a_pow = _s5_tables(*disc, ssm_c_re[layer], ssm_c_im[layer])
    a_tab = a_pow.reshape(2, N_LANE_BLOCKS, 2, 1, STATE_W)
    d_tab = jnp.tile(ssm_d[layer].reshape(N_LANE_BLOCKS, 1, LANES), (1, CHUNK, 1))
    d_tab = d_tab.reshape(N_LANE_BLOCKS, 1, CHUNK_W)
    sin = _s5_states(u4c, u4l, e_tab, a_tab, b)
    y4 = _s5_outputs(u4l, sin, m_tab, f_tab, d_tab)

    x1, h2 = _merge(attn, y4, g, x, g1, sh2, sc2, norm2_w[layer].reshape(1, d),
                    w_attn_br[layer].astype(BF16), w_glu[layer].astype(BF16),
                    w_out[layer].astype(BF16))

    return _conv_ffn(h2, x1, g2, w_up[layer].astype(BF16), conv_w[layer],
                     conv_b[layer].reshape(1, 2 * D_FF), w_down[layer].astype(BF16),
                     final_norm_w.reshape(1, d))
```

```python
import functools
import math

import jax
import jax.numpy as jnp
import numpy as np
from jax import lax
from jax.experimental import pallas as pl
from jax.experimental.pallas import tpu as pltpu

F32 = jnp.float32
BF16 = jnp.bfloat16

D_MODEL = 1024
GRID_W = 64
N_HEADS = 8
N_KV_HEADS = 2
HEAD_DIM = 128
HEADS_PER_KV = N_HEADS // N_KV_HEADS
ATTN_DIM = N_HEADS * HEAD_DIM
KV_DIM = N_KV_HEADS * HEAD_DIM
ROPE_THETA = 10000.0
ROPE_AXIS_DIM = HEAD_DIM // 2
SSM_DIM = 512
SSM_GROUP = 16
N_SSM_GROUPS = SSM_DIM // SSM_GROUP
SSM_STATE = 64
Q_END = ATTN_DIM
K_END = Q_END + KV_DIM
V_END = K_END + KV_DIM
U_END = V_END + SSM_DIM
D_IN = U_END + 2 * D_MODEL
D_FF = 2816
N_MOD = 6
EPS = 1e-6

LANES = 128
SUB = 8
CHUNK = 8
N_LANE_BLOCKS = SSM_DIM // LANES
GROUPS_PER_BLOCK = LANES // SSM_GROUP
CHUNK_W = CHUNK * LANES
STATE_W = GROUPS_PER_BLOCK * SSM_STATE

FF_CHUNK = 256
N_FF_CHUNKS = D_FF // FF_CHUNK
HALO = 16

VMEM_LIMIT = 56 * 1024 * 1024

TOK_TILE = 64
TQ_ATTN = 1024
PROJ_SPLIT = 4
MERGE_SPLIT = 4
TM_FFN = 512
S5_STATE_ROWS = 1024
S5_STATE_PART = 512
S5_OUT_ROWS = 1024


def _sigmoid(x):
    return 1.0 / (1.0 + jnp.exp(-x))


def _gelu_tanh(x):
    return 0.5 * x * (1.0 + jnp.tanh(math.sqrt(2.0 / math.pi) * (x + 0.044715 * (x * x * x))))


def _rms(x):
    return x * lax.rsqrt(jnp.mean(x * x, axis=-1, keepdims=True) + EPS)


def _bdot(a, b):
    return jnp.dot(a, b, preferred_element_type=F32)


def _const_spec(shape):
    nd = len(shape)
    return pl.BlockSpec(shape, lambda *_: (0,) * nd)


def _params(n_axes):
    return pltpu.CompilerParams(dimension_semantics=("arbitrary",) * n_axes,
                                vmem_limit_bytes=VMEM_LIMIT)


def _mod_kernel(c_ref, w_ref, b_ref, o_ref):
    c = c_ref[...]
    a = c * _sigmoid(c)
    o_ref[...] = jnp.dot(a, w_ref[...], precision=lax.Precision.HIGHEST,
                         preferred_element_type=F32) + b_ref[...]


def _modulation(c_rows, w_mod, b_mod):
    rows = c_rows.shape[0]
    n = w_mod.shape[1]
    tn = 1536
    return pl.pallas_call(
        _mod_kernel,
        grid=(n // tn,),
        in_specs=[pl.BlockSpec((rows, D_MODEL), lambda j: (0, 0)),
                  pl.BlockSpec((D_MODEL, tn), lambda j: (0, j)),
                  pl.BlockSpec((1, tn), lambda j: (0, j))],
        out_specs=pl.BlockSpec((rows, tn), lambda j: (0, j)),
        out_shape=jax.ShapeDtypeStruct((rows, n), F32),
        compiler_params=_params(1),
        name="modulation",
    )(c_rows, w_mod, b_mod.reshape(1, n))


def _norm_rope_store(t, nw, rope, out_ref, ts, col, nb):
    t = _rms(t) * nw
    if rope is not None:
        cos, sin_prev, sin_next = rope
        prev = pltpu.roll(t, 1, 1).reshape(nb, -1, HEAD_DIM)
        nxt = pltpu.roll(t, HEAD_DIM - 1, 1).reshape(nb, -1, HEAD_DIM)
        t3 = t.reshape(nb, -1, HEAD_DIM) * cos + prev * sin_prev + nxt * sin_next
    else:
        t3 = t.reshape(nb, -1, HEAD_DIM)
    out_ref[:, ts, col:col + HEAD_DIM] = t3.astype(out_ref.dtype)


def _chunk_row_select(nb, tokens, inverse=False):
    sel = np.zeros((nb * tokens, nb * tokens), np.float32)
    for b in range(nb):
        for t in range(tokens):
            k, s = divmod(t, CHUNK)
            sel[(s * (tokens // CHUNK) + k) * nb + b, b * tokens + t] = 1.0
    return jnp.asarray(sel.T if inverse else sel, dtype=BF16)


def _store_chunk_rows(u_perm, u4_ref, row0):
    rows = u_perm.shape[0] // CHUNK
    for j in range(N_LANE_BLOCKS):
        for s in range(CHUNK):
            u4_ref[j, row0:row0 + rows, s * LANES:(s + 1) * LANES] = (
                u_perm[s * rows:(s + 1) * rows, j * LANES:(j + 1) * LANES].astype(u4_ref.dtype))


def _inproj_kernel(x_ref, sh_ref, sc_ref, n1_ref, w_ref, sel_ref, qn_ref, kn_ref,
                   cos_ref, sinp_ref, sinn_ref, *refs, latent):
    nb, tokens_all, _ = x_ref.shape
    tokens = tokens_all // PROJ_SPLIT
    rows = nb * tokens
    if latent:
        q_ref, k_ref, v_ref, u4_ref, g_ref = refs
    else:
        k_ref, v_ref, u4_ref = refs

    def lhs(i):
        ts = slice(i * tokens, (i + 1) * tokens)
        h = _rms(x_ref[:, ts]) * n1_ref[...]
        h = h * (1.0 + sc_ref[...]) + sh_ref[...]
        return h.reshape(rows, D_MODEL).astype(BF16)

    def project(i, hb):
        ts = slice(i * tokens, (i + 1) * tokens)
        rope = (cos_ref[ts], sinp_ref[ts], sinn_ref[ts]) if latent else None
        if latent:
            q = _bdot(hb, w_ref[:, :Q_END])
            for hd in range(N_HEADS):
                _norm_rope_store(q[:, hd * HEAD_DIM:(hd + 1) * HEAD_DIM], qn_ref[...], rope,
                                 q_ref, ts, hd * HEAD_DIM, nb)
        k = _bdot(hb, w_ref[:, Q_END:K_END])
        for hd in range(N_KV_HEADS):
            _norm_rope_store(k[:, hd * HEAD_DIM:(hd + 1) * HEAD_DIM], kn_ref[...], rope,
                             k_ref, ts, hd * HEAD_DIM, nb)
        v_ref[:, ts] = _bdot(hb, w_ref[:, K_END:V_END]).reshape(nb, tokens, KV_DIM).astype(v_ref.dtype)
        hb_perm = _bdot(sel_ref[...], hb).astype(BF16)
        _store_chunk_rows(_bdot(hb_perm, w_ref[:, V_END:U_END]), u4_ref, i * (rows // CHUNK))
        if latent:
            g_ref[:, ts] = _bdot(hb, w_ref[:, U_END:]).reshape(
                nb, tokens, 2 * D_MODEL).astype(g_ref.dtype)

    cur = lhs(0)
    for i in range(PROJ_SPLIT):
        nxt = lhs(i + 1) if i + 1 < PROJ_SPLIT else None
        project(i, cur)
        cur = nxt


def _inproj(x, sh1, sc1, n1, w_in_b, qn, kn, rope, latent):
    b, l, _ = x.shape
    tt = TOK_TILE
    tok = lambda w: pl.BlockSpec((b, tt, w), lambda i: (0, i, 0))
    mod_spec = _const_spec(sh1.shape)
    rope_spec = pl.BlockSpec((tt, HEAD_DIM), lambda i: (i, 0))
    u4_spec = pl.BlockSpec((N_LANE_BLOCKS, tt // CHUNK * b, CHUNK_W), lambda i: (0, i, 0))
    kv_u_specs = [tok(KV_DIM), tok(KV_DIM), u4_spec]
    sel = _chunk_row_select(b, tt // PROJ_SPLIT)
    kv_u_shapes = [jax.ShapeDtypeStruct((b, l, KV_DIM), BF16),
                   jax.ShapeDtypeStruct((b, l, KV_DIM), BF16),
                   jax.ShapeDtypeStruct((N_LANE_BLOCKS, l // CHUNK * b, CHUNK_W), BF16)]
    if latent:
        out_specs = [tok(ATTN_DIM)] + kv_u_specs + [tok(2 * D_MODEL)]
        out_shape = ([jax.ShapeDtypeStruct((b, l, ATTN_DIM), BF16)] + kv_u_shapes
                     + [jax.ShapeDtypeStruct((b, l, 2 * D_MODEL), BF16)])
    else:
        out_specs, out_shape = kv_u_specs, kv_u_shapes
    return pl.pallas_call(
        functools.partial(_inproj_kernel, latent=latent),
        grid=(l // tt,),
        in_specs=[tok(D_MODEL), mod_spec, mod_spec, _const_spec((1, D_MODEL)),
                  _const_spec(w_in_b.shape), _const_spec(sel.shape),
                  _const_spec((1, HEAD_DIM)), _const_spec((1, HEAD_DIM)),
                  rope_spec, rope_spec, rope_spec],
        out_specs=out_specs,
        out_shape=out_shape,
        compiler_params=_params(1),
        name="inproj_latent" if latent else "inproj_ctx",
    )(x, sh1, sc1, n1, w_in_b, sel, qn, kn, *rope)


def _attn_kernel(q_ref, kc_ref, k_ref, vc_ref, v_ref, o_ref):
    keys = jnp.concatenate([kc_ref[0], k_ref[0]], axis=0)
    vals = jnp.concatenate([vc_ref[0], v_ref[0]], axis=0)
    vals = jnp.concatenate([vals, jnp.ones_like(vals)], axis=-1)
    nt = (((1,), (1,)), ((), ()))

    def scores(r):
        q = q_ref[0, :, r * HEAD_DIM:(r + 1) * HEAD_DIM]
        return lax.dot_general(q, keys, nt, preferred_element_type=F32)

    cur = scores(0)
    for r in range(HEADS_PER_KV):
        nxt = scores(r + 1) if r + 1 < HEADS_PER_KV else None
        p = jnp.exp(cur - jnp.max(cur, axis=-1, keepdims=True)).astype(BF16)
        o = _bdot(p, vals)
        o_ref[0, :, r * HEAD_DIM:(r + 1) * HEAD_DIM] = (
            o[:, :HEAD_DIM] / o[:, HEAD_DIM:]).astype(o_ref.dtype)
        cur = nxt


def _attention(q, kc, k, vc, v):
    b, l, _ = q.shape
    lc = kc.shape[1]
    tq = TQ_ATTN
    qspec = pl.BlockSpec((1, tq, HEADS_PER_KV * HEAD_DIM), lambda i, h, j: (i, j, h))
    kv = lambda n: pl.BlockSpec((1, n, HEAD_DIM), lambda i, h, j: (i, 0, h))
    return pl.pallas_call(
        _attn_kernel,
        grid=(b, N_KV_HEADS, l // tq),
        in_specs=[qspec, kv(lc), kv(l), kv(lc), kv(l)],
        out_specs=qspec,
        out_shape=jax.ShapeDtypeStruct((b, l, ATTN_DIM), BF16),
        compiler_params=_params(3),
        name="attention",
    )(q, kc, k, vc, v)


def _s5_discretise(lam_re, lam_im, log_dt, b_re, b_im):
    dt = jnp.exp(log_dt)[..., None]
    mag = jnp.exp(lam_re * dt)
    ang = lam_im * dt
    a_re = mag * jnp.cos(ang)
    a_im = mag * jnp.sin(ang)
    den = lam_re * lam_re + lam_im * lam_im
    nr = a_re - 1.0
    ni = a_im
    f_re = ((nr * lam_re + ni * lam_im) / den)[:, :, None, :]
    f_im = ((ni * lam_re - nr * lam_im) / den)[:, :, None, :]
    bt_re = b_re.transpose(0, 1, 3, 2)
    bt_im = b_im.transpose(0, 1, 3, 2)
    return (a_re[:, :, None, :], a_im[:, :, None, :],
            f_re * bt_re - f_im * bt_im, f_re * bt_im + f_im * bt_re)


def _s5_table_kernel(a_re_ref, a_im_ref, bb_re_ref, bb_im_ref, c_re_ref, c_im_ref,
                     m_ref, e_ref, f_ref, ap_ref, x_scr, e_scr, f_scr, c_scr):
    hp = lax.Precision.HIGHEST
    i32 = jnp.int32
    na, p_, t_ = GROUPS_PER_BLOCK, SSM_GROUP, CHUNK
    nt = (((1,), (1,)), ((), ()))
    for d in range(2):
        for a in range(na):
            a_re, a_im = a_re_ref[d, a], a_im_ref[d, a]
            b_re, b_im = bb_re_ref[d, a], bb_im_ref[d, a]
            c_re, c_im = c_re_ref[d, a], c_im_ref[d, a]
            c_scr[d, 0, a * p_:(a + 1) * p_, :] = c_re
            c_scr[d, 1, a * p_:(a + 1) * p_, :] = c_im
            pw_re, pw_im = jnp.ones_like(a_re), jnp.zeros_like(a_im)
            for tau in range(t_ + 1):
                if tau < t_:
                    pb_re = b_re * pw_re - b_im * pw_im
                    pb_im = b_re * pw_im + b_im * pw_re
                    r0 = (tau * na + a) * p_
                    x_scr[d, 0, r0:r0 + p_, :] = pb_re
                    x_scr[d, 1, r0:r0 + p_, :] = pb_im
                    s_in = t_ - 1 - tau if d == 0 else tau
                    r0 = (s_in * na + a) * p_
                    e_scr[d, 0, r0:r0 + p_, :] = pb_re
                    e_scr[d, 1, r0:r0 + p_, :] = pb_im
                if tau > 0:
                    t_out = tau - 1 if d == 0 else t_ - tau
                    r0 = (t_out * na + a) * p_
                    f_scr[d, 0, r0:r0 + p_, :] = c_re * pw_re - c_im * pw_im
                    f_scr[d, 1, r0:r0 + p_, :] = -(c_re * pw_im + c_im * pw_re)
                if tau == t_:
                    ap_ref[d, 0, 0, a] = pw_re
                    ap_ref[d, 0, 1, a] = pw_im
                else:
                    pw_re, pw_im = pw_re * a_re - pw_im * a_im, pw_re * a_im + pw_im * a_re

    grp_bits = p_.bit_length() - 1
    st_bits = SSM_STATE.bit_length() - 1
    blk_bits = na.bit_length() - 1
    gmask = na - 1
    r_w = lax.broadcasted_iota(i32, (CHUNK_W, 1), 0)
    c_w = lax.broadcasted_iota(i32, (1, CHUNK_W), 1)
    c_l = lax.broadcasted_iota(i32, (1, LANES), 1)
    r_n = lax.broadcasted_iota(i32, (SSM_STATE, 1), 0)
    c_n = lax.broadcasted_iota(i32, (1, SSM_STATE), 1)
    grp_tok_r = (r_w >> grp_bits) & gmask
    grp_tok_c = (c_w >> grp_bits) & gmask
    grp_st_r = (r_w >> st_bits) & gmask
    grp_st_c = (c_w >> st_bits) & gmask

    lag = []
    for d in range(2):
        full = (lax.dot_general(x_scr[d, 0], c_scr[d, 0], nt, precision=hp, preferred_element_type=F32)
                - lax.dot_general(x_scr[d, 1], c_scr[d, 1], nt, precision=hp, preferred_element_type=F32))
        lag.append(jnp.where(grp_tok_r == (c_l >> grp_bits), full, 0.0))
    for s in range(t_):
        for t in range(t_):
            if t > s:
                blk = lag[0][(t - s) * LANES:(t - s + 1) * LANES]
            elif s > t:
                blk = lag[1][(s - t) * LANES:(s - t + 1) * LANES]
            else:
                blk = lag[0][:LANES] + lag[1][:LANES]
            m_ref[0, s * LANES:(s + 1) * LANES, t * LANES:(t + 1) * LANES] = blk.astype(m_ref.dtype)

    for d in range(2):
        e_full = jnp.zeros((CHUNK_W, 2 * STATE_W), F32)
        f_full = jnp.zeros((2 * STATE_W, CHUNK_W), F32)
        for c in range(2):
            rep = (((c_w >> (st_bits + blk_bits)) == c) & ((c_w & (SSM_STATE - 1)) == r_n))
            rep = rep.astype(F32).astype(BF16)
            rep_t = (((r_w >> (st_bits + blk_bits)) == c) & ((r_w & (SSM_STATE - 1)) == c_n))
            rep_t = rep_t.astype(F32).astype(BF16)
            e_full = e_full + _bdot(e_scr[d, c].astype(BF16), rep)
            f_full = f_full + lax.dot_general(rep_t, f_scr[d, c].astype(BF16), nt,
                                              preferred_element_type=F32)
        e_ref[d, 0] = jnp.where(grp_tok_r == grp_st_c, e_full, 0.0).astype(e_ref.dtype)
        f_ref[d, 0] = jnp.where(grp_st_r == grp_tok_c, f_full, 0.0).astype(f_ref.dtype)


def _s5_tables(a_re, a_im, bb_re, bb_im, c_re, c_im):
    nj, na = N_LANE_BLOCKS, GROUPS_PER_BLOCK
    grp = lambda r: pl.BlockSpec((2, na, r, SSM_STATE), lambda j: (0, j, 0, 0))
    dj = lambda r, c: pl.BlockSpec((2, 1, r, c), lambda j: (0, j, 0, 0))
    src = lambda: pltpu.VMEM((2, 2, CHUNK_W, SSM_STATE), F32)
    return pl.pallas_call(
        _s5_table_kernel,
        grid=(nj,),
        in_specs=[grp(1), grp(1), grp(SSM_GROUP), grp(SSM_GROUP), grp(SSM_GROUP), grp(SSM_GROUP)],
        out_specs=[pl.BlockSpec((1, CHUNK_W, CHUNK_W), lambda j: (j, 0, 0)),
                   dj(CHUNK_W, 2 * STATE_W), dj(2 * STATE_W, CHUNK_W),
                   pl.BlockSpec((2, 1, 2, na, 1, SSM_STATE), lambda j: (0, j, 0, 0, 0, 0))],
        out_shape=[jax.ShapeDtypeStruct((nj, CHUNK_W, CHUNK_W), BF16),
                   jax.ShapeDtypeStruct((2, nj, CHUNK_W, 2 * STATE_W), BF16),
                   jax.ShapeDtypeStruct((2, nj, 2 * STATE_W, CHUNK_W), BF16),
                   jax.ShapeDtypeStruct((2, nj, 2, na, 1, SSM_STATE), F32)],
        scratch_shapes=[src(), src(), src(), pltpu.VMEM((2, 2, LANES, SSM_STATE), F32)],
        compiler_params=_params(1),
        name="s5_tables",
    )(a_re, a_im, bb_re, bb_im, c_re, c_im)


def _s5_state_kernel(uc_ref, ul_ref, e_ref, a_ref, sin_ref, sre_scr, sim_scr, *, nb):
    d = pl.program_id(1)
    r = pl.program_id(2)
    a_re = jnp.broadcast_to(a_ref[0, 0, 0], (nb, STATE_W))
    a_im = jnp.broadcast_to(a_ref[0, 0, 1], (nb, STATE_W))

    def run(u_ref, reverse, store):
        rows = u_ref.shape[1]
        part = min(rows, S5_STATE_PART)
        parts = list(range(rows // part))
        chunks = list(range(part // nb))
        if reverse:
            parts, chunks = parts[::-1], chunks[::-1]
        inj = [_bdot(u_ref[0, p * part:(p + 1) * part], e_ref[0, 0]) for p in parts]
        s_re, s_im = sre_scr[...], sim_scr[...]
        for c, p in zip(inj, parts):
            for k in chunks:
                if store:
                    row = p * part + k * nb
                    sin_ref[0, 0, row:row + nb, 0:STATE_W] = s_re.astype(sin_ref.dtype)
                    sin_ref[0, 0, row:row + nb, STATE_W:] = s_im.astype(sin_ref.dtype)
                c_re = c[k * nb:(k + 1) * nb, 0:STATE_W]
                c_im = c[k * nb:(k + 1) * nb, STATE_W:]
                s_re, s_im = s_re * a_re - s_im * a_im + c_re, s_re * a_im + s_im * a_re + c_im
        sre_scr[...] = s_re
        sim_scr[...] = s_im

    @pl.when(r == 0)
    def _():
        sre_scr[...] = jnp.zeros_like(sre_scr)
        sim_scr[...] = jnp.zeros_like(sim_scr)

    for reverse in (False, True):
        @pl.when((r == 0) & (d == int(reverse)))
        def _():
            run(uc_ref, reverse, False)

        @pl.when((r > 0) & (d == int(reverse)))
        def _():
            run(ul_ref, reverse, True)


def _s5_states(u4c, u4l, e_tab, a_tab, nb):
    rows_c = u4c.shape[1]
    rows = S5_STATE_ROWS
    n_tiles = u4l.shape[1] // rows

    def lat_tile(d, r):
        t = jnp.maximum(r - 1, 0)
        return jnp.where(d == 0, t, n_tiles - 1 - t)

    return pl.pallas_call(
        functools.partial(_s5_state_kernel, nb=nb),
        grid=(N_LANE_BLOCKS, 2, n_tiles + 1),
        in_specs=[pl.BlockSpec((1, rows_c, CHUNK_W), lambda j, d, r: (j, 0, 0)),
                  pl.BlockSpec((1, rows, CHUNK_W), lambda j, d, r: (j, lat_tile(d, r), 0)),
                  pl.BlockSpec((1, 1, CHUNK_W, 2 * STATE_W), lambda j, d, r: (d, j, 0, 0)),
                  pl.BlockSpec((1, 1, 2, 1, STATE_W), lambda j, d, r: (d, j, 0, 0, 0))],
        out_specs=pl.BlockSpec((1, 1, rows, 2 * STATE_W), lambda j, d, r: (d, j, lat_tile(d, r), 0)),
        out_shape=jax.ShapeDtypeStruct((2, N_LANE_BLOCKS, u4l.shape[1], 2 * STATE_W), BF16),
        scratch_shapes=[pltpu.VMEM((nb, STATE_W), F32), pltpu.VMEM((nb, STATE_W), F32)],
        compiler_params=_params(3),
        name="s5_states",
    )(u4c, u4l, e_tab, a_tab)


def _s5_out_kernel(ul_ref, sin_ref, m_ref, f_ref, d_ref, y_ref):
    u = ul_ref[0]
    y_ref[0] = (u.astype(F32) * d_ref[0] + _bdot(u, m_ref[0])
                + _bdot(sin_ref[0, 0], f_ref[0, 0]) + _bdot(sin_ref[1, 0], f_ref[1, 0]))


def _s5_outputs(u4l, sin, m_tab, f_tab, d_tab):
    rows = S5_OUT_ROWS
    return pl.pallas_call(
        _s5_out_kernel,
        grid=(N_LANE_BLOCKS, u4l.shape[1] // rows),
        in_specs=[pl.BlockSpec((1, rows, CHUNK_W), lambda j, r: (j, r, 0)),
                  pl.BlockSpec((2, 1, rows, 2 * STATE_W), lambda j, r: (0, j, r, 0)),
                  pl.BlockSpec((1, CHUNK_W, CHUNK_W), lambda j, r: (j, 0, 0)),
                  pl.BlockSpec((2, 1, 2 * STATE_W, CHUNK_W), lambda j, r: (0, j, 0, 0)),
                  pl.BlockSpec((1, 1, CHUNK_W), lambda j, r: (j, 0, 0))],
        out_specs=pl.BlockSpec((1, rows, CHUNK_W), lambda j, r: (j, r, 0)),
        out_shape=jax.ShapeDtypeStruct(u4l.shape, F32),
        compiler_params=_params(2),
        name="s5_outputs",
    )(u4l, sin, m_tab, f_tab, d_tab)


def _merge_kernel(attn_ref, y4_ref, g_ref, x_ref, g1_ref, sh2_ref, sc2_ref, n2_ref,
                  sel_ref, wab_ref, wglu_ref, wout_ref, x1_ref, h2_ref):
    nb, tokens_all, _ = x_ref.shape
    tokens = tokens_all // MERGE_SPLIT
    rows = nb * tokens
    crow = tokens // CHUNK * nb

    def branch_dots(i):
        ts = slice(i * tokens, (i + 1) * tokens)
        act = [_gelu_tanh(y4_ref[j, i * crow:(i + 1) * crow]).astype(BF16)
               for j in range(N_LANE_BLOCKS)]
        act = jnp.concatenate(
            [jnp.concatenate([act[j][:, s * LANES:(s + 1) * LANES] for s in range(CHUNK)], axis=0)
             for j in range(N_LANE_BLOCKS)], axis=-1)
        act = _bdot(sel_ref[...], act).astype(BF16)
        p_attn = _bdot(attn_ref[:, ts].reshape(rows, ATTN_DIM), wab_ref[...])
        return p_attn, _bdot(act, wglu_ref[...])

    def mix_dot(i, p_attn, glu):
        ts = slice(i * tokens, (i + 1) * tokens)
        p_ssm = glu[:, :D_MODEL] * _sigmoid(glu[:, D_MODEL:])
        g = g_ref[:, ts].reshape(rows, 2 * D_MODEL).astype(F32)
        mix = _sigmoid(g[:, :D_MODEL]) * p_attn + _sigmoid(g[:, D_MODEL:]) * p_ssm
        return _bdot(mix.astype(BF16), wout_ref[...]).reshape(nb, tokens, D_MODEL)

    def finish(i, x_mix):
        ts = slice(i * tokens, (i + 1) * tokens)
        x1 = x_ref[:, ts] + g1_ref[...] * x_mix
        x1_ref[:, ts] = x1
        h2 = _rms(x1) * n2_ref[...]
        h2_ref[:, ts] = (h2 * (1.0 + sc2_ref[...]) + sh2_ref[...]).astype(h2_ref.dtype)

    dots = [None] * MERGE_SPLIT
    mixed = [None] * MERGE_SPLIT
    dots[0] = branch_dots(0)
    for i in range(MERGE_SPLIT):
        if i + 1 < MERGE_SPLIT:
            dots[i + 1] = branch_dots(i + 1)
        mixed[i] = mix_dot(i, *dots[i])
        if i > 0:
            finish(i - 1, mixed[i - 1])
    finish(MERGE_SPLIT - 1, mixed[MERGE_SPLIT - 1])


def _merge(attn, y4, g, x, g1, sh2, sc2, n2, wab, wglu, wout):
    b, l, _ = x.shape
    tt = TOK_TILE
    tok = lambda w: pl.BlockSpec((b, tt, w), lambda i: (0, i, 0))
    per_b = _const_spec((b, 1, D_MODEL))
    y4_spec = pl.BlockSpec((N_LANE_BLOCKS, tt // CHUNK * b, CHUNK_W), lambda i: (0, i, 0))
    sel = _chunk_row_select(b, tt // MERGE_SPLIT, inverse=True)
    return pl.pallas_call(
        _merge_kernel,
        grid=(l // tt,),
        in_specs=[tok(ATTN_DIM), y4_spec, tok(2 * D_MODEL), tok(D_MODEL),
                  per_b, per_b, per_b, _const_spec((1, D_MODEL)), _const_spec(sel.shape),
                  _const_spec(wab.shape), _const_spec(wglu.shape), _const_spec(wout.shape)],
        out_specs=[tok(D_MODEL), tok(D_MODEL)],
        out_shape=[jax.ShapeDtypeStruct((b, l, D_MODEL), F32),
                   jax.ShapeDtypeStruct((b, l, D_MODEL), BF16)],
        compiler_params=_params(1),
        name="merge",
    )(attn, y4, g, x, g1, sh2, sc2, n2, sel, wab, wglu, wout)


def _ffn_row_select(tm):
    seg_len = tm // SUB
    tile = np.zeros((tm, tm), np.float32)
    for seg in range(SUB):
        for i in range(seg_len):
            tile[SUB * i + seg, seg * seg_len + i] = 1.0
    halo = np.zeros((SUB, 2 * HALO), np.float32)
    halo[0, HALO - 1] = 1.0
    halo[1, HALO] = 1.0
    return jnp.asarray(tile, dtype=BF16), jnp.asarray(halo, dtype=BF16)


def _ffn_kernel(sel_ref, selh_ref, hp_ref, h_ref, hn_ref, x1_ref, g2_ref, wup_ref, cw_ref, cb_ref,
                wd_ref, fw_ref, o_ref, acc_scr, *, tm):
    j = pl.program_id(1)
    keep_prev = (j > 0).astype(F32)
    keep_next = (j < pl.num_programs(1) - 1).astype(F32)
    halo = jnp.concatenate([hp_ref[0], hn_ref[0]], axis=0)
    lhs = jnp.concatenate([_bdot(selh_ref[...], halo), _bdot(sel_ref[...], h_ref[0])],
                          axis=0).astype(BF16)
    sub = lax.broadcasted_iota(jnp.int32, (SUB, 1), 0)

    def up(f):
        return (_bdot(lhs, wup_ref[:, f * FF_CHUNK:(f + 1) * FF_CHUNK]),
                _bdot(lhs, wup_ref[:, D_FF + f * FF_CHUNK:D_FF + (f + 1) * FF_CHUNK]))

    def conv(zall, col):
        z = zall[SUB:]
        before = jnp.where(sub == 0, zall[0:1] * keep_prev, pltpu.roll(z[tm - SUB:], 1, 0))
        after = jnp.where(sub == SUB - 1, zall[1:2] * keep_next, pltpu.roll(z[:SUB], SUB - 1, 0))
        z_prev = jnp.concatenate([before, z[:tm - SUB]], axis=0)
        z_next = jnp.concatenate([z[SUB:], after], axis=0)
        cw = cw_ref[:, col:col + FF_CHUNK]
        return z_prev * cw[0:1] + z * cw[1:2] + z_next * cw[2:3] + cb_ref[:, col:col + FF_CHUNK]

    acts = []
    cur = up(0)
    for f in range(N_FF_CHUNKS):
        nxt = up(f + 1) if f + 1 < N_FF_CHUNKS else None
        val = conv(cur[0], f * FF_CHUNK)
        gate = conv(cur[1], D_FF + f * FF_CHUNK)
        acts.append((gate * _sigmoid(gate) * val).astype(BF16))
        cur = nxt
    acc = _bdot(jnp.concatenate(acts, axis=-1), wd_ref[...])
    seg_len = tm // SUB
    for k in range(D_MODEL // LANES):
        acc_scr[k] = acc[:, k * LANES:(k + 1) * LANES]
        for seg in range(SUB):
            o_ref[0, seg * seg_len:(seg + 1) * seg_len, k * LANES:(k + 1) * LANES] = (
                acc_scr[k, pl.ds(seg, seg_len, stride=SUB), :])
    x2 = x1_ref[0] + g2_ref[0] * o_ref[0]
    o_ref[0] = _rms(x2) * fw_ref[...]


def _conv_ffn(h2, x1, g2, wup, cw, cb, wd, fw):
    b, l, _ = x1.shape
    tm = TM_FFN
    nh = tm // HALO
    last = l // HALO - 1
    tok = lambda: pl.BlockSpec((1, tm, D_MODEL), lambda i, j: (i, j, 0))
    prev = pl.BlockSpec((1, HALO, D_MODEL), lambda i, j: (i, jnp.maximum(j * nh - 1, 0), 0))
    nxt = pl.BlockSpec((1, HALO, D_MODEL), lambda i, j: (i, jnp.minimum((j + 1) * nh, last), 0))
    per_b = pl.BlockSpec((1, 1, D_MODEL), lambda i, j: (i, 0, 0))
    once = lambda shape: pl.BlockSpec(shape, lambda *_: (0,) * len(shape),
                                      pipeline_mode=pl.Buffered(1))
    sel, selh = _ffn_row_select(tm)
    return pl.pallas_call(
        functools.partial(_ffn_kernel, tm=tm),
        grid=(b, l // tm),
        in_specs=[once(sel.shape), once(selh.shape), prev, tok(), nxt, tok(), per_b,
                  once(wup.shape), once(cw.shape), once(cb.shape), once(wd.shape),
                  _const_spec((1, D_MODEL))],
        out_specs=tok(),
        out_shape=jax.ShapeDtypeStruct((b, l, D_MODEL), F32),
        scratch_shapes=[pltpu.VMEM((D_MODEL // LANES, tm, LANES), F32)],
        compiler_params=_params(2),
        name="conv_ffn",
    )(sel, selh, h2, h2, h2, x1, g2, wup, cw, cb, wd, fw)


def _rope_tables(l):
    rows = jnp.repeat(jnp.arange(l // GRID_W, dtype=F32), GRID_W)
    cols = jnp.tile(jnp.arange(GRID_W, dtype=F32), l // GRID_W)
    inv_freq = ROPE_THETA ** (-jnp.arange(0, ROPE_AXIS_DIM, 2, dtype=F32) / ROPE_AXIS_DIM)
    ang = jnp.concatenate([rows[:, None] * inv_freq, cols[:, None] * inv_freq], axis=-1)
    cos = jnp.repeat(jnp.cos(ang), 2, axis=-1)
    sin = jnp.repeat(jnp.sin(ang), 2, axis=-1)
    odd = (jnp.arange(HEAD_DIM) % 2 == 1)[None, :]
    return cos, jnp.where(odd, sin, 0.0), jnp.where(odd, 0.0, -sin)


def kernel(x, c, ctx, c_ctx, w_mod, b_mod, norm1_w, norm2_w, w_in, q_norm_w, k_norm_w, w_attn_br,
           ssm_lambda_re, ssm_lambda_im, ssm_log_dt, ssm_b_re, ssm_b_im, ssm_c_re, ssm_c_im, ssm_d,
           w_glu, w_out, w_up, conv_w, conv_b, w_down, final_norm_w):
    b, l, d = x.shape
    lc = ctx.shape[1]
    assert w_mod.shape[0] == 1 and d == D_MODEL and l % TOK_TILE == 0 and lc % TOK_TILE == 0
    assert (l // CHUNK * b) % S5_STATE_ROWS == 0 and (l // CHUNK * b) % S5_OUT_ROWS == 0
    layer = 0

    pad = (-(b + 1)) % 8
    c_rows = jnp.concatenate([c, c_ctx[None, :], jnp.zeros((pad, d), F32)], axis=0)
    mod = _modulation(c_rows, w_mod[layer], b_mod[layer])
    sh1, sc1, g1, sh2, sc2, g2 = [m[:b, None, :] for m in jnp.split(mod, N_MOD, axis=-1)]
    csh1, csc1 = mod[b:b + 1, None, :d], mod[b:b + 1, None, d:2 * d]

    w_in_b = w_in[layer].astype(BF16)
    qn = (q_norm_w[layer] * (1.0 / math.sqrt(HEAD_DIM))).reshape(1, HEAD_DIM)
    kn = k_norm_w[layer].reshape(1, HEAD_DIM)
    n1 = norm1_w[layer].reshape(1, d)

    no_rope = (jnp.ones((lc, HEAD_DIM), F32),) + (jnp.zeros((lc, HEAD_DIM), F32),) * 2
    q, k, v, u4l, g = _inproj(x, sh1, sc1, n1, w_in_b, qn, kn, _rope_tables(l), latent=True)
    kc, vc, u4c = _inproj(ctx, csh1, csc1, n1, w_in_b, qn, kn, no_rope, latent=False)

    attn = _attention(q, kc, k, vc, v)

    disc = _s5_discretise(ssm_lambda_re[layer], ssm_lambda_im[layer], ssm_log_dt[layer],
                          ssm_b_re[layer], ssm_b_im[layer])
    m_tab, e_tab, f_tab, a_pow = _s5_tables(*disc, ssm_c_re[layer], ssm_c_im[layer])
    a_tab = a_pow.reshape(2, N_LANE_BLOCKS, 2, 1, STATE_W)
    d_tab = jnp.tile(ssm_d[layer].reshape(N_LANE_BLOCKS, 1, LANES), (1, CHUNK, 1))
    d_tab = d_tab.reshape(N_LANE_BLOCKS, 1, CHUNK_W)
    sin = _s5_states(u4c, u4l, e_tab, a_tab, b)
    y4 = _s5_outputs(u4l, sin, m_tab, f_tab, d_tab)

    x1, h2 = _merge(attn, y4, g, x, g1, sh2, sc2, norm2_w[layer].reshape(1, d),
                    w_attn_br[layer].astype(BF16), w_glu[layer].astype(BF16),
                    w_out[layer].astype(BF16))

    return _conv_ffn(h2, x1, g2, w_up[layer].astype(BF16), conv_w[layer],
                     conv_b[layer].reshape(1, 2 * D_FF), w_down[layer].astype(BF16),
                     final_norm_w.reshape(1, d))
```

```python
import functools
import math

import jax
import jax.numpy as jnp
import numpy as np
from jax import lax
from jax.experimental import pallas as pl
from jax.experimental.pallas import tpu as pltpu

F32 = jnp.float32
BF16 = jnp.bfloat16

D_MODEL = 1024
GRID_W = 64
N_HEADS = 8
N_KV_HEADS = 2
HEAD_DIM = 128
HEADS_PER_KV = N_HEADS // N_KV_HEADS
ATTN_DIM = N_HEADS * HEAD_DIM
KV_DIM = N_KV_HEADS * HEAD_DIM
ROPE_THETA = 10000.0
ROPE_AXIS_DIM = HEAD_DIM // 2
SSM_DIM = 512
SSM_GROUP = 16
N_SSM_GROUPS = SSM_DIM // SSM_GROUP
SSM_STATE = 64
Q_END = ATTN_DIM
K_END = Q_END + KV_DIM
V_END = K_END + KV_DIM
U_END = V_END + SSM_DIM
D_IN = U_END + 2 * D_MODEL
D_FF = 2816
N_MOD = 6
EPS = 1e-6

LANES = 128
SUB = 8
CHUNK = 8
N_LANE_BLOCKS = SSM_DIM // LANES
GROUPS_PER_BLOCK = LANES // SSM_GROUP
CHUNK_W = CHUNK * LANES
STATE_W = GROUPS_PER_BLOCK * SSM_STATE

FF_CHUNK = 256
N_FF_CHUNKS = D_FF // FF_CHUNK
HALO = 16

VMEM_LIMIT = 56 * 1024 * 1024

TOK_TILE = 64
TQ_ATTN = 1024
PROJ_SPLIT = 4
MERGE_SPLIT = 4
TM_FFN = 512
S5_STATE_ROWS = 1024
S5_STATE_PART = 512
S5_OUT_ROWS = 1024


def _sigmoid(x):
    return 1.0 / (1.0 + jnp.exp(-x))


def _gelu_tanh(x):
    return 0.5 * x * (1.0 + jnp.tanh(math.sqrt(2.0 / math.pi) * (x + 0.044715 * (x * x * x))))


def _rms(x):
    return x * lax.rsqrt(jnp.mean(x * x, axis=-1, keepdims=True) + EPS)


def _bdot(a, b):
    return jnp.dot(a, b, preferred_element_type=F32)


def _const_spec(shape):
    nd = len(shape)
    return pl.BlockSpec(shape, lambda *_: (0,) * nd)


def _params(n_axes):
    return pltpu.CompilerParams(dimension_semantics=("arbitrary",) * n_axes,
                                vmem_limit_bytes=VMEM_LIMIT)


def _split_bf16(v):
    hi = v.astype(BF16)
    return hi, (v - hi.astype(F32)).astype(BF16)


def _mod_kernel(c_ref, w_ref, b_ref, o_ref):
    c = c_ref[...]
    rows = c.shape[0]
    a_hi, a_lo = _split_bf16(c * _sigmoid(c))
    w_hi, w_lo = _split_bf16(w_ref[...])
    both = _bdot(jnp.concatenate([a_hi, a_lo], axis=0), w_hi)
    o_ref[...] = both[:rows] + both[rows:] + _bdot(a_hi, w_lo) + b_ref[...]


def _modulation(c_rows, w_mod, b_mod):
    rows = c_rows.shape[0]
    n = w_mod.shape[1]
    tn = 1536
    return pl.pallas_call(
        _mod_kernel,
        grid=(n // tn,),
        in_specs=[pl.BlockSpec((rows, D_MODEL), lambda j: (0, 0)),
                  pl.BlockSpec((D_MODEL, tn), lambda j: (0, j)),
                  pl.BlockSpec((1, tn), lambda j: (0, j))],
        out_specs=pl.BlockSpec((rows, tn), lambda j: (0, j)),
        out_shape=jax.ShapeDtypeStruct((rows, n), F32),
        compiler_params=_params(1),
        name="modulation",
    )(c_rows, w_mod, b_mod.reshape(1, n))


def _norm_rope_store(t, nw, rope, out_ref, ts, col, nb):
    t = _rms(t) * nw
    if rope is not None:
        cos, sin_prev, sin_next = rope
        prev = pltpu.roll(t, 1, 1).reshape(nb, -1, HEAD_DIM)
        nxt = pltpu.roll(t, HEAD_DIM - 1, 1).reshape(nb, -1, HEAD_DIM)
        t3 = t.reshape(nb, -1, HEAD_DIM) * cos + prev * sin_prev + nxt * sin_next
    else:
        t3 = t.reshape(nb, -1, HEAD_DIM)
    out_ref[:, ts, col:col + HEAD_DIM] = t3.astype(out_ref.dtype)


def _chunk_row_select(nb, tokens, inverse=False):
    sel = np.zeros((nb * tokens, nb * tokens), np.float32)
    for b in range(nb):
        for t in range(tokens):
            k, s = divmod(t, CHUNK)
            sel[(s * (tokens // CHUNK) + k) * nb + b, b * tokens + t] = 1.0
    return jnp.asarray(sel.T if inverse else sel, dtype=BF16)


def _store_chunk_rows(u_perm, u4_ref, row0):
    rows = u_perm.shape[0] // CHUNK
    for j in range(N_LANE_BLOCKS):
        for s in range(CHUNK):
            u4_ref[j, row0:row0 + rows, s * LANES:(s + 1) * LANES] = (
                u_perm[s * rows:(s + 1) * rows, j * LANES:(j + 1) * LANES].astype(u4_ref.dtype))


def _inproj_kernel(x_ref, sh_ref, sc_ref, n1_ref, w_ref, sel_ref, qn_ref, kn_ref,
                   cos_ref, sinp_ref, sinn_ref, *refs, latent):
    nb, tokens_all, _ = x_ref.shape
    tokens = tokens_all // PROJ_SPLIT
    rows = nb * tokens
    if latent:
        q_ref, k_ref, v_ref, u4_ref, g_ref = refs
    else:
        k_ref, v_ref, u4_ref = refs

    def lhs(i):
        ts = slice(i * tokens, (i + 1) * tokens)
        h = _rms(x_ref[:, ts]) * n1_ref[...]
        h = h * (1.0 + sc_ref[...]) + sh_ref[...]
        return h.reshape(rows, D_MODEL).astype(BF16)

    def project(i, hb):
        ts = slice(i * tokens, (i + 1) * tokens)
        rope = (cos_ref[ts], sinp_ref[ts], sinn_ref[ts]) if latent else None
        if latent:
            q = _bdot(hb, w_ref[:, :Q_END])
            for hd in range(N_HEADS):
                _norm_rope_store(q[:, hd * HEAD_DIM:(hd + 1) * HEAD_DIM], qn_ref[...], rope,
                                 q_ref, ts, hd * HEAD_DIM, nb)
        k = _bdot(hb, w_ref[:, Q_END:K_END])
        for hd in range(N_KV_HEADS):
            _norm_rope_store(k[:, hd * HEAD_DIM:(hd + 1) * HEAD_DIM], kn_ref[...], rope,
                             k_ref, ts, hd * HEAD_DIM, nb)
        v_ref[:, ts] = _bdot(hb, w_ref[:, K_END:V_END]).reshape(nb, tokens, KV_DIM).astype(v_ref.dtype)
        hb_perm = _bdot(sel_ref[...], hb).astype(BF16)
        _store_chunk_rows(_bdot(hb_perm, w_ref[:, V_END:U_END]), u4_ref, i * (rows // CHUNK))
        if latent:
            g_ref[:, ts] = _bdot(hb, w_ref[:, U_END:]).reshape(
                nb, tokens, 2 * D_MODEL).astype(g_ref.dtype)

    cur = lhs(0)
    for i in range(PROJ_SPLIT):
        nxt = lhs(i + 1) if i + 1 < PROJ_SPLIT else None
        project(i, cur)
        cur = nxt


def _inproj(x, sh1, sc1, n1, w_in_b, qn, kn, rope, latent):
    b, l, _ = x.shape
    tt = TOK_TILE
    tok = lambda w: pl.BlockSpec((b, tt, w), lambda i: (0, i, 0))
    mod_spec = _const_spec(sh1.shape)
    rope_spec = pl.BlockSpec((tt, HEAD_DIM), lambda i: (i, 0))
    u4_spec = pl.BlockSpec((N_LANE_BLOCKS, tt // CHUNK * b, CHUNK_W), lambda i: (0, i, 0))
    kv_u_specs = [tok(KV_DIM), tok(KV_DIM), u4_spec]
    sel = _chunk_row_select(b, tt // PROJ_SPLIT)
    kv_u_shapes = [jax.ShapeDtypeStruct((b, l, KV_DIM), BF16),
                   jax.ShapeDtypeStruct((b, l, KV_DIM), BF16),
                   jax.ShapeDtypeStruct((N_LANE_BLOCKS, l // CHUNK * b, CHUNK_W), BF16)]
    if latent:
        out_specs = [tok(ATTN_DIM)] + kv_u_specs + [tok(2 * D_MODEL)]
        out_shape = ([jax.ShapeDtypeStruct((b, l, ATTN_DIM), BF16)] + kv_u_shapes
                     + [jax.ShapeDtypeStruct((b, l, 2 * D_MODEL), BF16)])
    else:
        out_specs, out_shape = kv_u_specs, kv_u_shapes
    return pl.pallas_call(
        functools.partial(_inproj_kernel, latent=latent),
        grid=(l // tt,),
        in_specs=[tok(D_MODEL), mod_spec, mod_spec, _const_spec((1, D_MODEL)),
                  _const_spec(w_in_b.shape), _const_spec(sel.shape),
                  _const_spec((1, HEAD_DIM)), _const_spec((1, HEAD_DIM)),
                  rope_spec, rope_spec, rope_spec],
        out_specs=out_specs,
        out_shape=out_shape,
        compiler_params=_params(1),
        name="inproj_latent" if latent else "inproj_ctx",
    )(x, sh1, sc1, n1, w_in_b, sel, qn, kn, *rope)


def _attn_kernel(q_ref, kc_ref, k_ref, vc_ref, v_ref, o_ref):
    keys = jnp.concatenate([kc_ref[0], k_ref[0]], axis=0)
    vals = jnp.concatenate([vc_ref[0], v_ref[0]], axis=0)
    vals = jnp.concatenate([vals, jnp.ones_like(vals)], axis=-1)
    nt = (((1,), (1,)), ((), ()))

    def scores(r):
        q = q_ref[0, :, r * HEAD_DIM:(r + 1) * HEAD_DIM]
        return lax.dot_general(q, keys, nt, preferred_element_type=F32)

    cur = scores(0)
    for r in range(HEADS_PER_KV):
        nxt = scores(r + 1) if r + 1 < HEADS_PER_KV else None
        p = jnp.exp(cur - jnp.max(cur, axis=-1, keepdims=True)).astype(BF16)
        o = _bdot(p, vals)
        o_ref[0, :, r * HEAD_DIM:(r + 1) * HEAD_DIM] = (
            o[:, :HEAD_DIM] / o[:, HEAD_DIM:]).astype(o_ref.dtype)
        cur = nxt


def _attention(q, kc, k, vc, v):
    b, l, _ = q.shape
    lc = kc.shape[1]
    tq = TQ_ATTN
    qspec = pl.BlockSpec((1, tq, HEADS_PER_KV * HEAD_DIM), lambda i, h, j: (i, j, h))
    kv = lambda n: pl.BlockSpec((1, n, HEAD_DIM), lambda i, h, j: (i, 0, h))
    return pl.pallas_call(
        _attn_kernel,
        grid=(b, N_KV_HEADS, l // tq),
        in_specs=[qspec, kv(lc), kv(l), kv(lc), kv(l)],
        out_specs=qspec,
        out_shape=jax.ShapeDtypeStruct((b, l, ATTN_DIM), BF16),
        compiler_params=_params(3),
        name="attention",
    )(q, kc, k, vc, v)


def _s5_discretise(lam_re, lam_im, log_dt, b_re, b_im):
    dt = jnp.exp(log_dt)[..., None]
    mag = jnp.exp(lam_re * dt)
    ang = lam_im * dt
    a_re = mag * jnp.cos(ang)
    a_im = mag * jnp.sin(ang)
    den = lam_re * lam_re + lam_im * lam_im
    nr = a_re - 1.0
    ni = a_im
    f_re = ((nr * lam_re + ni * lam_im) / den)[:, :, None, :]
    f_im = ((ni * lam_re - nr * lam_im) / den)[:, :, None, :]
    bt_re = b_re.transpose(0, 1, 3, 2)
    bt_im = b_im.transpose(0, 1, 3, 2)
    return (a_re[:, :, None, :], a_im[:, :, None, :],
            f_re * bt_re - f_im * bt_im, f_re * bt_im + f_im * bt_re)


def _s5_table_kernel(a_re_ref, a_im_ref, bb_re_ref, bb_im_ref, c_re_ref, c_im_ref,
                     m_ref, e_ref, f_ref, ap_ref, x_scr, e_scr, f_scr, c_scr):
    hp = lax.Precision.HIGHEST
    i32 = jnp.int32
    na, p_, t_ = GROUPS_PER_BLOCK, SSM_GROUP, CHUNK
    nt = (((1,), (1,)), ((), ()))
    rows = na * p_
    for d in range(2):
        a_re = jnp.broadcast_to(a_re_ref[d], (na, p_, SSM_STATE)).reshape(rows, SSM_STATE)
        a_im = jnp.broadcast_to(a_im_ref[d], (na, p_, SSM_STATE)).reshape(rows, SSM_STATE)
        b_re, b_im = bb_re_ref[d].reshape(rows, SSM_STATE), bb_im_ref[d].reshape(rows, SSM_STATE)
        c_re, c_im = c_re_ref[d].reshape(rows, SSM_STATE), c_im_ref[d].reshape(rows, SSM_STATE)
        c_scr[d, 0] = c_re
        c_scr[d, 1] = c_im
        pw_re, pw_im = jnp.ones_like(a_re), jnp.zeros_like(a_im)
        for tau in range(t_ + 1):
            if tau < t_:
                pb_re = b_re * pw_re - b_im * pw_im
                pb_im = b_re * pw_im + b_im * pw_re
                x_scr[d, 0, tau * rows:(tau + 1) * rows, :] = pb_re
                x_scr[d, 1, tau * rows:(tau + 1) * rows, :] = pb_im
                s_in = t_ - 1 - tau if d == 0 else tau
                e_scr[d, 0, s_in * rows:(s_in + 1) * rows, :] = pb_re
                e_scr[d, 1, s_in * rows:(s_in + 1) * rows, :] = pb_im
            if tau > 0:
                t_out = tau - 1 if d == 0 else t_ - tau
                f_scr[d, 0, t_out * rows:(t_out + 1) * rows, :] = c_re * pw_re - c_im * pw_im
                f_scr[d, 1, t_out * rows:(t_out + 1) * rows, :] = -(c_re * pw_im + c_im * pw_re)
            if tau == t_:
                ap_ref[d, 0, 0] = pw_re.reshape(na, p_, SSM_STATE)[:, 0:1, :]
                ap_ref[d, 0, 1] = pw_im.reshape(na, p_, SSM_STATE)[:, 0:1, :]
            else:
                pw_re, pw_im = pw_re * a_re - pw_im * a_im, pw_re * a_im + pw_im * a_re

    grp_bits = p_.bit_length() - 1
    st_bits = SSM_STATE.bit_length() - 1
    blk_bits = na.bit_length() - 1
    gmask = na - 1
    r_w = lax.broadcasted_iota(i32, (CHUNK_W, 1), 0)
    c_w = lax.broadcasted_iota(i32, (1, CHUNK_W), 1)
    c_l = lax.broadcasted_iota(i32, (1, LANES), 1)
    r_n = lax.broadcasted_iota(i32, (SSM_STATE, 1), 0)
    c_n = lax.broadcasted_iota(i32, (1, SSM_STATE), 1)
    grp_tok_r = (r_w >> grp_bits) & gmask
    grp_tok_c = (c_w >> grp_bits) & gmask
    grp_st_r = (r_w >> st_bits) & gmask
    grp_st_c = (c_w >> st_bits) & gmask

    lag = []
    for d in range(2):
        full = (lax.dot_general(x_scr[d, 0], c_scr[d, 0], nt, precision=hp, preferred_element_type=F32)
                - lax.dot_general(x_scr[d, 1], c_scr[d, 1], nt, precision=hp, preferred_element_type=F32))
        lag.append(jnp.where(grp_tok_r == (c_l >> grp_bits), full, 0.0))
    for s in range(t_):
        for t in range(t_):
            if t > s:
                blk = lag[0][(t - s) * LANES:(t - s + 1) * LANES]
            elif s > t:
                blk = lag[1][(s - t) * LANES:(s - t + 1) * LANES]
            else:
                blk = lag[0][:LANES] + lag[1][:LANES]
            m_ref[0, s * LANES:(s + 1) * LANES, t * LANES:(t + 1) * LANES] = blk.astype(m_ref.dtype)

    for d in range(2):
        e_full = jnp.zeros((CHUNK_W, 2 * STATE_W), F32)
        f_full = jnp.zeros((2 * STATE_W, CHUNK_W), F32)
        for c in range(2):
            rep = (((c_w >> (st_bits + blk_bits)) == c) & ((c_w & (SSM_STATE - 1)) == r_n))
            rep = rep.astype(F32).astype(BF16)
            rep_t = (((r_w >> (st_bits + blk_bits)) == c) & ((r_w & (SSM_STATE - 1)) == c_n))
            rep_t = rep_t.astype(F32).astype(BF16)
            e_full = e_full + _bdot(e_scr[d, c].astype(BF16), rep)
            f_full = f_full + lax.dot_general(rep_t, f_scr[d, c].astype(BF16), nt,
                                              preferred_element_type=F32)
        e_ref[d, 0] = jnp.where(grp_tok_r == grp_st_c, e_full, 0.0).astype(e_ref.dtype)
        f_ref[d, 0] = jnp.where(grp_st_r == grp_tok_c, f_full, 0.0).astype(f_ref.dtype)


def _s5_tables(a_re, a_im, bb_re, bb_im, c_re, c_im):
    nj, na = N_LANE_BLOCKS, GROUPS_PER_BLOCK
    grp = lambda r: pl.BlockSpec((2, na, r, SSM_STATE), lambda j: (0, j, 0, 0))
    dj = lambda r, c: pl.BlockSpec((2, 1, r, c), lambda j: (0, j, 0, 0))
    src = lambda: pltpu.VMEM((2, 2, CHUNK_W, SSM_STATE), F32)
    return pl.pallas_call(
        _s5_table_kernel,
        grid=(nj,),
        in_specs=[grp(1), grp(1), grp(SSM_GROUP), grp(SSM_GROUP), grp(SSM_GROUP), grp(SSM_GROUP)],
        out_specs=[pl.BlockSpec((1, CHUNK_W, CHUNK_W), lambda j: (j, 0, 0)),
                   dj(CHUNK_W, 2 * STATE_W), dj(2 * STATE_W, CHUNK_W),
                   pl.BlockSpec((2, 1, 2, na, 1, SSM_STATE), lambda j: (0, j, 0, 0, 0, 0))],
        out_shape=[jax.ShapeDtypeStruct((nj, CHUNK_W, CHUNK_W), BF16),
                   jax.ShapeDtypeStruct((2, nj, CHUNK_W, 2 * STATE_W), BF16),
                   jax.ShapeDtypeStruct((2, nj, 2 * STATE_W, CHUNK_W), BF16),
                   jax.ShapeDtypeStruct((2, nj, 2, na, 1, SSM_STATE), F32)],
        scratch_shapes=[src(), src(), src(), pltpu.VMEM((2, 2, LANES, SSM_STATE), F32)],
        compiler_params=_params(1),
        name="s5_tables",
    )(a_re, a_im, bb_re, bb_im, c_re, c_im)


def _s5_state_kernel(uc_ref, ul_ref, e_ref, a_ref, sin_ref, sre_scr, sim_scr, *, nb):
    d = pl.program_id(1)
    r = pl.program_id(2)
    a_re = jnp.broadcast_to(a_ref[0, 0, 0], (nb, STATE_W))
    a_im = jnp.broadcast_to(a_ref[0, 0, 1], (nb, STATE_W))

    def run(u_ref, reverse, store):
        rows = u_ref.shape[1]
        part = min(rows, S5_STATE_PART)
        parts = list(range(rows // part))
        chunks = list(range(part // nb))
        if reverse:
            parts, chunks = parts[::-1], chunks[::-1]
        inj = [_bdot(u_ref[0, p * part:(p + 1) * part], e_ref[0, 0]) for p in parts]
        s_re, s_im = sre_scr[...], sim_scr[...]
        for c, p in zip(inj, parts):
            for k in chunks:
                if store:
                    row = p * part + k * nb
                    sin_ref[0, 0, row:row + nb, 0:STATE_W] = s_re.astype(sin_ref.dtype)
                    sin_ref[0, 0, row:row + nb, STATE_W:] = s_im.astype(sin_ref.dtype)
                c_re = c[k * nb:(k + 1) * nb, 0:STATE_W]
                c_im = c[k * nb:(k + 1) * nb, STATE_W:]
                s_re, s_im = s_re * a_re - s_im * a_im + c_re, s_re * a_im + s_im * a_re + c_im
        sre_scr[...] = s_re
        sim_scr[...] = s_im

    @pl.when(r == 0)
    def _():
        sre_scr[...] = jnp.zeros_like(sre_scr)
        sim_scr[...] = jnp.zeros_like(sim_scr)

    for reverse in (False, True):
        @pl.when((r == 0) & (d == int(reverse)))
        def _():
            run(uc_ref, reverse, False)

        @pl.when((r > 0) & (d == int(reverse)))
        def _():
            run(ul_ref, reverse, True)


def _s5_states(u4c, u4l, e_tab, a_tab, nb):
    rows_c = u4c.shape[1]
    rows = S5_STATE_ROWS
    n_tiles = u4l.shape[1] // rows

    def lat_tile(d, r):
        t = jnp.maximum(r - 1, 0)
        return jnp.where(d == 0, t, n_tiles - 1 - t)

    return pl.pallas_call(
        functools.partial(_s5_state_kernel, nb=nb),
        grid=(N_LANE_BLOCKS, 2, n_tiles + 1),
        in_specs=[pl.BlockSpec((1, rows_c, CHUNK_W), lambda j, d, r: (j, 0, 0)),
                  pl.BlockSpec((1, rows, CHUNK_W), lambda j, d, r: (j, lat_tile(d, r), 0)),
                  pl.BlockSpec((1, 1, CHUNK_W, 2 * STATE_W), lambda j, d, r: (d, j, 0, 0)),
                  pl.BlockSpec((1, 1, 2, 1, STATE_W), lambda j, d, r: (d, j, 0, 0, 0))],
        out_specs=pl.BlockSpec((1, 1, rows, 2 * STATE_W), lambda j, d, r: (d, j, lat_tile(d, r), 0)),
        out_shape=jax.ShapeDtypeStruct((2, N_LANE_BLOCKS, u4l.shape[1], 2 * STATE_W), BF16),
        scratch_shapes=[pltpu.VMEM((nb, STATE_W), F32), pltpu.VMEM((nb, STATE_W), F32)],
        compiler_params=_params(3),
        name="s5_states",
    )(u4c, u4l, e_tab, a_tab)


def _s5_out_kernel(ul_ref, sin_ref, m_ref, f_ref, d_ref, y_ref):
    u = ul_ref[0]
    y_ref[0] = (u.astype(F32) * d_ref[0] + _bdot(u, m_ref[0])
                + _bdot(sin_ref[0, 0], f_ref[0, 0]) + _bdot(sin_ref[1, 0], f_ref[1, 0]))


def _s5_outputs(u4l, sin, m_tab, f_tab, d_tab):
    rows = S5_OUT_ROWS
    return pl.pallas_call(
        _s5_out_kernel,
        grid=(N_LANE_BLOCKS, u4l.shape[1] // rows),
        in_specs=[pl.BlockSpec((1, rows, CHUNK_W), lambda j, r: (j, r, 0)),
                  pl.BlockSpec((2, 1, rows, 2 * STATE_W), lambda j, r: (0, j, r, 0)),
                  pl.BlockSpec((1, CHUNK_W, CHUNK_W), lambda j, r: (j, 0, 0)),
                  pl.BlockSpec((2, 1, 2 * STATE_W, CHUNK_W), lambda j, r: (0, j, 0, 0)),
                  pl.BlockSpec((1, 1, CHUNK_W), lambda j, r: (j, 0, 0))],
        out_specs=pl.BlockSpec((1, rows, CHUNK_W), lambda j, r: (j, r, 0)),
        out_shape=jax.ShapeDtypeStruct(u4l.shape, F32),
        compiler_params=_params(2),
        name="s5_outputs",
    )(u4l, sin, m_tab, f_tab, d_tab)


def _merge_kernel(attn_ref, y4_ref, g_ref, x_ref, g1_ref, sh2_ref, sc2_ref, n2_ref,
                  sel_ref, wab_ref, wglu_ref, wout_ref, x1_ref, h2_ref):
    nb, tokens_all, _ = x_ref.shape
    tokens = tokens_all // MERGE_SPLIT
    rows = nb * tokens
    crow = tokens // CHUNK * nb

    def branch_dots(i):
        ts = slice(i * tokens, (i + 1) * tokens)
        act = [_gelu_tanh(y4_ref[j, i * crow:(i + 1) * crow]).astype(BF16)
               for j in range(N_LANE_BLOCKS)]
        act = jnp.concatenate(
            [jnp.concatenate([act[j][:, s * LANES:(s + 1) * LANES] for s in range(CHUNK)], axis=0)
             for j in range(N_LANE_BLOCKS)], axis=-1)
        act = _bdot(sel_ref[...], act).astype(BF16)
        p_attn = _bdot(attn_ref[:, ts].reshape(rows, ATTN_DIM), wab_ref[...])
        return p_attn, _bdot(act, wglu_ref[...])

    def mix_dot(i, p_attn, glu):
        ts = slice(i * tokens, (i + 1) * tokens)
        p_ssm = glu[:, :D_MODEL] * _sigmoid(glu[:, D_MODEL:])
        g = g_ref[:, ts].reshape(rows, 2 * D_MODEL).astype(F32)
        mix = _sigmoid(g[:, :D_MODEL]) * p_attn + _sigmoid(g[:, D_MODEL:]) * p_ssm
        return _bdot(mix.astype(BF16), wout_ref[...]).reshape(nb, tokens, D_MODEL)

    def finish(i, x_mix):
        ts = slice(i * tokens, (i + 1) * tokens)
        x1 = x_ref[:, ts] + g1_ref[...] * x_mix
        x1_ref[:, ts] = x1
        h2 = _rms(x1) * n2_ref[...]
        h2_ref[:, ts] = (h2 * (1.0 + sc2_ref[...]) + sh2_ref[...]).astype(h2_ref.dtype)

    dots = [None] * MERGE_SPLIT
    mixed = [None] * MERGE_SPLIT
    dots[0] = branch_dots(0)
    for i in range(MERGE_SPLIT):
        if i + 1 < MERGE_SPLIT:
            dots[i + 1] = branch_dots(i + 1)
        mixed[i] = mix_dot(i, *dots[i])
        if i > 0:
            finish(i - 1, mixed[i - 1])
    finish(MERGE_SPLIT - 1, mixed[MERGE_SPLIT - 1])


def _merge(attn, y4, g, x, g1, sh2, sc2, n2, wab, wglu, wout):
    b, l, _ = x.shape
    tt = TOK_TILE
    tok = lambda w: pl.BlockSpec((b, tt, w), lambda i: (0, i, 0))
    per_b = _const_spec((b, 1, D_MODEL))
    y4_spec = pl.BlockSpec((N_LANE_BLOCKS, tt // CHUNK * b, CHUNK_W), lambda i: (0, i, 0))
    sel = _chunk_row_select(b, tt // MERGE_SPLIT, inverse=True)
    return pl.pallas_call(
        _merge_kernel,
        grid=(l // tt,),
        in_specs=[tok(ATTN_DIM), y4_spec, tok(2 * D_MODEL), tok(D_MODEL),
                  per_b, per_b, per_b, _const_spec((1, D_MODEL)), _const_spec(sel.shape),
                  _const_spec(wab.shape), _const_spec(wglu.shape), _const_spec(wout.shape)],
        out_specs=[tok(D_MODEL), tok(D_MODEL)],
        out_shape=[jax.ShapeDtypeStruct((b, l, D_MODEL), F32),
                   jax.ShapeDtypeStruct((b, l, D_MODEL), BF16)],
        compiler_params=_params(1),
        name="merge",
    )(attn, y4, g, x, g1, sh2, sc2, n2, sel, wab, wglu, wout)


def _ffn_row_select(tm):
    seg_len = tm // SUB
    tile = np.zeros((tm, tm), np.float32)
    for seg in range(SUB):
        for i in range(seg_len):
            tile[SUB * i + seg, seg * seg_len + i] = 1.0
    halo = np.zeros((SUB, 2 * HALO), np.float32)
    halo[0, HALO - 1] = 1.0
    halo[1, HALO] = 1.0
    return jnp.asarray(tile, dtype=BF16), jnp.asarray(halo, dtype=BF16)


def _ffn_kernel(sel_ref, selh_ref, hp_ref, h_ref, hn_ref, x1_ref, g2_ref, wup_ref, cw_ref, cb_ref,
                wd_ref, fw_ref, o_ref, acc_scr, *, tm):
    j = pl.program_id(1)
    keep_prev = (j > 0).astype(F32)
    keep_next = (j < pl.num_programs(1) - 1).astype(F32)
    halo = jnp.concatenate([hp_ref[0], hn_ref[0]], axis=0)
    lhs = jnp.concatenate([_bdot(selh_ref[...], halo), _bdot(sel_ref[...], h_ref[0])],
                          axis=0).astype(BF16)
    sub = lax.broadcasted_iota(jnp.int32, (SUB, 1), 0)

    def up(f):
        return (_bdot(lhs, wup_ref[:, f * FF_CHUNK:(f + 1) * FF_CHUNK]),
                _bdot(lhs, wup_ref[:, D_FF + f * FF_CHUNK:D_FF + (f + 1) * FF_CHUNK]))

    def conv(zall, col):
        z = zall[SUB:]
        before = jnp.where(sub == 0, zall[0:1] * keep_prev, pltpu.roll(z[tm - SUB:], 1, 0))
        after = jnp.where(sub == SUB - 1, zall[1:2] * keep_next, pltpu.roll(z[:SUB], SUB - 1, 0))
        z_prev = jnp.concatenate([before, z[:tm - SUB]], axis=0)
        z_next = jnp.concatenate([z[SUB:], after], axis=0)
        cw = cw_ref[:, col:col + FF_CHUNK]
        return z_prev * cw[0:1] + z * cw[1:2] + z_next * cw[2:3] + cb_ref[:, col:col + FF_CHUNK]

    acts = []
    cur = up(0)
    for f in range(N_FF_CHUNKS):
        nxt = up(f + 1) if f + 1 < N_FF_CHUNKS else None
        val = conv(cur[0], f * FF_CHUNK)
        gate = conv(cur[1], D_FF + f * FF_CHUNK)
        acts.append((gate * _sigmoid(gate) * val).astype(BF16))
        cur = nxt
    acc = _bdot(jnp.concatenate(acts, axis=-1), wd_ref[...])
    seg_len = tm // SUB
    for k in range(D_MODEL // LANES):
        acc_scr[k] = acc[:, k * LANES:(k + 1) * LANES]
        for seg in range(SUB):
            o_ref[0, seg * seg_len:(seg + 1) * seg_len, k * LANES:(k + 1) * LANES] = (
                acc_scr[k, pl.ds(seg, seg_len, stride=SUB), :])
    x2 = x1_ref[0] + g2_ref[0] * o_ref[0]
    o_ref[0] = _rms(x2) * fw_ref[...]


def _conv_ffn(h2, x1, g2, wup, cw, cb, wd, fw):
    b, l, _ = x1.shape
    tm = TM_FFN
    nh = tm // HALO
    last = l // HALO - 1
    tok = lambda: pl.BlockSpec((1, tm, D_MODEL), lambda i, j: (i, j, 0))
    prev = pl.BlockSpec((1, HALO, D_MODEL), lambda i, j: (i, jnp.maximum(j * nh - 1, 0), 0))
    nxt = pl.BlockSpec((1, HALO, D_MODEL), lambda i, j: (i, jnp.minimum((j + 1) * nh, last), 0))
    per_b = pl.BlockSpec((1, 1, D_MODEL), lambda i, j: (i, 0, 0))
    once = lambda shape: pl.BlockSpec(shape, lambda *_: (0,) * len(shape),
                                      pipeline_mode=pl.Buffered(1))
    sel, selh = _ffn_row_select(tm)
    return pl.pallas_call(
        functools.partial(_ffn_kernel, tm=tm),
        grid=(b, l // tm),
        in_specs=[once(sel.shape), once(selh.shape), prev, tok(), nxt, tok(), per_b,
                  once(wup.shape), once(cw.shape), once(cb.shape), once(wd.shape),
                  _const_spec((1, D_MODEL))],
        out_specs=tok(),
        out_shape=jax.ShapeDtypeStruct((b, l, D_MODEL), F32),
        scratch_shapes=[pltpu.VMEM((D_MODEL // LANES, tm, LANES), F32)],
        compiler_params=_params(2),
        name="conv_ffn",
    )(sel, selh, h2, h2, h2, x1, g2, wup, cw, cb, wd, fw)


def _rope_tables(l):
    rows = jnp.repeat(jnp.arange(l // GRID_W, dtype=F32), GRID_W)
    cols = jnp.tile(jnp.arange(GRID_W, dtype=F32), l // GRID_W)
    inv_freq = ROPE_THETA ** (-jnp.arange(0, ROPE_AXIS_DIM, 2, dtype=F32) / ROPE_AXIS_DIM)
    ang = jnp.concatenate([rows[:, None] * inv_freq, cols[:, None] * inv_freq], axis=-1)
    cos = jnp.repeat(jnp.cos(ang), 2, axis=-1)
    sin = jnp.repeat(jnp.sin(ang), 2, axis=-1)
    odd = (jnp.arange(HEAD_DIM) % 2 == 1)[None, :]
    return cos, jnp.where(odd, sin, 0.0), jnp.where(odd, 0.0, -sin)


def kernel(x, c, ctx, c_ctx, w_mod, b_mod, norm1_w, norm2_w, w_in, q_norm_w, k_norm_w, w_attn_br,
           ssm_lambda_re, ssm_lambda_im, ssm_log_dt, ssm_b_re, ssm_b_im, ssm_c_re, ssm_c_im, ssm_d,
           w_glu, w_out, w_up, conv_w, conv_b, w_down, final_norm_w):
    b, l, d = x.shape
    lc = ctx.shape[1]
    assert w_mod.shape[0] == 1 and d == D_MODEL and l % TOK_TILE == 0 and lc % TOK_TILE == 0
    assert (l // CHUNK * b) % S5_STATE_ROWS == 0 and (l // CHUNK * b) % S5_OUT_ROWS == 0
    layer = 0

    pad = (-(b + 1)) % 16
    c_rows = jnp.concatenate([c, c_ctx[None, :], jnp.zeros((pad, d), F32)], axis=0)
    mod = _modulation(c_rows, w_mod[layer], b_mod[layer])
    sh1, sc1, g1, sh2, sc2, g2 = [m[:b, None, :] for m in jnp.split(mod, N_MOD, axis=-1)]
    csh1, csc1 = mod[b:b + 1, None, :d], mod[b:b + 1, None, d:2 * d]

    w_in_b = w_in[layer].astype(BF16)
    qn = (q_norm_w[layer] * (1.0 / math.sqrt(HEAD_DIM))).reshape(1, HEAD_DIM)
    kn = k_norm_w[layer].reshape(1, HEAD_DIM)
    n1 = norm1_w[layer].reshape(1, d)

    no_rope = (jnp.ones((lc, HEAD_DIM), F32),) + (jnp.zeros((lc, HEAD_DIM), F32),) * 2
    q, k, v, u4l, g = _inproj(x, sh1, sc1, n1, w_in_b, qn, kn, _rope_tables(l), latent=True)
    kc, vc, u4c = _inproj(ctx, csh1, csc1, n1, w_in_b, qn, kn, no_rope, latent=False)

    attn = _attention(q, kc, k, vc, v)

    disc = _s5_discretise(ssm_lambda_re[layer], ssm_lambda_im[layer], ssm_log_dt[layer],
                          ssm_b_re[layer], ssm_b_im[layer])
    m_tab, e_tab, f_tab, a_pow = _s5_tables(*disc, ssm_c_re[layer], ssm_c_im[layer])
    a_tab = a_pow.reshape(2, N_LANE_BLOCKS, 2, 1, STATE_W)
    d_tab = jnp.tile(ssm_d[layer].reshape(N_LANE_BLOCKS, 1, LANES), (1, CHUNK, 1))
    d_tab = d_tab.reshape(N_LANE_BLOCKS, 1, CHUNK_W)
    sin = _s5_states(u4c, u4l, e_tab, a_tab, b)
    y4 = _s5_outputs(u4l, sin, m_tab, f_tab, d_tab)

    x1, h2 = _merge(attn, y4, g, x, g1, sh2, sc2, norm2_w[layer].reshape(1, d),
                    w_attn_br[layer].astype(BF16), w_glu[layer].astype(BF16),
                    w_out[layer].astype(BF16))

    return _conv_ffn(h2, x1, g2, w_up[layer].astype(BF16), conv_w[layer],
                     conv_b[layer].reshape(1, 2 * D_FF), w_down[layer].astype(BF16),
                     final_norm_w.reshape(1, d))
```

```python
import functools
import math

import jax
import jax.numpy as jnp
import numpy as np
from jax import lax
from jax.experimental import pallas as pl
from jax.experimental.pallas import tpu as pltpu

F32 = jnp.float32
BF16 = jnp.bfloat16

D_MODEL = 1024
GRID_W = 64
N_HEADS = 8
N_KV_HEADS = 2
HEAD_DIM = 128
HEADS_PER_KV = N_HEADS // N_KV_HEADS
ATTN_DIM = N_HEADS * HEAD_DIM
KV_DIM = N_KV_HEADS * HEAD_DIM
ROPE_THETA = 10000.0
ROPE_AXIS_DIM = HEAD_DIM // 2
SSM_DIM = 512
SSM_GROUP = 16
N_SSM_GROUPS = SSM_DIM // SSM_GROUP
SSM_STATE = 64
Q_END = ATTN_DIM
K_END = Q_END + KV_DIM
V_END = K_END + KV_DIM
U_END = V_END + SSM_DIM
D_IN = U_END + 2 * D_MODEL
D_FF = 2816
N_MOD = 6
EPS = 1e-6

LANES = 128
SUB = 8
CHUNK = 8
N_LANE_BLOCKS = SSM_DIM // LANES
GROUPS_PER_BLOCK = LANES // SSM_GROUP
CHUNK_W = CHUNK * LANES
STATE_W = GROUPS_PER_BLOCK * SSM_STATE

FF_CHUNK = 256
N_FF_CHUNKS = D_FF // FF_CHUNK
HALO = 16

VMEM_LIMIT = 56 * 1024 * 1024

MOD_COL_TILE = 1536
TOK_TILE = 64
TQ_ATTN = 1024
PROJ_SPLIT = 4
MERGE_SPLIT = 4
TM_FFN = 512
S5_STATE_ROWS = 1024
S5_STATE_PART = 512
S5_OUT_ROWS = 1024


def _sigmoid(x):
    return 1.0 / (1.0 + jnp.exp(-x))


def _gelu_tanh(x):
    return 0.5 * x * (1.0 + jnp.tanh(math.sqrt(2.0 / math.pi) * (x + 0.044715 * (x * x * x))))


def _rms(x):
    return x * lax.rsqrt(jnp.mean(x * x, axis=-1, keepdims=True) + EPS)


def _bdot(a, b):
    return jnp.dot(a, b, preferred_element_type=F32)


def _const_spec(shape):
    nd = len(shape)
    return pl.BlockSpec(shape, lambda *_: (0,) * nd)


def _params(n_axes):
    return pltpu.CompilerParams(dimension_semantics=("arbitrary",) * n_axes,
                                vmem_limit_bytes=VMEM_LIMIT)


def _split_bf16(v):
    hi = v.astype(BF16)
    return hi, (v - hi.astype(F32)).astype(BF16)


def _mod_kernel(c_ref, w_ref, b_ref, o_ref):
    c = c_ref[...]
    rows = c.shape[0]
    a_hi, a_lo = _split_bf16(c * _sigmoid(c))
    w_hi, w_lo = _split_bf16(w_ref[...])
    both = _bdot(jnp.concatenate([a_hi, a_lo], axis=0), w_hi)
    o_ref[...] = both[:rows] + both[rows:] + _bdot(a_hi, w_lo) + b_ref[...]


def _modulation(c_rows, w_mod, b_mod):
    rows = c_rows.shape[0]
    n = w_mod.shape[1]
    tn = MOD_COL_TILE
    return pl.pallas_call(
        _mod_kernel,
        grid=(n // tn,),
        in_specs=[pl.BlockSpec((rows, D_MODEL), lambda j: (0, 0)),
                  pl.BlockSpec((D_MODEL, tn), lambda j: (0, j)),
                  pl.BlockSpec((1, tn), lambda j: (0, j))],
        out_specs=pl.BlockSpec((rows, tn), lambda j: (0, j)),
        out_shape=jax.ShapeDtypeStruct((rows, n), F32),
        compiler_params=_params(1),
        name="modulation",
    )(c_rows, w_mod, b_mod.reshape(1, n))


def _norm_rope_store(t, nw, rope, out_ref, ts, col, nb):
    t = _rms(t) * nw
    if rope is not None:
        cos, sin_prev, sin_next = rope
        prev = pltpu.roll(t, 1, 1).reshape(nb, -1, HEAD_DIM)
        nxt = pltpu.roll(t, HEAD_DIM - 1, 1).reshape(nb, -1, HEAD_DIM)
        t3 = t.reshape(nb, -1, HEAD_DIM) * cos + prev * sin_prev + nxt * sin_next
    else:
        t3 = t.reshape(nb, -1, HEAD_DIM)
    out_ref[:, ts, col:col + HEAD_DIM] = t3.astype(out_ref.dtype)


def _chunk_row_select(nb, tokens, inverse=False):
    sel = np.zeros((nb * tokens, nb * tokens), np.float32)
    for b in range(nb):
        for t in range(tokens):
            k, s = divmod(t, CHUNK)
            sel[(s * (tokens // CHUNK) + k) * nb + b, b * tokens + t] = 1.0
    return jnp.asarray(sel.T if inverse else sel, dtype=BF16)


def _store_chunk_rows(u_perm, u4_ref, row0):
    rows = u_perm.shape[0] // CHUNK
    for j in range(N_LANE_BLOCKS):
        for s in range(CHUNK):
            u4_ref[j, row0:row0 + rows, s * LANES:(s + 1) * LANES] = (
                u_perm[s * rows:(s + 1) * rows, j * LANES:(j + 1) * LANES].astype(u4_ref.dtype))


def _inproj_kernel(x_ref, sh_ref, sc_ref, n1_ref, w_ref, sel_ref, qn_ref, kn_ref,
                   cos_ref, sinp_ref, sinn_ref, *refs, latent):
    nb, tokens_all, _ = x_ref.shape
    tokens = tokens_all // PROJ_SPLIT
    rows = nb * tokens
    if latent:
        q_ref, k_ref, v_ref, u4_ref, g_ref = refs
    else:
        k_ref, v_ref, u4_ref = refs

    def lhs(i):
        ts = slice(i * tokens, (i + 1) * tokens)
        h = _rms(x_ref[:, ts]) * n1_ref[...]
        h = h * (1.0 + sc_ref[...]) + sh_ref[...]
        return h.reshape(rows, D_MODEL).astype(BF16)

    def project(i, hb):
        ts = slice(i * tokens, (i + 1) * tokens)
        rope = (cos_ref[ts], sinp_ref[ts], sinn_ref[ts]) if latent else None
        if latent:
            q = _bdot(hb, w_ref[:, :Q_END])
            for hd in range(N_HEADS):
                _norm_rope_store(q[:, hd * HEAD_DIM:(hd + 1) * HEAD_DIM], qn_ref[...], rope,
                                 q_ref, ts, hd * HEAD_DIM, nb)
        k = _bdot(hb, w_ref[:, Q_END:K_END])
        for hd in range(N_KV_HEADS):
            _norm_rope_store(k[:, hd * HEAD_DIM:(hd + 1) * HEAD_DIM], kn_ref[...], rope,
                             k_ref, ts, hd * HEAD_DIM, nb)
        v_ref[:, ts] = _bdot(hb, w_ref[:, K_END:V_END]).reshape(nb, tokens, KV_DIM).astype(v_ref.dtype)
        hb_perm = _bdot(sel_ref[...], hb).astype(BF16)
        _store_chunk_rows(_bdot(hb_perm, w_ref[:, V_END:U_END]), u4_ref, i * (rows // CHUNK))
        if latent:
            g_ref[:, ts] = _bdot(hb, w_ref[:, U_END:]).reshape(
                nb, tokens, 2 * D_MODEL).astype(g_ref.dtype)

    cur = lhs(0)
    for i in range(PROJ_SPLIT):
        nxt = lhs(i + 1) if i + 1 < PROJ_SPLIT else None
        project(i, cur)
        cur = nxt


def _inproj(x, sh1, sc1, n1, w_in_b, qn, kn, rope, latent):
    b, l, _ = x.shape
    tt = TOK_TILE
    tok = lambda w: pl.BlockSpec((b, tt, w), lambda i: (0, i, 0))
    mod_spec = _const_spec(sh1.shape)
    rope_spec = pl.BlockSpec((tt, HEAD_DIM), lambda i: (i, 0))
    u4_spec = pl.BlockSpec((N_LANE_BLOCKS, tt // CHUNK * b, CHUNK_W), lambda i: (0, i, 0))
    kv_u_specs = [tok(KV_DIM), tok(KV_DIM), u4_spec]
    sel = _chunk_row_select(b, tt // PROJ_SPLIT)
    kv_u_shapes = [jax.ShapeDtypeStruct((b, l, KV_DIM), BF16),
                   jax.ShapeDtypeStruct((b, l, KV_DIM), BF16),
                   jax.ShapeDtypeStruct((N_LANE_BLOCKS, l // CHUNK * b, CHUNK_W), BF16)]
    if latent:
        out_specs = [tok(ATTN_DIM)] + kv_u_specs + [tok(2 * D_MODEL)]
        out_shape = ([jax.ShapeDtypeStruct((b, l, ATTN_DIM), BF16)] + kv_u_shapes
                     + [jax.ShapeDtypeStruct((b, l, 2 * D_MODEL), BF16)])
    else:
        out_specs, out_shape = kv_u_specs, kv_u_shapes
    return pl.pallas_call(
        functools.partial(_inproj_kernel, latent=latent),
        grid=(l // tt,),
        in_specs=[tok(D_MODEL), mod_spec, mod_spec, _const_spec((1, D_MODEL)),
                  _const_spec(w_in_b.shape), _const_spec(sel.shape),
                  _const_spec((1, HEAD_DIM)), _const_spec((1, HEAD_DIM)),
                  rope_spec, rope_spec, rope_spec],
        out_specs=out_specs,
        out_shape=out_shape,
        compiler_params=_params(1),
        name="inproj_latent" if latent else "inproj_ctx",
    )(x, sh1, sc1, n1, w_in_b, sel, qn, kn, *rope)


def _attn_kernel(q_ref, kc_ref, k_ref, vc_ref, v_ref, o_ref):
    keys = jnp.concatenate([kc_ref[0], k_ref[0]], axis=0)
    vals = jnp.concatenate([vc_ref[0], v_ref[0]], axis=0)
    vals = jnp.concatenate([vals, jnp.ones_like(vals)], axis=-1)
    nt = (((1,), (1,)), ((), ()))

    def scores(r):
        q = q_ref[0, :, r * HEAD_DIM:(r + 1) * HEAD_DIM]
        return lax.dot_general(q, keys, nt, preferred_element_type=F32)

    cur = scores(0)
    for r in range(HEADS_PER_KV):
        nxt = scores(r + 1) if r + 1 < HEADS_PER_KV else None
        p = jnp.exp(cur - jnp.max(cur, axis=-1, keepdims=True)).astype(BF16)
        o = _bdot(p, vals)
        o_ref[0, :, r * HEAD_DIM:(r + 1) * HEAD_DIM] = (
            o[:, :HEAD_DIM] / o[:, HEAD_DIM:]).astype(o_ref.dtype)
        cur = nxt


def _attention(q, kc, k, vc, v):
    b, l, _ = q.shape
    lc = kc.shape[1]
    tq = TQ_ATTN
    qspec = pl.BlockSpec((1, tq, HEADS_PER_KV * HEAD_DIM), lambda i, h, j: (i, j, h))
    kv = lambda n: pl.BlockSpec((1, n, HEAD_DIM), lambda i, h, j: (i, 0, h))
    return pl.pallas_call(
        _attn_kernel,
        grid=(b, N_KV_HEADS, l // tq),
        in_specs=[qspec, kv(lc), kv(l), kv(lc), kv(l)],
        out_specs=qspec,
        out_shape=jax.ShapeDtypeStruct((b, l, ATTN_DIM), BF16),
        compiler_params=_params(3),
        name="attention",
    )(q, kc, k, vc, v)


def _s5_discretise(lam_re, lam_im, log_dt, b_re, b_im):
    dt = jnp.exp(log_dt)[..., None]
    mag = jnp.exp(lam_re * dt)
    ang = lam_im * dt
    a_re = mag * jnp.cos(ang)
    a_im = mag * jnp.sin(ang)
    den = lam_re * lam_re + lam_im * lam_im
    nr = a_re - 1.0
    ni = a_im
    f_re = ((nr * lam_re + ni * lam_im) / den)[:, :, None, :]
    f_im = ((ni * lam_re - nr * lam_im) / den)[:, :, None, :]
    bt_re = b_re.transpose(0, 1, 3, 2)
    bt_im = b_im.transpose(0, 1, 3, 2)
    return (a_re[:, :, None, :], a_im[:, :, None, :],
            f_re * bt_re - f_im * bt_im, f_re * bt_im + f_im * bt_re)


def _s5_table_kernel(a_re_ref, a_im_ref, bb_re_ref, bb_im_ref, c_re_ref, c_im_ref,
                     m_ref, e_ref, f_ref, ap_ref, x_scr, e_scr, f_scr, c_scr):
    hp = lax.Precision.HIGHEST
    i32 = jnp.int32
    na, p_, t_ = GROUPS_PER_BLOCK, SSM_GROUP, CHUNK
    nt = (((1,), (1,)), ((), ()))
    rows = na * p_
    for d in range(2):
        a_re = jnp.broadcast_to(a_re_ref[d], (na, p_, SSM_STATE)).reshape(rows, SSM_STATE)
        a_im = jnp.broadcast_to(a_im_ref[d], (na, p_, SSM_STATE)).reshape(rows, SSM_STATE)
        b_re, b_im = bb_re_ref[d].reshape(rows, SSM_STATE), bb_im_ref[d].reshape(rows, SSM_STATE)
        c_re, c_im = c_re_ref[d].reshape(rows, SSM_STATE), c_im_ref[d].reshape(rows, SSM_STATE)
        c_scr[d, 0] = c_re
        c_scr[d, 1] = c_im
        pw_re, pw_im = jnp.ones_like(a_re), jnp.zeros_like(a_im)
        for tau in range(t_ + 1):
            if tau < t_:
                pb_re = b_re * pw_re - b_im * pw_im
                pb_im = b_re * pw_im + b_im * pw_re
                x_scr[d, 0, tau * rows:(tau + 1) * rows, :] = pb_re
                x_scr[d, 1, tau * rows:(tau + 1) * rows, :] = pb_im
                s_in = t_ - 1 - tau if d == 0 else tau
                e_scr[d, s_in * rows:(s_in + 1) * rows, 0:SSM_STATE] = pb_re
                e_scr[d, s_in * rows:(s_in + 1) * rows, SSM_STATE:] = pb_im
            if tau > 0:
                t_out = tau - 1 if d == 0 else t_ - tau
                f_scr[d, t_out * rows:(t_out + 1) * rows, 0:SSM_STATE] = c_re * pw_re - c_im * pw_im
                f_scr[d, t_out * rows:(t_out + 1) * rows, SSM_STATE:] = -(c_re * pw_im + c_im * pw_re)
            if tau == t_:
                ap_ref[d, 0, 0] = pw_re.reshape(na, p_, SSM_STATE)[:, 0:1, :]
                ap_ref[d, 0, 1] = pw_im.reshape(na, p_, SSM_STATE)[:, 0:1, :]
            else:
                pw_re, pw_im = pw_re * a_re - pw_im * a_im, pw_re * a_im + pw_im * a_re

    grp_bits = p_.bit_length() - 1
    st_bits = SSM_STATE.bit_length() - 1
    blk_bits = na.bit_length() - 1
    gmask = na - 1
    r_w = lax.broadcasted_iota(i32, (CHUNK_W, 1), 0)
    c_w = lax.broadcasted_iota(i32, (1, CHUNK_W), 1)
    c_l = lax.broadcasted_iota(i32, (1, LANES), 1)
    grp_tok_r = (r_w >> grp_bits) & gmask
    grp_tok_c = (c_w >> grp_bits) & gmask
    grp_st_r = (r_w >> st_bits) & gmask
    grp_st_c = (c_w >> st_bits) & gmask

    lag = []
    for d in range(2):
        full = (lax.dot_general(x_scr[d, 0], c_scr[d, 0], nt, precision=hp, preferred_element_type=F32)
                - lax.dot_general(x_scr[d, 1], c_scr[d, 1], nt, precision=hp, preferred_element_type=F32))
        lag.append(jnp.where(grp_tok_r == (c_l >> grp_bits), full, 0.0))
    for s in range(t_):
        for t in range(t_):
            if t > s:
                blk = lag[0][(t - s) * LANES:(t - s + 1) * LANES]
            elif s > t:
                blk = lag[1][(s - t) * LANES:(s - t + 1) * LANES]
            else:
                blk = lag[0][:LANES] + lag[1][:LANES]
            m_ref[0, s * LANES:(s + 1) * LANES, t * LANES:(t + 1) * LANES] = blk.astype(m_ref.dtype)

    r_l = lax.broadcasted_iota(i32, (LANES, 1), 0)
    rep = (((r_l >> st_bits) == (c_w >> (st_bits + blk_bits)))
           & ((r_l & (SSM_STATE - 1)) == (c_w & (SSM_STATE - 1)))).astype(F32).astype(BF16)
    rep_t = (((c_l >> st_bits) == (r_w >> (st_bits + blk_bits)))
             & ((c_l & (SSM_STATE - 1)) == (r_w & (SSM_STATE - 1)))).astype(F32).astype(BF16)
    for d in range(2):
        e_full = _bdot(e_scr[d].astype(BF16), rep)
        f_full = lax.dot_general(rep_t, f_scr[d].astype(BF16), nt, preferred_element_type=F32)
        e_ref[d, 0] = jnp.where(grp_tok_r == grp_st_c, e_full, 0.0).astype(e_ref.dtype)
        f_ref[d, 0] = jnp.where(grp_st_r == grp_tok_c, f_full, 0.0).astype(f_ref.dtype)


def _s5_tables(a_re, a_im, bb_re, bb_im, c_re, c_im):
    nj, na = N_LANE_BLOCKS, GROUPS_PER_BLOCK
    grp = lambda r: pl.BlockSpec((2, na, r, SSM_STATE), lambda j: (0, j, 0, 0))
    dj = lambda r, c: pl.BlockSpec((2, 1, r, c), lambda j: (0, j, 0, 0))
    src = lambda: pltpu.VMEM((2, 2, CHUNK_W, SSM_STATE), F32)
    pair = lambda: pltpu.VMEM((2, CHUNK_W, 2 * SSM_STATE), F32)
    return pl.pallas_call(
        _s5_table_kernel,
        grid=(nj,),
        in_specs=[grp(1), grp(1), grp(SSM_GROUP), grp(SSM_GROUP), grp(SSM_GROUP), grp(SSM_GROUP)],
        out_specs=[pl.BlockSpec((1, CHUNK_W, CHUNK_W), lambda j: (j, 0, 0)),
                   dj(CHUNK_W, 2 * STATE_W), dj(2 * STATE_W, CHUNK_W),
                   pl.BlockSpec((2, 1, 2, na, 1, SSM_STATE), lambda j: (0, j, 0, 0, 0, 0))],
        out_shape=[jax.ShapeDtypeStruct((nj, CHUNK_W, CHUNK_W), BF16),
                   jax.ShapeDtypeStruct((2, nj, CHUNK_W, 2 * STATE_W), BF16),
                   jax.ShapeDtypeStruct((2, nj, 2 * STATE_W, CHUNK_W), BF16),
                   jax.ShapeDtypeStruct((2, nj, 2, na, 1, SSM_STATE), F32)],
        scratch_shapes=[src(), pair(), pair(), pltpu.VMEM((2, 2, LANES, SSM_STATE), F32)],
        compiler_params=_params(1),
        name="s5_tables",
    )(a_re, a_im, bb_re, bb_im, c_re, c_im)


def _s5_state_kernel(uc_ref, ul_ref, e_ref, a_ref, sin_ref, sre_scr, sim_scr, *, nb):
    d = pl.program_id(1)
    r = pl.program_id(2)
    a_re = jnp.broadcast_to(a_ref[0, 0, 0], (nb, STATE_W))
    a_im = jnp.broadcast_to(a_ref[0, 0, 1], (nb, STATE_W))

    def run(u_ref, reverse, store):
        rows = u_ref.shape[1]
        part = min(rows, S5_STATE_PART)
        parts = list(range(rows // part))
        chunks = list(range(part // nb))
        if reverse:
            parts, chunks = parts[::-1], chunks[::-1]
        inj = [_bdot(u_ref[0, p * part:(p + 1) * part], e_ref[0, 0]) for p in parts]
        s_re, s_im = sre_scr[...], sim_scr[...]
        for c, p in zip(inj, parts):
            for k in chunks:
                if store:
                    row = p * part + k * nb
                    sin_ref[0, 0, row:row + nb, 0:STATE_W] = s_re.astype(sin_ref.dtype)
                    sin_ref[0, 0, row:row + nb, STATE_W:] = s_im.astype(sin_ref.dtype)
                c_re = c[k * nb:(k + 1) * nb, 0:STATE_W]
                c_im = c[k * nb:(k + 1) * nb, STATE_W:]
                s_re, s_im = s_re * a_re - s_im * a_im + c_re, s_re * a_im + s_im * a_re + c_im
        sre_scr[...] = s_re
        sim_scr[...] = s_im

    @pl.when(r == 0)
    def _():
        sre_scr[...] = jnp.zeros_like(sre_scr)
        sim_scr[...] = jnp.zeros_like(sim_scr)

    for reverse in (False, True):
        @pl.when((r == 0) & (d == int(reverse)))
        def _():
            run(uc_ref, reverse, False)

        @pl.when((r > 0) & (d == int(reverse)))
        def _():
            run(ul_ref, reverse, True)


def _s5_states(u4c, u4l, e_tab, a_tab, nb):
    rows_c = u4c.shape[1]
    rows = S5_STATE_ROWS
    n_tiles = u4l.shape[1] // rows

    def lat_tile(d, r):
        t = jnp.maximum(r - 1, 0)
        return jnp.where(d == 0, t, n_tiles - 1 - t)

    return pl.pallas_call(
        functools.partial(_s5_state_kernel, nb=nb),
        grid=(N_LANE_BLOCKS, 2, n_tiles + 1),
        in_specs=[pl.BlockSpec((1, rows_c, CHUNK_W), lambda j, d, r: (j, 0, 0)),
                  pl.BlockSpec((1, rows, CHUNK_W), lambda j, d, r: (j, lat_tile(d, r), 0)),
                  pl.BlockSpec((1, 1, CHUNK_W, 2 * STATE_W), lambda j, d, r: (d, j, 0, 0)),
                  pl.BlockSpec((1, 1, 2, 1, STATE_W), lambda j, d, r: (d, j, 0, 0, 0))],
        out_specs=pl.BlockSpec((1, 1, rows, 2 * STATE_W), lambda j, d, r: (d, j, lat_tile(d, r), 0)),
        out_shape=jax.ShapeDtypeStruct((2, N_LANE_BLOCKS, u4l.shape[1], 2 * STATE_W), BF16),
        scratch_shapes=[pltpu.VMEM((nb, STATE_W), F32), pltpu.VMEM((nb, STATE_W), F32)],
        compiler_params=_params(3),
        name="s5_states",
    )(u4c, u4l, e_tab, a_tab)


def _s5_out_kernel(ul_ref, sin_ref, m_ref, f_ref, d_ref, y_ref):
    u = ul_ref[0]
    y_ref[0] = (u.astype(F32) * d_ref[0] + _bdot(u, m_ref[0])
                + _bdot(sin_ref[0, 0], f_ref[0, 0]) + _bdot(sin_ref[1, 0], f_ref[1, 0]))


def _s5_outputs(u4l, sin, m_tab, f_tab, d_tab):
    rows = S5_OUT_ROWS
    return pl.pallas_call(
        _s5_out_kernel,
        grid=(N_LANE_BLOCKS, u4l.shape[1] // rows),
        in_specs=[pl.BlockSpec((1, rows, CHUNK_W), lambda j, r: (j, r, 0)),
                  pl.BlockSpec((2, 1, rows, 2 * STATE_W), lambda j, r: (0, j, r, 0)),
                  pl.BlockSpec((1, CHUNK_W, CHUNK_W), lambda j, r: (j, 0, 0)),
                  pl.BlockSpec((2, 1, 2 * STATE_W, CHUNK_W), lambda j, r: (0, j, 0, 0)),
                  pl.BlockSpec((1, 1, CHUNK_W), lambda j, r: (j, 0, 0))],
        out_specs=pl.BlockSpec((1, rows, CHUNK_W), lambda j, r: (j, r, 0)),
        out_shape=jax.ShapeDtypeStruct(u4l.shape, F32),
        compiler_params=_params(2),
        name="s5_outputs",
    )(u4l, sin, m_tab, f_tab, d_tab)


def _merge_kernel(attn_ref, y4_ref, g_ref, x_ref, g1_ref, sh2_ref, sc2_ref, n2_ref,
                  sel_ref, wab_ref, wglu_ref, wout_ref, x1_ref, h2_ref):
    nb, tokens_all, _ = x_ref.shape
    tokens = tokens_all // MERGE_SPLIT
    rows = nb * tokens
    crow = tokens // CHUNK * nb

    def branch_dots(i):
        ts = slice(i * tokens, (i + 1) * tokens)
        act = [_gelu_tanh(y4_ref[j, i * crow:(i + 1) * crow]).astype(BF16)
               for j in range(N_LANE_BLOCKS)]
        act = jnp.concatenate(
            [jnp.concatenate([act[j][:, s * LANES:(s + 1) * LANES] for s in range(CHUNK)], axis=0)
             for j in range(N_LANE_BLOCKS)], axis=-1)
        act = _bdot(sel_ref[...], act).astype(BF16)
        p_attn = _bdot(attn_ref[:, ts].reshape(rows, ATTN_DIM), wab_ref[...])
        return p_attn, _bdot(act, wglu_ref[...])

    def mix_dot(i, p_attn, glu):
        ts = slice(i * tokens, (i + 1) * tokens)
        p_ssm = glu[:, :D_MODEL] * _sigmoid(glu[:, D_MODEL:])
        g = g_ref[:, ts].reshape(rows, 2 * D_MODEL).astype(F32)
        mix = _sigmoid(g[:, :D_MODEL]) * p_attn + _sigmoid(g[:, D_MODEL:]) * p_ssm
        return _bdot(mix.astype(BF16), wout_ref[...]).reshape(nb, tokens, D_MODEL)

    def finish(i, x_mix):
        ts = slice(i * tokens, (i + 1) * tokens)
        x1 = x_ref[:, ts] + g1_ref[...] * x_mix
        x1_ref[:, ts] = x1
        h2 = _rms(x1) * n2_ref[...]
        h2_ref[:, ts] = (h2 * (1.0 + sc2_ref[...]) + sh2_ref[...]).astype(h2_ref.dtype)

    dots = [None] * MERGE_SPLIT
    mixed = [None] * MERGE_SPLIT
    dots[0] = branch_dots(0)
    for i in range(MERGE_SPLIT):
        if i + 1 < MERGE_SPLIT:
            dots[i + 1] = branch_dots(i + 1)
        mixed[i] = mix_dot(i, *dots[i])
        if i > 0:
            finish(i - 1, mixed[i - 1])
    finish(MERGE_SPLIT - 1, mixed[MERGE_SPLIT - 1])


def _merge(attn, y4, g, x, g1, sh2, sc2, n2, wab, wglu, wout):
    b, l, _ = x.shape
    tt = TOK_TILE
    tok = lambda w: pl.BlockSpec((b, tt, w), lambda i: (0, i, 0))
    per_b = _const_spec((b, 1, D_MODEL))
    y4_spec = pl.BlockSpec((N_LANE_BLOCKS, tt // CHUNK * b, CHUNK_W), lambda i: (0, i, 0))
    sel = _chunk_row_select(b, tt // MERGE_SPLIT, inverse=True)
    return pl.pallas_call(
        _merge_kernel,
        grid=(l // tt,),
        in_specs=[tok(ATTN_DIM), y4_spec, tok(2 * D_MODEL), tok(D_MODEL),
                  per_b, per_b, per_b, _const_spec((1, D_MODEL)), _const_spec(sel.shape),
                  _const_spec(wab.shape), _const_spec(wglu.shape), _const_spec(wout.shape)],
        out_specs=[tok(D_MODEL), tok(D_MODEL)],
        out_shape=[jax.ShapeDtypeStruct((b, l, D_MODEL), F32),
                   jax.ShapeDtypeStruct((b, l, D_MODEL), BF16)],
        compiler_params=_params(1),
        name="merge",
    )(attn, y4, g, x, g1, sh2, sc2, n2, sel, wab, wglu, wout)


def _ffn_row_select(tm):
    seg_len = tm // SUB
    tile = np.zeros((tm, tm), np.float32)
    for seg in range(SUB):
        for i in range(seg_len):
            tile[SUB * i + seg, seg * seg_len + i] = 1.0
    halo = np.zeros((SUB, 2 * HALO), np.float32)
    halo[0, HALO - 1] = 1.0
    halo[1, HALO] = 1.0
    return jnp.asarray(tile, dtype=BF16), jnp.asarray(halo, dtype=BF16)


def _ffn_kernel(sel_ref, selh_ref, hp_ref, h_ref, hn_ref, x1_ref, g2_ref, wup_ref, cw_ref, cb_ref,
                wd_ref, fw_ref, o_ref, acc_scr, *, tm):
    j = pl.program_id(1)
    keep_prev = (j > 0).astype(F32)
    keep_next = (j < pl.num_programs(1) - 1).astype(F32)
    halo = jnp.concatenate([hp_ref[0], hn_ref[0]], axis=0)
    lhs = jnp.concatenate([_bdot(selh_ref[...], halo), _bdot(sel_ref[...], h_ref[0])],
                          axis=0).astype(BF16)
    sub = lax.broadcasted_iota(jnp.int32, (SUB, 1), 0)

    def up(f):
        return (_bdot(lhs, wup_ref[:, f * FF_CHUNK:(f + 1) * FF_CHUNK]),
                _bdot(lhs, wup_ref[:, D_FF + f * FF_CHUNK:D_FF + (f + 1) * FF_CHUNK]))

    def conv(zall, col):
        z = zall[SUB:]
        before = jnp.where(sub == 0, zall[0:1] * keep_prev, pltpu.roll(z[tm - SUB:], 1, 0))
        after = jnp.where(sub == SUB - 1, zall[1:2] * keep_next, pltpu.roll(z[:SUB], SUB - 1, 0))
        z_prev = jnp.concatenate([before, z[:tm - SUB]], axis=0)
        z_next = jnp.concatenate([z[SUB:], after], axis=0)
        cw = cw_ref[:, col:col + FF_CHUNK]
        return z_prev * cw[0:1] + z * cw[1:2] + z_next * cw[2:3] + cb_ref[:, col:col + FF_CHUNK]

    acts = []
    cur = up(0)
    for f in range(N_FF_CHUNKS):
        nxt = up(f + 1) if f + 1 < N_FF_CHUNKS else None
        val = conv(cur[0], f * FF_CHUNK)
        gate = conv(cur[1], D_FF + f * FF_CHUNK)
        acts.append((gate * _sigmoid(gate) * val).astype(BF16))
        cur = nxt
    acc = _bdot(jnp.concatenate(acts, axis=-1), wd_ref[...])
    seg_len = tm // SUB
    for k in range(D_MODEL // LANES):
        acc_scr[k] = acc[:, k * LANES:(k + 1) * LANES]
        for seg in range(SUB):
            o_ref[0, seg * seg_len:(seg + 1) * seg_len, k * LANES:(k + 1) * LANES] = (
                acc_scr[k, pl.ds(seg, seg_len, stride=SUB), :])
    x2 = x1_ref[0] + g2_ref[0] * o_ref[0]
    o_ref[0] = _rms(x2) * fw_ref[...]


def _conv_ffn(h2, x1, g2, wup, cw, cb, wd, fw):
    b, l, _ = x1.shape
    tm = TM_FFN
    nh = tm // HALO
    last = l // HALO - 1
    tok = lambda: pl.BlockSpec((1, tm, D_MODEL), lambda i, j: (i, j, 0))
    prev = pl.BlockSpec((1, HALO, D_MODEL), lambda i, j: (i, jnp.maximum(j * nh - 1, 0), 0))
    nxt = pl.BlockSpec((1, HALO, D_MODEL), lambda i, j: (i, jnp.minimum((j + 1) * nh, last), 0))
    per_b = pl.BlockSpec((1, 1, D_MODEL), lambda i, j: (i, 0, 0))
    once = lambda shape: pl.BlockSpec(shape, lambda *_: (0,) * len(shape),
                                      pipeline_mode=pl.Buffered(1))
    sel, selh = _ffn_row_select(tm)
    return pl.pallas_call(
        functools.partial(_ffn_kernel, tm=tm),
        grid=(b, l // tm),
        in_specs=[once(sel.shape), once(selh.shape), prev, tok(), nxt, tok(), per_b,
                  once(wup.shape), once(cw.shape), once(cb.shape), once(wd.shape),
                  _const_spec((1, D_MODEL))],
        out_specs=tok(),
        out_shape=jax.ShapeDtypeStruct((b, l, D_MODEL), F32),
        scratch_shapes=[pltpu.VMEM((D_MODEL // LANES, tm, LANES), F32)],
        compiler_params=_params(2),
        name="conv_ffn",
    )(sel, selh, h2, h2, h2, x1, g2, wup, cw, cb, wd, fw)


def _rope_tables(l):
    rows = jnp.repeat(jnp.arange(l // GRID_W, dtype=F32), GRID_W)
    cols = jnp.tile(jnp.arange(GRID_W, dtype=F32), l // GRID_W)
    inv_freq = ROPE_THETA ** (-jnp.arange(0, ROPE_AXIS_DIM, 2, dtype=F32) / ROPE_AXIS_DIM)
    ang = jnp.concatenate([rows[:, None] * inv_freq, cols[:, None] * inv_freq], axis=-1)
    cos = jnp.repeat(jnp.cos(ang), 2, axis=-1)
    sin = jnp.repeat(jnp.sin(ang), 2, axis=-1)
    odd = (jnp.arange(HEAD_DIM) % 2 == 1)[None, :]
    return cos, jnp.where(odd, sin, 0.0), jnp.where(odd, 0.0, -sin)


def kernel(x, c, ctx, c_ctx, w_mod, b_mod, norm1_w, norm2_w, w_in, q_norm_w, k_norm_w, w_attn_br,
           ssm_lambda_re, ssm_lambda_im, ssm_log_dt, ssm_b_re, ssm_b_im, ssm_c_re, ssm_c_im, ssm_d,
           w_glu, w_out, w_up, conv_w, conv_b, w_down, final_norm_w):
    b, l, d = x.shape
    lc = ctx.shape[1]
    assert w_mod.shape[0] == 1 and d == D_MODEL and l % TOK_TILE == 0 and lc % TOK_TILE == 0
    assert (l // CHUNK * b) % S5_STATE_ROWS == 0 and (l // CHUNK * b) % S5_OUT_ROWS == 0
    layer = 0

    pad = (-(b + 1)) % 16
    c_rows = jnp.concatenate([c, c_ctx[None, :], jnp.zeros((pad, d), F32)], axis=0)
    mod = _modulation(c_rows, w_mod[layer], b_mod[layer])
    sh1, sc1, g1, sh2, sc2, g2 = [m[:b, None, :] for m in jnp.split(mod, N_MOD, axis=-1)]
    csh1, csc1 = mod[b:b + 1, None, :d], mod[b:b + 1, None, d:2 * d]

    w_in_b = w_in[layer].astype(BF16)
    qn = (q_norm_w[layer] * (1.0 / math.sqrt(HEAD_DIM))).reshape(1, HEAD_DIM)
    kn = k_norm_w[layer].reshape(1, HEAD_DIM)
    n1 = norm1_w[layer].reshape(1, d)

    no_rope = (jnp.ones((lc, HEAD_DIM), F32),) + (jnp.zeros((lc, HEAD_DIM), F32),) * 2
    q, k, v, u4l, g = _inproj(x, sh1, sc1, n1, w_in_b, qn, kn, _rope_tables(l), latent=True)
    kc, vc, u4c = _inproj(ctx, csh1, csc1, n1, w_in_b, qn, kn, no_rope, latent=False)

    attn = _attention(q, kc, k, vc, v)

    disc = _s5_discretise(ssm_lambda_re[layer], ssm_lambda_im[layer], ssm_log_dt[layer],
                          ssm_b_re[layer], ssm_b_im[layer])
    m_tab, e_tab, f_tab, a_pow = _s5_tables(*disc, ssm_c_re[layer], ssm_c_im[layer])
    a_tab = a_pow.reshape(2, N_LANE_BLOCKS, 2, 1, STATE_W)
    d_tab = jnp.tile(ssm_d[layer].reshape(N_LANE_BLOCKS, 1, LANES), (1, CHUNK, 1))
    d_tab = d_tab.reshape(N_LANE_BLOCKS, 1, CHUNK_W)
    sin = _s5_states(u4c, u4l, e_tab, a_tab, b)
    y4 = _s5_outputs(u4l, sin, m_tab, f_tab, d_tab)

    x1, h2 = _merge(attn, y4, g, x, g1, sh2, sc2, norm2_w[layer].reshape(1, d),
                    w_attn_br[layer].astype(BF16), w_glu[layer].astype(BF16),
                    w_out[layer].astype(BF16))

    return _conv_ffn(h2, x1, g2, w_up[layer].astype(BF16), conv_w[layer],
                     conv_b[layer].reshape(1, 2 * D_FF), w_down[layer].astype(BF16),
                     final_norm_w.reshape(1, d))
```

```python
import functools
import math

import jax
import jax.numpy as jnp
import numpy as np
from jax import lax
from jax.experimental import pallas as pl
from jax.experimental.pallas import tpu as pltpu

F32 = jnp.float32
BF16 = jnp.bfloat16

D_MODEL = 1024
GRID_W = 64
N_HEADS = 8
N_KV_HEADS = 2
HEAD_DIM = 128
HEADS_PER_KV = N_HEADS // N_KV_HEADS
ATTN_DIM = N_HEADS * HEAD_DIM
KV_DIM = N_KV_HEADS * HEAD_DIM
ROPE_THETA = 10000.0
ROPE_AXIS_DIM = HEAD_DIM // 2
SSM_DIM = 512
SSM_GROUP = 16
SSM_STATE = 64
Q_END = ATTN_DIM
K_END = Q_END + KV_DIM
V_END = K_END + KV_DIM
U_END = V_END + SSM_DIM
D_FF = 2816
N_MOD = 6
EPS = 1e-6

LANES = 128
SUB = 8
CHUNK = 8
N_LANE_BLOCKS = SSM_DIM // LANES
GROUPS_PER_BLOCK = LANES // SSM_GROUP
CHUNK_W = CHUNK * LANES
STATE_W = GROUPS_PER_BLOCK * SSM_STATE

FF_CHUNK = 256
N_FF_CHUNKS = D_FF // FF_CHUNK
HALO = 16

VMEM_LIMIT = 56 * 1024 * 1024

MOD_COL_TILE = 1536
TOK_TILE = 64
TQ_ATTN = 1024
PROJ_SPLIT = 4
MERGE_SPLIT = 4
TM_FFN = 512
S5_STATE_ROWS = 1024
S5_STATE_PART = 512
S5_OUT_ROWS = 1024


def _sigmoid(x):
    return 1.0 / (1.0 + jnp.exp(-x))


def _gelu_tanh(x):
    return 0.5 * x * (1.0 + jnp.tanh(math.sqrt(2.0 / math.pi) * (x + 0.044715 * (x * x * x))))


def _rms(x):
    return x * lax.rsqrt(jnp.mean(x * x, axis=-1, keepdims=True) + EPS)


def _bdot(a, b):
    return jnp.dot(a, b, preferred_element_type=F32)


def _const_spec(shape):
    nd = len(shape)
    return pl.BlockSpec(shape, lambda *_: (0,) * nd)


def _params(n_axes):
    return pltpu.CompilerParams(dimension_semantics=("arbitrary",) * n_axes,
                                vmem_limit_bytes=VMEM_LIMIT)


def _split_bf16(v):
    hi = v.astype(BF16)
    return hi, (v - hi.astype(F32)).astype(BF16)


def _mod_kernel(c_ref, w_ref, b_ref, o_ref):
    c = c_ref[...]
    rows = c.shape[0]
    a_hi, a_lo = _split_bf16(c * _sigmoid(c))
    w_hi, w_lo = _split_bf16(w_ref[...])
    both = _bdot(jnp.concatenate([a_hi, a_lo], axis=0), w_hi)
    o_ref[...] = both[:rows] + both[rows:] + _bdot(a_hi, w_lo) + b_ref[...]


def _modulation(c_rows, w_mod, b_mod):
    rows = c_rows.shape[0]
    n = w_mod.shape[1]
    tn = MOD_COL_TILE
    return pl.pallas_call(
        _mod_kernel,
        grid=(n // tn,),
        in_specs=[pl.BlockSpec((rows, D_MODEL), lambda j: (0, 0)),
                  pl.BlockSpec((D_MODEL, tn), lambda j: (0, j)),
                  pl.BlockSpec((1, tn), lambda j: (0, j))],
        out_specs=pl.BlockSpec((rows, tn), lambda j: (0, j)),
        out_shape=jax.ShapeDtypeStruct((rows, n), F32),
        compiler_params=_params(1),
        name="modulation",
    )(c_rows, w_mod, b_mod.reshape(1, n))


def _norm_rope_store(t, nw, rope, out_ref, ts, col, nb):
    t = _rms(t) * nw
    if rope is not None:
        cos, sin_prev, sin_next = rope
        prev = pltpu.roll(t, 1, 1).reshape(nb, -1, HEAD_DIM)
        nxt = pltpu.roll(t, HEAD_DIM - 1, 1).reshape(nb, -1, HEAD_DIM)
        t3 = t.reshape(nb, -1, HEAD_DIM) * cos + prev * sin_prev + nxt * sin_next
    else:
        t3 = t.reshape(nb, -1, HEAD_DIM)
    out_ref[:, ts, col:col + HEAD_DIM] = t3.astype(out_ref.dtype)


def _chunk_row_select(nb, tokens, inverse=False):
    sel = np.zeros((nb * tokens, nb * tokens), np.float32)
    for b in range(nb):
        for t in range(tokens):
            k, s = divmod(t, CHUNK)
            sel[(s * (tokens // CHUNK) + k) * nb + b, b * tokens + t] = 1.0
    return jnp.asarray(sel.T if inverse else sel, dtype=BF16)


def _store_chunk_rows(u_perm, u4_ref, row0):
    rows = u_perm.shape[0] // CHUNK
    for j in range(N_LANE_BLOCKS):
        for s in range(CHUNK):
            u4_ref[j, row0:row0 + rows, s * LANES:(s + 1) * LANES] = (
                u_perm[s * rows:(s + 1) * rows, j * LANES:(j + 1) * LANES].astype(u4_ref.dtype))


def _inproj_kernel(x_ref, sh_ref, sc_ref, n1_ref, w_ref, sel_ref, qn_ref, kn_ref,
                   cos_ref, sinp_ref, sinn_ref, *refs, latent):
    nb, tokens_all, _ = x_ref.shape
    tokens = tokens_all // PROJ_SPLIT
    rows = nb * tokens
    if latent:
        q_ref, k_ref, v_ref, u4_ref, g_ref = refs
    else:
        k_ref, v_ref, u4_ref = refs

    def lhs(i):
        ts = slice(i * tokens, (i + 1) * tokens)
        h = _rms(x_ref[:, ts]) * n1_ref[...]
        h = h * (1.0 + sc_ref[...]) + sh_ref[...]
        return h.reshape(rows, D_MODEL).astype(BF16)

    def project(i, hb):
        ts = slice(i * tokens, (i + 1) * tokens)
        rope = (cos_ref[ts], sinp_ref[ts], sinn_ref[ts]) if latent else None
        if latent:
            q = _bdot(hb, w_ref[:, :Q_END])
            for hd in range(N_HEADS):
                _norm_rope_store(q[:, hd * HEAD_DIM:(hd + 1) * HEAD_DIM], qn_ref[...], rope,
                                 q_ref, ts, hd * HEAD_DIM, nb)
        k = _bdot(hb, w_ref[:, Q_END:K_END])
        for hd in range(N_KV_HEADS):
            _norm_rope_store(k[:, hd * HEAD_DIM:(hd + 1) * HEAD_DIM], kn_ref[...], rope,
                             k_ref, ts, hd * HEAD_DIM, nb)
        v_ref[:, ts] = _bdot(hb, w_ref[:, K_END:V_END]).reshape(nb, tokens, KV_DIM).astype(v_ref.dtype)
        hb_perm = _bdot(sel_ref[...], hb).astype(BF16)
        _store_chunk_rows(_bdot(hb_perm, w_ref[:, V_END:U_END]), u4_ref, i * (rows // CHUNK))
        if latent:
            g_ref[:, ts] = _bdot(hb, w_ref[:, U_END:]).reshape(
                nb, tokens, 2 * D_MODEL).astype(g_ref.dtype)

    cur = lhs(0)
    for i in range(PROJ_SPLIT):
        nxt = lhs(i + 1) if i + 1 < PROJ_SPLIT else None
        project(i, cur)
        cur = nxt


def _inproj(x, sh1, sc1, n1, w_in_b, qn, kn, rope, latent):
    b, l, _ = x.shape
    tt = TOK_TILE
    tok = lambda w: pl.BlockSpec((b, tt, w), lambda i: (0, i, 0))
    mod_spec = _const_spec(sh1.shape)
    rope_spec = pl.BlockSpec((tt, HEAD_DIM), lambda i: (i, 0))
    u4_spec = pl.BlockSpec((N_LANE_BLOCKS, tt // CHUNK * b, CHUNK_W), lambda i: (0, i, 0))
    kv_u_specs = [tok(KV_DIM), tok(KV_DIM), u4_spec]
    sel = _chunk_row_select(b, tt // PROJ_SPLIT)
    kv_u_shapes = [jax.ShapeDtypeStruct((b, l, KV_DIM), BF16),
                   jax.ShapeDtypeStruct((b, l, KV_DIM), BF16),
                   jax.ShapeDtypeStruct((N_LANE_BLOCKS, l // CHUNK * b, CHUNK_W), BF16)]
    if latent:
        out_specs = [tok(ATTN_DIM)] + kv_u_specs + [tok(2 * D_MODEL)]
        out_shape = ([jax.ShapeDtypeStruct((b, l, ATTN_DIM), BF16)] + kv_u_shapes
                     + [jax.ShapeDtypeStruct((b, l, 2 * D_MODEL), BF16)])
    else:
        out_specs, out_shape = kv_u_specs, kv_u_shapes
    return pl.pallas_call(
        functools.partial(_inproj_kernel, latent=latent),
        grid=(l // tt,),
        in_specs=[tok(D_MODEL), mod_spec, mod_spec, _const_spec((1, D_MODEL)),
                  _const_spec(w_in_b.shape), _const_spec(sel.shape),
                  _const_spec((1, HEAD_DIM)), _const_spec((1, HEAD_DIM)),
                  rope_spec, rope_spec, rope_spec],
        out_specs=out_specs,
        out_shape=out_shape,
        compiler_params=_params(1),
        name="inproj_latent" if latent else "inproj_ctx",
    )(x, sh1, sc1, n1, w_in_b, sel, qn, kn, *rope)


def _attn_kernel(q_ref, kc_ref, k_ref, vc_ref, v_ref, o_ref):
    keys = jnp.concatenate([kc_ref[0], k_ref[0]], axis=0)
    vals = jnp.concatenate([vc_ref[0], v_ref[0]], axis=0)
    vals = jnp.concatenate([vals, jnp.ones_like(vals)], axis=-1)
    nt = (((1,), (1,)), ((), ()))

    def scores(r):
        q = q_ref[0, :, r * HEAD_DIM:(r + 1) * HEAD_DIM]
        return lax.dot_general(q, keys, nt, preferred_element_type=F32)

    cur = scores(0)
    for r in range(HEADS_PER_KV):
        nxt = scores(r + 1) if r + 1 < HEADS_PER_KV else None
        p = jnp.exp(cur - jnp.max(cur, axis=-1, keepdims=True)).astype(BF16)
        o = _bdot(p, vals)
        o_ref[0, :, r * HEAD_DIM:(r + 1) * HEAD_DIM] = (
            o[:, :HEAD_DIM] / o[:, HEAD_DIM:]).astype(o_ref.dtype)
        cur = nxt


def _attention(q, kc, k, vc, v):
    b, l, _ = q.shape
    lc = kc.shape[1]
    tq = TQ_ATTN
    qspec = pl.BlockSpec((1, tq, HEADS_PER_KV * HEAD_DIM), lambda i, h, j: (i, j, h))
    kv = lambda n: pl.BlockSpec((1, n, HEAD_DIM), lambda i, h, j: (i, 0, h))
    return pl.pallas_call(
        _attn_kernel,
        grid=(b, N_KV_HEADS, l // tq),
        in_specs=[qspec, kv(lc), kv(l), kv(lc), kv(l)],
        out_specs=qspec,
        out_shape=jax.ShapeDtypeStruct((b, l, ATTN_DIM), BF16),
        compiler_params=_params(3),
        name="attention",
    )(q, kc, k, vc, v)


def _s5_discretise(lam_re, lam_im, log_dt, b_re, b_im):
    dt = jnp.exp(log_dt)[..., None]
    mag = jnp.exp(lam_re * dt)
    ang = lam_im * dt
    a_re = mag * jnp.cos(ang)
    a_im = mag * jnp.sin(ang)
    den = lam_re * lam_re + lam_im * lam_im
    nr = a_re - 1.0
    ni = a_im
    f_re = ((nr * lam_re + ni * lam_im) / den)[:, :, None, :]
    f_im = ((ni * lam_re - nr * lam_im) / den)[:, :, None, :]
    bt_re = b_re.transpose(0, 1, 3, 2)
    bt_im = b_im.transpose(0, 1, 3, 2)
    return (a_re[:, :, None, :], a_im[:, :, None, :],
            f_re * bt_re - f_im * bt_im, f_re * bt_im + f_im * bt_re)


def _s5_table_kernel(a_re_ref, a_im_ref, bb_re_ref, bb_im_ref, c_re_ref, c_im_ref,
                     m_ref, e_ref, f_ref, ap_ref, x_scr, e_scr, f_scr, c_scr):
    hp = lax.Precision.HIGHEST
    i32 = jnp.int32
    na, p_, t_ = GROUPS_PER_BLOCK, SSM_GROUP, CHUNK
    nt = (((1,), (1,)), ((), ()))
    rows = na * p_
    for d in range(2):
        a_re = jnp.broadcast_to(a_re_ref[d], (na, p_, SSM_STATE)).reshape(rows, SSM_STATE)
        a_im = jnp.broadcast_to(a_im_ref[d], (na, p_, SSM_STATE)).reshape(rows, SSM_STATE)
        b_re, b_im = bb_re_ref[d].reshape(rows, SSM_STATE), bb_im_ref[d].reshape(rows, SSM_STATE)
        c_re, c_im = c_re_ref[d].reshape(rows, SSM_STATE), c_im_ref[d].reshape(rows, SSM_STATE)
        c_scr[d, 0] = c_re
        c_scr[d, 1] = c_im
        pw_re, pw_im = jnp.ones_like(a_re), jnp.zeros_like(a_im)
        for tau in range(t_ + 1):
            if tau < t_:
                pb_re = b_re * pw_re - b_im * pw_im
                pb_im = b_re * pw_im + b_im * pw_re
                x_scr[d, 0, tau * rows:(tau + 1) * rows, :] = pb_re
                x_scr[d, 1, tau * rows:(tau + 1) * rows, :] = pb_im
                s_in = t_ - 1 - tau if d == 0 else tau
                e_scr[d, s_in * rows:(s_in + 1) * rows, 0:SSM_STATE] = pb_re
                e_scr[d, s_in * rows:(s_in + 1) * rows, SSM_STATE:] = pb_im
            if tau > 0:
                t_out = tau - 1 if d == 0 else t_ - tau
                f_scr[d, t_out * rows:(t_out + 1) * rows, 0:SSM_STATE] = c_re * pw_re - c_im * pw_im
                f_scr[d, t_out * rows:(t_out + 1) * rows, SSM_STATE:] = -(c_re * pw_im + c_im * pw_re)
            if tau == t_:
                ap_ref[d, 0, 0] = pw_re.reshape(na, p_, SSM_STATE)[:, 0:1, :]
                ap_ref[d, 0, 1] = pw_im.reshape(na, p_, SSM_STATE)[:, 0:1, :]
            else:
                pw_re, pw_im = pw_re * a_re - pw_im * a_im, pw_re * a_im + pw_im * a_re

    grp_bits = p_.bit_length() - 1
    st_bits = SSM_STATE.bit_length() - 1
    blk_bits = na.bit_length() - 1
    gmask = na - 1
    r_w = lax.broadcasted_iota(i32, (CHUNK_W, 1), 0)
    c_w = lax.broadcasted_iota(i32, (1, CHUNK_W), 1)
    c_l = lax.broadcasted_iota(i32, (1, LANES), 1)
    grp_tok_r = (r_w >> grp_bits) & gmask
    grp_tok_c = (c_w >> grp_bits) & gmask
    grp_st_r = (r_w >> st_bits) & gmask
    grp_st_c = (c_w >> st_bits) & gmask

    lag = []
    for d in range(2):
        full = (lax.dot_general(x_scr[d, 0], c_scr[d, 0], nt, precision=hp, preferred_element_type=F32)
                - lax.dot_general(x_scr[d, 1], c_scr[d, 1], nt, precision=hp, preferred_element_type=F32))
        lag.append(jnp.where(grp_tok_r == (c_l >> grp_bits), full, 0.0))
    for s in range(t_):
        for t in range(t_):
            if t > s:
                blk = lag[0][(t - s) * LANES:(t - s + 1) * LANES]
            elif s > t:
                blk = lag[1][(s - t) * LANES:(s - t + 1) * LANES]
            else:
                blk = lag[0][:LANES] + lag[1][:LANES]
            m_ref[0, s * LANES:(s + 1) * LANES, t * LANES:(t + 1) * LANES] = blk.astype(m_ref.dtype)

    r_l = lax.broadcasted_iota(i32, (LANES, 1), 0)
    rep = (((r_l >> st_bits) == (c_w >> (st_bits + blk_bits)))
           & ((r_l & (SSM_STATE - 1)) == (c_w & (SSM_STATE - 1)))).astype(F32).astype(BF16)
    rep_t = (((c_l >> st_bits) == (r_w >> (st_bits + blk_bits)))
             & ((c_l & (SSM_STATE - 1)) == (r_w & (SSM_STATE - 1)))).astype(F32).astype(BF16)
    for d in range(2):
        e_full = _bdot(e_scr[d].astype(BF16), rep)
        f_full = lax.dot_general(rep_t, f_scr[d].astype(BF16), nt, preferred_element_type=F32)
        e_ref[d, 0] = jnp.where(grp_tok_r == grp_st_c, e_full, 0.0).astype(e_ref.dtype)
        f_ref[d, 0] = jnp.where(grp_st_r == grp_tok_c, f_full, 0.0).astype(f_ref.dtype)


def _s5_tables(a_re, a_im, bb_re, bb_im, c_re, c_im):
    nj, na = N_LANE_BLOCKS, GROUPS_PER_BLOCK
    grp = lambda r: pl.BlockSpec((2, na, r, SSM_STATE), lambda j: (0, j, 0, 0))
    dj = lambda r, c: pl.BlockSpec((2, 1, r, c), lambda j: (0, j, 0, 0))
    src = lambda: pltpu.VMEM((2, 2, CHUNK_W, SSM_STATE), F32)
    pair = lambda: pltpu.VMEM((2, CHUNK_W, 2 * SSM_STATE), F32)
    return pl.pallas_call(
        _s5_table_kernel,
        grid=(nj,),
        in_specs=[grp(1), grp(1), grp(SSM_GROUP), grp(SSM_GROUP), grp(SSM_GROUP), grp(SSM_GROUP)],
        out_specs=[pl.BlockSpec((1, CHUNK_W, CHUNK_W), lambda j: (j, 0, 0)),
                   dj(CHUNK_W, 2 * STATE_W), dj(2 * STATE_W, CHUNK_W),
                   pl.BlockSpec((2, 1, 2, na, 1, SSM_STATE), lambda j: (0, j, 0, 0, 0, 0))],
        out_shape=[jax.ShapeDtypeStruct((nj, CHUNK_W, CHUNK_W), BF16),
                   jax.ShapeDtypeStruct((2, nj, CHUNK_W, 2 * STATE_W), BF16),
                   jax.ShapeDtypeStruct((2, nj, 2 * STATE_W, CHUNK_W), BF16),
                   jax.ShapeDtypeStruct((2, nj, 2, na, 1, SSM_STATE), F32)],
        scratch_shapes=[src(), pair(), pair(), pltpu.VMEM((2, 2, LANES, SSM_STATE), F32)],
        compiler_params=_params(1),
        name="s5_tables",
    )(a_re, a_im, bb_re, bb_im, c_re, c_im)


def _s5_state_kernel(uc_ref, ul_ref, e_ref, a_ref, sin_ref, sre_scr, sim_scr, *, nb):
    d = pl.program_id(1)
    r = pl.program_id(2)
    a_re = jnp.broadcast_to(a_ref[0, 0, 0], (nb, STATE_W))
    a_im = jnp.broadcast_to(a_ref[0, 0, 1], (nb, STATE_W))

    def run(u_ref, reverse, store):
        rows = u_ref.shape[1]
        part = min(rows, S5_STATE_PART)
        parts = list(range(rows // part))
        chunks = list(range(part // nb))
        if reverse:
            parts, chunks = parts[::-1], chunks[::-1]
        inj = [_bdot(u_ref[0, p * part:(p + 1) * part], e_ref[0, 0]) for p in parts]
        s_re, s_im = sre_scr[...], sim_scr[...]
        for c, p in zip(inj, parts):
            for k in chunks:
                if store:
                    row = p * part + k * nb
                    sin_ref[0, 0, row:row + nb, 0:STATE_W] = s_re.astype(sin_ref.dtype)
                    sin_ref[0, 0, row:row + nb, STATE_W:] = s_im.astype(sin_ref.dtype)
                c_re = c[k * nb:(k + 1) * nb, 0:STATE_W]
                c_im = c[k * nb:(k + 1) * nb, STATE_W:]
                s_re, s_im = s_re * a_re - s_im * a_im + c_re, s_re * a_im + s_im * a_re + c_im
        sre_scr[...] = s_re
        sim_scr[...] = s_im

    @pl.when(r == 0)
    def _():
        sre_scr[...] = jnp.zeros_like(sre_scr)
        sim_scr[...] = jnp.zeros_like(sim_scr)

    for reverse in (False, True):
        @pl.when((r == 0) & (d == int(reverse)))
        def _():
            run(uc_ref, reverse, False)

        @pl.when((r > 0) & (d == int(reverse)))
        def _():
            run(ul_ref, reverse, True)


def _s5_states(u4c, u4l, e_tab, a_tab, nb):
    rows_c = u4c.shape[1]
    rows = S5_STATE_ROWS
    n_tiles = u4l.shape[1] // rows

    def lat_tile(d, r):
        t = jnp.maximum(r - 1, 0)
        return jnp.where(d == 0, t, n_tiles - 1 - t)

    return pl.pallas_call(
        functools.partial(_s5_state_kernel, nb=nb),
        grid=(N_LANE_BLOCKS, 2, n_tiles + 1),
        in_specs=[pl.BlockSpec((1, rows_c, CHUNK_W), lambda j, d, r: (j, 0, 0)),
                  pl.BlockSpec((1, rows, CHUNK_W), lambda j, d, r: (j, lat_tile(d, r), 0)),
                  pl.BlockSpec((1, 1, CHUNK_W, 2 * STATE_W), lambda j, d, r: (d, j, 0, 0)),
                  pl.BlockSpec((1, 1, 2, 1, STATE_W), lambda j, d, r: (d, j, 0, 0, 0))],
        out_specs=pl.BlockSpec((1, 1, rows, 2 * STATE_W), lambda j, d, r: (d, j, lat_tile(d, r), 0)),
        out_shape=jax.ShapeDtypeStruct((2, N_LANE_BLOCKS, u4l.shape[1], 2 * STATE_W), BF16),
        scratch_shapes=[pltpu.VMEM((nb, STATE_W), F32), pltpu.VMEM((nb, STATE_W), F32)],
        compiler_params=_params(3),
        name="s5_states",
    )(u4c, u4l, e_tab, a_tab)


def _s5_out_kernel(ul_ref, sin_ref, m_ref, f_ref, d_ref, y_ref):
    u = ul_ref[0]
    y_ref[0] = (u.astype(F32) * d_ref[0] + _bdot(u, m_ref[0])
                + _bdot(sin_ref[0, 0], f_ref[0, 0]) + _bdot(sin_ref[1, 0], f_ref[1, 0]))


def _s5_outputs(u4l, sin, m_tab, f_tab, d_tab):
    rows = S5_OUT_ROWS
    return pl.pallas_call(
        _s5_out_kernel,
        grid=(N_LANE_BLOCKS, u4l.shape[1] // rows),
        in_specs=[pl.BlockSpec((1, rows, CHUNK_W), lambda j, r: (j, r, 0)),
                  pl.BlockSpec((2, 1, rows, 2 * STATE_W), lambda j, r: (0, j, r, 0)),
                  pl.BlockSpec((1, CHUNK_W, CHUNK_W), lambda j, r: (j, 0, 0)),
                  pl.BlockSpec((2, 1, 2 * STATE_W, CHUNK_W), lambda j, r: (0, j, 0, 0)),
                  pl.BlockSpec((1, 1, CHUNK_W), lambda j, r: (j, 0, 0))],
        out_specs=pl.BlockSpec((1, rows, CHUNK_W), lambda j, r: (j, r, 0)),
        out_shape=jax.ShapeDtypeStruct(u4l.shape, F32),
        compiler_params=_params(2),
        name="s5_outputs",
    )(u4l, sin, m_tab, f_tab, d_tab)


def _merge_kernel(attn_ref, y4_ref, g_ref, x_ref, g1_ref, sh2_ref, sc2_ref, n2_ref,
                  sel_ref, wab_ref, wglu_ref, wout_ref, x1_ref, h2_ref):
    nb, tokens_all, _ = x_ref.shape
    tokens = tokens_all // MERGE_SPLIT
    rows = nb * tokens
    crow = tokens // CHUNK * nb

    def branch_dots(i):
        ts = slice(i * tokens, (i + 1) * tokens)
        act = [_gelu_tanh(y4_ref[j, i * crow:(i + 1) * crow]).astype(BF16)
               for j in range(N_LANE_BLOCKS)]
        act = jnp.concatenate(
            [jnp.concatenate([act[j][:, s * LANES:(s + 1) * LANES] for s in range(CHUNK)], axis=0)
             for j in range(N_LANE_BLOCKS)], axis=-1)
        act = _bdot(sel_ref[...], act).astype(BF16)
        p_attn = _bdot(attn_ref[:, ts].reshape(rows, ATTN_DIM), wab_ref[...])
        return p_attn, _bdot(act, wglu_ref[...])

    def mix_dot(i, p_attn, glu):
        ts = slice(i * tokens, (i + 1) * tokens)
        p_ssm = glu[:, :D_MODEL] * _sigmoid(glu[:, D_MODEL:])
        g = g_ref[:, ts].reshape(rows, 2 * D_MODEL).astype(F32)
        mix = _sigmoid(g[:, :D_MODEL]) * p_attn + _sigmoid(g[:, D_MODEL:]) * p_ssm
        return _bdot(mix.astype(BF16), wout_ref[...]).reshape(nb, tokens, D_MODEL)

    def finish(i, x_mix):
        ts = slice(i * tokens, (i + 1) * tokens)
        x1 = x_ref[:, ts] + g1_ref[...] * x_mix
        x1_ref[:, ts] = x1
        h2 = _rms(x1) * n2_ref[...]
        h2_ref[:, ts] = (h2 * (1.0 + sc2_ref[...]) + sh2_ref[...]).astype(h2_ref.dtype)

    dots = [None] * MERGE_SPLIT
    mixed = [None] * MERGE_SPLIT
    dots[0] = branch_dots(0)
    for i in range(MERGE_SPLIT):
        if i + 1 < MERGE_SPLIT:
            dots[i + 1] = branch_dots(i + 1)
        mixed[i] = mix_dot(i, *dots[i])
        if i > 0:
            finish(i - 1, mixed[i - 1])
    finish(MERGE_SPLIT - 1, mixed[MERGE_SPLIT - 1])


def _merge(attn, y4, g, x, g1, sh2, sc2, n2, wab, wglu, wout):
    b, l, _ = x.shape
    tt = TOK_TILE
    tok = lambda w: pl.BlockSpec((b, tt, w), lambda i: (0, i, 0))
    per_b = _const_spec((b, 1, D_MODEL))
    y4_spec = pl.BlockSpec((N_LANE_BLOCKS, tt // CHUNK * b, CHUNK_W), lambda i: (0, i, 0))
    sel = _chunk_row_select(b, tt // MERGE_SPLIT, inverse=True)
    return pl.pallas_call(
        _merge_kernel,
        grid=(l // tt,),
        in_specs=[tok(ATTN_DIM), y4_spec, tok(2 * D_MODEL), tok(D_MODEL),
                  per_b, per_b, per_b, _const_spec((1, D_MODEL)), _const_spec(sel.shape),
                  _const_spec(wab.shape), _const_spec(wglu.shape), _const_spec(wout.shape)],
        out_specs=[tok(D_MODEL), tok(D_MODEL)],
        out_shape=[jax.ShapeDtypeStruct((b, l, D_MODEL), F32),
                   jax.ShapeDtypeStruct((b, l, D_MODEL), BF16)],
        compiler_params=_params(1),
        name="merge",
    )(attn, y4, g, x, g1, sh2, sc2, n2, sel, wab, wglu, wout)


def _ffn_row_select(tm):
    seg_len = tm // SUB
    tile = np.zeros((tm, tm), np.float32)
    for seg in range(SUB):
        for i in range(seg_len):
            tile[SUB * i + seg, seg * seg_len + i] = 1.0
    halo = np.zeros((SUB, 2 * HALO), np.float32)
    halo[0, HALO - 1] = 1.0
    halo[1, HALO] = 1.0
    return jnp.asarray(tile, dtype=BF16), jnp.asarray(halo, dtype=BF16)


def _ffn_kernel(sel_ref, selh_ref, hp_ref, h_ref, hn_ref, x1_ref, g2_ref, wup_ref, cw_ref, cb_ref,
                wd_ref, fw_ref, o_ref, acc_scr, *, tm):
    j = pl.program_id(1)
    keep_prev = (j > 0).astype(F32)
    keep_next = (j < pl.num_programs(1) - 1).astype(F32)
    halo = jnp.concatenate([hp_ref[0], hn_ref[0]], axis=0)
    lhs = jnp.concatenate([_bdot(selh_ref[...], halo), _bdot(sel_ref[...], h_ref[0])],
                          axis=0).astype(BF16)
    sub = lax.broadcasted_iota(jnp.int32, (SUB, 1), 0)

    def up(f):
        return (_bdot(lhs, wup_ref[:, f * FF_CHUNK:(f + 1) * FF_CHUNK]),
                _bdot(lhs, wup_ref[:, D_FF + f * FF_CHUNK:D_FF + (f + 1) * FF_CHUNK]))

    def conv(zall, col):
        z = zall[SUB:]
        before = jnp.where(sub == 0, zall[0:1] * keep_prev, pltpu.roll(z[tm - SUB:], 1, 0))
        after = jnp.where(sub == SUB - 1, zall[1:2] * keep_next, pltpu.roll(z[:SUB], SUB - 1, 0))
        z_prev = jnp.concatenate([before, z[:tm - SUB]], axis=0)
        z_next = jnp.concatenate([z[SUB:], after], axis=0)
        cw = cw_ref[:, col:col + FF_CHUNK]
        return z_prev * cw[0:1] + z * cw[1:2] + z_next * cw[2:3] + cb_ref[:, col:col + FF_CHUNK]

    acts = []
    cur = up(0)
    for f in range(N_FF_CHUNKS):
        nxt = up(f + 1) if f + 1 < N_FF_CHUNKS else None
        val = conv(cur[0], f * FF_CHUNK)
        gate = conv(cur[1], D_FF + f * FF_CHUNK)
        acts.append((gate * _sigmoid(gate) * val).astype(BF16))
        cur = nxt
    acc = _bdot(jnp.concatenate(acts, axis=-1), wd_ref[...])
    seg_len = tm // SUB
    for k in range(D_MODEL // LANES):
        acc_scr[k] = acc[:, k * LANES:(k + 1) * LANES]
        for seg in range(SUB):
            o_ref[0, seg * seg_len:(seg + 1) * seg_len, k * LANES:(k + 1) * LANES] = (
                acc_scr[k, pl.ds(seg, seg_len, stride=SUB), :])
    x2 = x1_ref[0] + g2_ref[0] * o_ref[0]
    o_ref[0] = _rms(x2) * fw_ref[...]


def _conv_ffn(h2, x1, g2, wup, cw, cb, wd, fw):
    b, l, _ = x1.shape
    tm = TM_FFN
    nh = tm // HALO
    last = l // HALO - 1
    tok = lambda: pl.BlockSpec((1, tm, D_MODEL), lambda i, j: (i, j, 0))
    prev = pl.BlockSpec((1, HALO, D_MODEL), lambda i, j: (i, jnp.maximum(j * nh - 1, 0), 0))
    nxt = pl.BlockSpec((1, HALO, D_MODEL), lambda i, j: (i, jnp.minimum((j + 1) * nh, last), 0))
    per_b = pl.BlockSpec((1, 1, D_MODEL), lambda i, j: (i, 0, 0))
    once = lambda shape: pl.BlockSpec(shape, lambda *_: (0,) * len(shape),
                                      pipeline_mode=pl.Buffered(1))
    sel, selh = _ffn_row_select(tm)
    return pl.pallas_call(
        functools.partial(_ffn_kernel, tm=tm),
        grid=(b, l // tm),
        in_specs=[once(sel.shape), once(selh.shape), prev, tok(), nxt, tok(), per_b,
                  once(wup.shape), once(cw.shape), once(cb.shape), once(wd.shape),
                  _const_spec((1, D_MODEL))],
        out_specs=tok(),
        out_shape=jax.ShapeDtypeStruct((b, l, D_MODEL), F32),
        scratch_shapes=[pltpu.VMEM((D_MODEL // LANES, tm, LANES), F32)],
        compiler_params=_params(2),
        name="conv_ffn",
    )(sel, selh, h2, h2, h2, x1, g2, wup, cw, cb, wd, fw)


def _rope_tables(l):
    rows = jnp.repeat(jnp.arange(l // GRID_W, dtype=F32), GRID_W)
    cols = jnp.tile(jnp.arange(GRID_W, dtype=F32), l // GRID_W)
    inv_freq = ROPE_THETA ** (-jnp.arange(0, ROPE_AXIS_DIM, 2, dtype=F32) / ROPE_AXIS_DIM)
    ang = jnp.concatenate([rows[:, None] * inv_freq, cols[:, None] * inv_freq], axis=-1)
    cos = jnp.repeat(jnp.cos(ang), 2, axis=-1)
    sin = jnp.repeat(jnp.sin(ang), 2, axis=-1)
    odd = (jnp.arange(HEAD_DIM) % 2 == 1)[None, :]
    return cos, jnp.where(odd, sin, 0.0), jnp.where(odd, 0.0, -sin)


def kernel(x, c, ctx, c_ctx, w_mod, b_mod, norm1_w, norm2_w, w_in, q_norm_w, k_norm_w, w_attn_br,
           ssm_lambda_re, ssm_lambda_im, ssm_log_dt, ssm_b_re, ssm_b_im, ssm_c_re, ssm_c_im, ssm_d,
           w_glu, w_out, w_up, conv_w, conv_b, w_down, final_norm_w):
    b, l, d = x.shape
    lc = ctx.shape[1]
    assert w_mod.shape[0] == 1 and d == D_MODEL and l % TOK_TILE == 0 and lc % TOK_TILE == 0
    assert (l // CHUNK * b) % S5_STATE_ROWS == 0 and (l // CHUNK * b) % S5_OUT_ROWS == 0
    layer = 0

    pad = (-(b + 1)) % 16
    c_rows = jnp.concatenate([c, c_ctx[None, :], jnp.zeros((pad, d), F32)], axis=0)
    mod = _modulation(c_rows, w_mod[layer], b_mod[layer])
    sh1, sc1, g1, sh2, sc2, g2 = [m[:b, None, :] for m in jnp.split(mod, N_MOD, axis=-1)]
    csh1, csc1 = mod[b:b + 1, None, :d], mod[b:b + 1, None, d:2 * d]

    w_in_b = w_in[layer].astype(BF16)
    qn = (q_norm_w[layer] * (1.0 / math.sqrt(HEAD_DIM))).reshape(1, HEAD_DIM)
    kn = k_norm_w[layer].reshape(1, HEAD_DIM)
    n1 = norm1_w[layer].reshape(1, d)

    no_rope = (jnp.ones((lc, HEAD_DIM), F32),) + (jnp.zeros((lc, HEAD_DIM), F32),) * 2
    q, k, v, u4l, g = _inproj(x, sh1, sc1, n1, w_in_b, qn, kn, _rope_tables(l), latent=True)
    kc, vc, u4c = _inproj(ctx, csh1, csc1, n1, w_in_b, qn, kn, no_rope, latent=False)

    attn = _attention(q, kc, k, vc, v)

    disc = _s5_discretise(ssm_lambda_re[layer], ssm_lambda_im[layer], ssm_log_dt[layer],
                          ssm_b_re[layer], ssm_b_im[layer])
    m_tab, e_tab, f_tab, a_pow = _s5_tables(*disc, ssm_c_re[layer], ssm_c_im[layer])
    a_tab = a_pow.reshape(2, N_LANE_BLOCKS, 2, 1, STATE_W)
    d_tab = jnp.tile(ssm_d[layer].reshape(N_LANE_BLOCKS, 1, LANES), (1, CHUNK, 1))
    d_tab = d_tab.reshape(N_LANE_BLOCKS, 1, CHUNK_W)
    sin = _s5_states(u4c, u4l, e_tab, a_tab, b)
    y4 = _s5_outputs(u4l, sin, m_tab, f_tab, d_tab)

    x1, h2 = _merge(attn, y4, g, x, g1, sh2, sc2, norm2_w[layer].reshape(1, d),
                    w_attn_br[layer].astype(BF16), w_glu[layer].astype(BF16),
                    w_out[layer].astype(BF16))

    return _conv_ffn(h2, x1, g2, w_up[layer].astype(BF16), conv_w[layer],
                     conv_b[layer].reshape(1, 2 * D_FF), w_down[layer].astype(BF16),
                     final_norm_w.reshape(1, d))
```

```python
import functools
import math

import jax
import jax.numpy as jnp
import numpy as np
from jax import lax
from jax.experimental import pallas as pl
from jax.experimental.pallas import tpu as pltpu

F32 = jnp.float32
BF16 = jnp.bfloat16

D_MODEL = 1024
GRID_W = 64
N_HEADS = 8
N_KV_HEADS = 2
HEAD_DIM = 128
HEADS_PER_KV = N_HEADS // N_KV_HEADS
ATTN_DIM = N_HEADS * HEAD_DIM
KV_DIM = N_KV_HEADS * HEAD_DIM
ROPE_THETA = 10000.0
ROPE_AXIS_DIM = HEAD_DIM // 2
SSM_DIM = 512
SSM_GROUP = 16
SSM_STATE = 64
Q_END = ATTN_DIM
K_END = Q_END + KV_DIM
V_END = K_END + KV_DIM
U_END = V_END + SSM_DIM
D_FF = 2816
N_MOD = 6
EPS = 1e-6

LANES = 128
SUB = 8
CHUNK = 8
N_LANE_BLOCKS = SSM_DIM // LANES
GROUPS_PER_BLOCK = LANES // SSM_GROUP
CHUNK_W = CHUNK * LANES
STATE_W = GROUPS_PER_BLOCK * SSM_STATE

FF_CHUNK = 256
N_FF_CHUNKS = D_FF // FF_CHUNK
HALO = 16

VMEM_LIMIT = 56 * 1024 * 1024

MOD_COL_TILE = 1536
TOK_TILE = 64
TQ_ATTN = 1024
PROJ_SPLIT = 4
MERGE_SPLIT = 4
TM_FFN = 512
S5_STATE_ROWS = 1024
S5_STATE_PART = 512
S5_OUT_ROWS = 1024


def _sigmoid(x):
    return 1.0 / (1.0 + jnp.exp(-x))


def _gelu_tanh(x):
    return 0.5 * x * (1.0 + jnp.tanh(math.sqrt(2.0 / math.pi) * (x + 0.044715 * (x * x * x))))


def _rms(x):
    return x * lax.rsqrt(jnp.mean(x * x, axis=-1, keepdims=True) + EPS)


def _bdot(a, b):
    return jnp.dot(a, b, preferred_element_type=F32)


def _const_spec(shape):
    nd = len(shape)
    return pl.BlockSpec(shape, lambda *_: (0,) * nd)


def _params(n_axes):
    return pltpu.CompilerParams(dimension_semantics=("arbitrary",) * n_axes,
                                vmem_limit_bytes=VMEM_LIMIT)


def _split_bf16(v):
    hi = v.astype(BF16)
    return hi, (v - hi.astype(F32)).astype(BF16)


def _mod_kernel(c_ref, w_ref, b_ref, o_ref):
    c = c_ref[...]
    rows = c.shape[0]
    a_hi, a_lo = _split_bf16(c * _sigmoid(c))
    w_hi, w_lo = _split_bf16(w_ref[...])
    both = _bdot(jnp.concatenate([a_hi, a_lo], axis=0), w_hi)
    o_ref[...] = both[:rows] + both[rows:] + _bdot(a_hi, w_lo) + b_ref[...]


def _modulation(c_rows, w_mod, b_mod):
    rows = c_rows.shape[0]
    n = w_mod.shape[1]
    tn = MOD_COL_TILE
    return pl.pallas_call(
        _mod_kernel,
        grid=(n // tn,),
        in_specs=[pl.BlockSpec((rows, D_MODEL), lambda j: (0, 0)),
                  pl.BlockSpec((D_MODEL, tn), lambda j: (0, j)),
                  pl.BlockSpec((1, tn), lambda j: (0, j))],
        out_specs=pl.BlockSpec((rows, tn), lambda j: (0, j)),
        out_shape=jax.ShapeDtypeStruct((rows, n), F32),
        compiler_params=_params(1),
        name="modulation",
    )(c_rows, w_mod, b_mod.reshape(1, n))


def _norm_rope_store(t, nw, rope, out_ref, ts, col, nb):
    t = _rms(t) * nw
    if rope is not None:
        cos, sin_prev, sin_next = rope
        prev = pltpu.roll(t, 1, 1).reshape(nb, -1, HEAD_DIM)
        nxt = pltpu.roll(t, HEAD_DIM - 1, 1).reshape(nb, -1, HEAD_DIM)
        t3 = t.reshape(nb, -1, HEAD_DIM) * cos + prev * sin_prev + nxt * sin_next
    else:
        t3 = t.reshape(nb, -1, HEAD_DIM)
    out_ref[:, ts, col:col + HEAD_DIM] = t3.astype(out_ref.dtype)


def _chunk_row_select(nb, tokens, inverse=False):
    sel = np.zeros((nb * tokens, nb * tokens), np.float32)
    for b in range(nb):
        for t in range(tokens):
            k, s = divmod(t, CHUNK)
            sel[(s * (tokens // CHUNK) + k) * nb + b, b * tokens + t] = 1.0
    return jnp.asarray(sel.T if inverse else sel, dtype=BF16)


def _store_chunk_rows(u_perm, u4_ref, row0):
    rows = u_perm.shape[0] // CHUNK
    for j in range(N_LANE_BLOCKS):
        for s in range(CHUNK):
            u4_ref[j, row0:row0 + rows, s * LANES:(s + 1) * LANES] = (
                u_perm[s * rows:(s + 1) * rows, j * LANES:(j + 1) * LANES].astype(u4_ref.dtype))


def _inproj_kernel(x_ref, sh_ref, sc_ref, n1_ref, w_ref, sel_ref, qn_ref, kn_ref,
                   cos_ref, sinp_ref, sinn_ref, *refs, latent):
    nb, tokens_all, _ = x_ref.shape
    tokens = tokens_all // PROJ_SPLIT
    rows = nb * tokens
    if latent:
        q_ref, k_ref, v_ref, u4_ref, g_ref = refs
    else:
        k_ref, v_ref, u4_ref = refs

    def lhs(i):
        ts = slice(i * tokens, (i + 1) * tokens)
        h = _rms(x_ref[:, ts]) * n1_ref[...]
        h = h * (1.0 + sc_ref[...]) + sh_ref[...]
        return h.reshape(rows, D_MODEL).astype(BF16)

    def project(i, hb):
        ts = slice(i * tokens, (i + 1) * tokens)
        rope = (cos_ref[ts], sinp_ref[ts], sinn_ref[ts]) if latent else None
        if latent:
            q = _bdot(hb, w_ref[:, :Q_END])
            for hd in range(N_HEADS):
                _norm_rope_store(q[:, hd * HEAD_DIM:(hd + 1) * HEAD_DIM], qn_ref[...], rope,
                                 q_ref, ts, hd * HEAD_DIM, nb)
        k = _bdot(hb, w_ref[:, Q_END:K_END])
        for hd in range(N_KV_HEADS):
            _norm_rope_store(k[:, hd * HEAD_DIM:(hd + 1) * HEAD_DIM], kn_ref[...], rope,
                             k_ref, ts, hd * HEAD_DIM, nb)
        v_ref[:, ts] = _bdot(hb, w_ref[:, K_END:V_END]).reshape(nb, tokens, KV_DIM).astype(v_ref.dtype)
        hb_perm = _bdot(sel_ref[...], hb).astype(BF16)
        _store_chunk_rows(_bdot(hb_perm, w_ref[:, V_END:U_END]), u4_ref, i * (rows // CHUNK))
        if latent:
            g_ref[:, ts] = _bdot(hb, w_ref[:, U_END:]).reshape(
                nb, tokens, 2 * D_MODEL).astype(g_ref.dtype)

    cur = lhs(0)
    for i in range(PROJ_SPLIT):
        nxt = lhs(i + 1) if i + 1 < PROJ_SPLIT else None
        project(i, cur)
        cur = nxt


def _inproj(x, sh1, sc1, n1, w_in_b, qn, kn, rope, latent):
    b, l, _ = x.shape
    tt = TOK_TILE
    tok = lambda w: pl.BlockSpec((b, tt, w), lambda i: (0, i, 0))
    mod_spec = _const_spec(sh1.shape)
    rope_spec = pl.BlockSpec((tt, HEAD_DIM), lambda i: (i, 0))
    u4_spec = pl.BlockSpec((N_LANE_BLOCKS, tt // CHUNK * b, CHUNK_W), lambda i: (0, i, 0))
    kv_u_specs = [tok(KV_DIM), tok(KV_DIM), u4_spec]
    sel = _chunk_row_select(b, tt // PROJ_SPLIT)
    kv_u_shapes = [jax.ShapeDtypeStruct((b, l, KV_DIM), BF16),
                   jax.ShapeDtypeStruct((b, l, KV_DIM), BF16),
                   jax.ShapeDtypeStruct((N_LANE_BLOCKS, l // CHUNK * b, CHUNK_W), BF16)]
    if latent:
        out_specs = [tok(ATTN_DIM)] + kv_u_specs + [tok(2 * D_MODEL)]
        out_shape = ([jax.ShapeDtypeStruct((b, l, ATTN_DIM), BF16)] + kv_u_shapes
                     + [jax.ShapeDtypeStruct((b, l, 2 * D_MODEL), BF16)])
    else:
        out_specs, out_shape = kv_u_specs, kv_u_shapes
    return pl.pallas_call(
        functools.partial(_inproj_kernel, latent=latent),
        grid=(l // tt,),
        in_specs=[tok(D_MODEL), mod_spec, mod_spec, _const_spec((1, D_MODEL)),
                  _const_spec(w_in_b.shape), _const_spec(sel.shape),
                  _const_spec((1, HEAD_DIM)), _const_spec((1, HEAD_DIM)),
                  rope_spec, rope_spec, rope_spec],
        out_specs=out_specs,
        out_shape=out_shape,
        compiler_params=_params(1),
        name="inproj_latent" if latent else "inproj_ctx",
    )(x, sh1, sc1, n1, w_in_b, sel, qn, kn, *rope)


def _attn_kernel(q_ref, kc_ref, k_ref, vc_ref, v_ref, o_ref):
    keys = jnp.concatenate([kc_ref[0], k_ref[0]], axis=0)
    vals = jnp.concatenate([vc_ref[0], v_ref[0]], axis=0)
    vals = jnp.concatenate([vals, jnp.ones_like(vals)], axis=-1)
    nt = (((1,), (1,)), ((), ()))

    def scores(r):
        q = q_ref[0, :, r * HEAD_DIM:(r + 1) * HEAD_DIM]
        return lax.dot_general(q, keys, nt, preferred_element_type=F32)

    cur = scores(0)
    for r in range(HEADS_PER_KV):
        nxt = scores(r + 1) if r + 1 < HEADS_PER_KV else None
        p = jnp.exp(cur - jnp.max(cur, axis=-1, keepdims=True)).astype(BF16)
        o = _bdot(p, vals)
        o_ref[0, :, r * HEAD_DIM:(r + 1) * HEAD_DIM] = (
            o[:, :HEAD_DIM] / o[:, HEAD_DIM:]).astype(o_ref.dtype)
        cur = nxt


def _attention(q, kc, k, vc, v):
    b, l, _ = q.shape
    lc = kc.shape[1]
    tq = TQ_ATTN
    qspec = pl.BlockSpec((1, tq, HEADS_PER_KV * HEAD_DIM), lambda i, h, j: (i, j, h))
    kv = lambda n: pl.BlockSpec((1, n, HEAD_DIM), lambda i, h, j: (i, 0, h))
    return pl.pallas_call(
        _attn_kernel,
        grid=(b, N_KV_HEADS, l // tq),
        in_specs=[qspec, kv(lc), kv(l), kv(lc), kv(l)],
        out_specs=qspec,
        out_shape=jax.ShapeDtypeStruct((b, l, ATTN_DIM), BF16),
        compiler_params=_params(3),
        name="attention",
    )(q, kc, k, vc, v)


def _s5_discretise(lam_re, lam_im, log_dt, b_re, b_im):
    dt = jnp.exp(log_dt)[..., None]
    mag = jnp.exp(lam_re * dt)
    ang = lam_im * dt
    a_re = mag * jnp.cos(ang)
    a_im = mag * jnp.sin(ang)
    den = lam_re * lam_re + lam_im * lam_im
    nr = a_re - 1.0
    ni = a_im
    f_re = ((nr * lam_re + ni * lam_im) / den)[:, :, None, :]
    f_im = ((ni * lam_re - nr * lam_im) / den)[:, :, None, :]
    bt_re = b_re.transpose(0, 1, 3, 2)
    bt_im = b_im.transpose(0, 1, 3, 2)
    return (a_re[:, :, None, :], a_im[:, :, None, :],
            f_re * bt_re - f_im * bt_im, f_re * bt_im + f_im * bt_re)


def _s5_table_kernel(a_re_ref, a_im_ref, bb_re_ref, bb_im_ref, c_re_ref, c_im_ref,
                     mf_ref, e_ref, ap_ref, x_scr, e_scr, f_scr, c_scr):
    hp = lax.Precision.HIGHEST
    i32 = jnp.int32
    na, p_, t_ = GROUPS_PER_BLOCK, SSM_GROUP, CHUNK
    nt = (((1,), (1,)), ((), ()))
    rows = na * p_
    for d in range(2):
        a_re = jnp.broadcast_to(a_re_ref[d], (na, p_, SSM_STATE)).reshape(rows, SSM_STATE)
        a_im = jnp.broadcast_to(a_im_ref[d], (na, p_, SSM_STATE)).reshape(rows, SSM_STATE)
        b_re, b_im = bb_re_ref[d].reshape(rows, SSM_STATE), bb_im_ref[d].reshape(rows, SSM_STATE)
        c_re, c_im = c_re_ref[d].reshape(rows, SSM_STATE), c_im_ref[d].reshape(rows, SSM_STATE)
        c_scr[d, 0] = c_re
        c_scr[d, 1] = c_im
        pw_re, pw_im = jnp.ones_like(a_re), jnp.zeros_like(a_im)
        for tau in range(t_ + 1):
            if tau < t_:
                pb_re = b_re * pw_re - b_im * pw_im
                pb_im = b_re * pw_im + b_im * pw_re
                x_scr[d, 0, tau * rows:(tau + 1) * rows, :] = pb_re
                x_scr[d, 1, tau * rows:(tau + 1) * rows, :] = pb_im
                s_in = t_ - 1 - tau if d == 0 else tau
                e_scr[d, s_in * rows:(s_in + 1) * rows, 0:SSM_STATE] = pb_re
                e_scr[d, s_in * rows:(s_in + 1) * rows, SSM_STATE:] = pb_im
            if tau > 0:
                t_out = tau - 1 if d == 0 else t_ - tau
                f_scr[d, t_out * rows:(t_out + 1) * rows, 0:SSM_STATE] = c_re * pw_re - c_im * pw_im
                f_scr[d, t_out * rows:(t_out + 1) * rows, SSM_STATE:] = -(c_re * pw_im + c_im * pw_re)
            if tau == t_:
                ap_ref[d, 0, 0] = pw_re.reshape(na, p_, SSM_STATE)[:, 0:1, :]
                ap_ref[d, 0, 1] = pw_im.reshape(na, p_, SSM_STATE)[:, 0:1, :]
            else:
                pw_re, pw_im = pw_re * a_re - pw_im * a_im, pw_re * a_im + pw_im * a_re

    grp_bits = p_.bit_length() - 1
    st_bits = SSM_STATE.bit_length() - 1
    blk_bits = na.bit_length() - 1
    gmask = na - 1
    r_w = lax.broadcasted_iota(i32, (CHUNK_W, 1), 0)
    c_w = lax.broadcasted_iota(i32, (1, CHUNK_W), 1)
    c_l = lax.broadcasted_iota(i32, (1, LANES), 1)
    grp_tok_r = (r_w >> grp_bits) & gmask
    grp_tok_c = (c_w >> grp_bits) & gmask
    grp_st_r = (r_w >> st_bits) & gmask
    grp_st_c = (c_w >> st_bits) & gmask

    lag = []
    for d in range(2):
        full = (lax.dot_general(x_scr[d, 0], c_scr[d, 0], nt, precision=hp, preferred_element_type=F32)
                - lax.dot_general(x_scr[d, 1], c_scr[d, 1], nt, precision=hp, preferred_element_type=F32))
        lag.append(jnp.where(grp_tok_r == (c_l >> grp_bits), full, 0.0))
    for s in range(t_):
        for t in range(t_):
            if t > s:
                blk = lag[0][(t - s) * LANES:(t - s + 1) * LANES]
            elif s > t:
                blk = lag[1][(s - t) * LANES:(s - t + 1) * LANES]
            else:
                blk = lag[0][:LANES] + lag[1][:LANES]
            mf_ref[0, s * LANES:(s + 1) * LANES, t * LANES:(t + 1) * LANES] = blk.astype(mf_ref.dtype)

    r_l = lax.broadcasted_iota(i32, (LANES, 1), 0)
    rep = (((r_l >> st_bits) == (c_w >> (st_bits + blk_bits)))
           & ((r_l & (SSM_STATE - 1)) == (c_w & (SSM_STATE - 1)))).astype(F32).astype(BF16)
    rep_t = (((c_l >> st_bits) == (r_w >> (st_bits + blk_bits)))
             & ((c_l & (SSM_STATE - 1)) == (r_w & (SSM_STATE - 1)))).astype(F32).astype(BF16)
    for d in range(2):
        e_full = _bdot(e_scr[d].astype(BF16), rep)
        f_full = lax.dot_general(rep_t, f_scr[d].astype(BF16), nt, preferred_element_type=F32)
        e_ref[d, 0] = jnp.where(grp_tok_r == grp_st_c, e_full, 0.0).astype(e_ref.dtype)
        mf_ref[0, (1 + d) * CHUNK_W:(2 + d) * CHUNK_W, :] = (
            jnp.where(grp_st_r == grp_tok_c, f_full, 0.0).astype(mf_ref.dtype))


def _s5_tables(a_re, a_im, bb_re, bb_im, c_re, c_im):
    nj, na = N_LANE_BLOCKS, GROUPS_PER_BLOCK
    grp = lambda r: pl.BlockSpec((2, na, r, SSM_STATE), lambda j: (0, j, 0, 0))
    dj = lambda r, c: pl.BlockSpec((2, 1, r, c), lambda j: (0, j, 0, 0))
    src = lambda: pltpu.VMEM((2, 2, CHUNK_W, SSM_STATE), F32)
    pair = lambda: pltpu.VMEM((2, CHUNK_W, 2 * SSM_STATE), F32)
    return pl.pallas_call(
        _s5_table_kernel,
        grid=(nj,),
        in_specs=[grp(1), grp(1), grp(SSM_GROUP), grp(SSM_GROUP), grp(SSM_GROUP), grp(SSM_GROUP)],
        out_specs=[pl.BlockSpec((1, 3 * CHUNK_W, CHUNK_W), lambda j: (j, 0, 0)),
                   dj(CHUNK_W, 2 * STATE_W),
                   pl.BlockSpec((2, 1, 2, na, 1, SSM_STATE), lambda j: (0, j, 0, 0, 0, 0))],
        out_shape=[jax.ShapeDtypeStruct((nj, 3 * CHUNK_W, CHUNK_W), BF16),
                   jax.ShapeDtypeStruct((2, nj, CHUNK_W, 2 * STATE_W), BF16),
                   jax.ShapeDtypeStruct((2, nj, 2, na, 1, SSM_STATE), F32)],
        scratch_shapes=[src(), pair(), pair(), pltpu.VMEM((2, 2, LANES, SSM_STATE), F32)],
        compiler_params=_params(1),
        name="s5_tables",
    )(a_re, a_im, bb_re, bb_im, c_re, c_im)


def _s5_state_kernel(uc_ref, ul_ref, e_ref, a_ref, sin_ref, sre_scr, sim_scr, *, nb):
    d = pl.program_id(1)
    r = pl.program_id(2)
    a_re = jnp.broadcast_to(a_ref[0, 0, 0], (nb, STATE_W))
    a_im = jnp.broadcast_to(a_ref[0, 0, 1], (nb, STATE_W))

    def run(u_ref, reverse, store):
        rows = u_ref.shape[1]
        part = min(rows, S5_STATE_PART)
        parts = list(range(rows // part))
        chunks = list(range(part // nb))
        if reverse:
            parts, chunks = parts[::-1], chunks[::-1]
        inj = [_bdot(u_ref[0, p * part:(p + 1) * part], e_ref[0, 0]) for p in parts]
        s_re, s_im = sre_scr[...], sim_scr[...]
        for c, p in zip(inj, parts):
            for k in chunks:
                if store:
                    row = p * part + k * nb
                    sin_ref[0, 0, row:row + nb, 0:STATE_W] = s_re.astype(sin_ref.dtype)
                    sin_ref[0, 0, row:row + nb, STATE_W:] = s_im.astype(sin_ref.dtype)
                c_re = c[k * nb:(k + 1) * nb, 0:STATE_W]
                c_im = c[k * nb:(k + 1) * nb, STATE_W:]
                s_re, s_im = s_re * a_re - s_im * a_im + c_re, s_re * a_im + s_im * a_re + c_im
        sre_scr[...] = s_re
        sim_scr[...] = s_im

    @pl.when(r == 0)
    def _():
        sre_scr[...] = jnp.zeros_like(sre_scr)
        sim_scr[...] = jnp.zeros_like(sim_scr)

    for reverse in (False, True):
        @pl.when((r == 0) & (d == int(reverse)))
        def _():
            run(uc_ref, reverse, False)

        @pl.when((r > 0) & (d == int(reverse)))
        def _():
            run(ul_ref, reverse, True)


def _s5_states(u4c, u4l, e_tab, a_tab, nb):
    rows_c = u4c.shape[1]
    rows = S5_STATE_ROWS
    n_tiles = u4l.shape[1] // rows

    def lat_tile(d, r):
        t = jnp.maximum(r - 1, 0)
        return jnp.where(d == 0, t, n_tiles - 1 - t)

    return pl.pallas_call(
        functools.partial(_s5_state_kernel, nb=nb),
        grid=(N_LANE_BLOCKS, 2, n_tiles + 1),
        in_specs=[pl.BlockSpec((1, rows_c, CHUNK_W), lambda j, d, r: (j, 0, 0)),
                  pl.BlockSpec((1, rows, CHUNK_W), lambda j, d, r: (j, lat_tile(d, r), 0)),
                  pl.BlockSpec((1, 1, CHUNK_W, 2 * STATE_W), lambda j, d, r: (d, j, 0, 0)),
                  pl.BlockSpec((1, 1, 2, 1, STATE_W), lambda j, d, r: (d, j, 0, 0, 0))],
        out_specs=pl.BlockSpec((1, 1, rows, 2 * STATE_W), lambda j, d, r: (d, j, lat_tile(d, r), 0)),
        out_shape=jax.ShapeDtypeStruct((2, N_LANE_BLOCKS, u4l.shape[1], 2 * STATE_W), BF16),
        scratch_shapes=[pltpu.VMEM((nb, STATE_W), F32), pltpu.VMEM((nb, STATE_W), F32)],
        compiler_params=_params(3),
        name="s5_states",
    )(u4c, u4l, e_tab, a_tab)


def _s5_out_kernel(ul_ref, sin_ref, mf_ref, d_ref, y_ref):
    u = ul_ref[0]
    lhs = jnp.concatenate([u, sin_ref[0, 0], sin_ref[1, 0]], axis=-1)
    y_ref[0] = u.astype(F32) * d_ref[0] + _bdot(lhs, mf_ref[0])


def _s5_outputs(u4l, sin, mf_tab, d_tab):
    rows = S5_OUT_ROWS
    return pl.pallas_call(
        _s5_out_kernel,
        grid=(N_LANE_BLOCKS, u4l.shape[1] // rows),
        in_specs=[pl.BlockSpec((1, rows, CHUNK_W), lambda j, r: (j, r, 0)),
                  pl.BlockSpec((2, 1, rows, 2 * STATE_W), lambda j, r: (0, j, r, 0)),
                  pl.BlockSpec((1, 3 * CHUNK_W, CHUNK_W), lambda j, r: (j, 0, 0)),
                  pl.BlockSpec((1, 1, CHUNK_W), lambda j, r: (j, 0, 0))],
        out_specs=pl.BlockSpec((1, rows, CHUNK_W), lambda j, r: (j, r, 0)),
        out_shape=jax.ShapeDtypeStruct(u4l.shape, F32),
        compiler_params=_params(2),
        name="s5_outputs",
    )(u4l, sin, mf_tab, d_tab)


def _merge_kernel(attn_ref, y4_ref, g_ref, x_ref, g1_ref, sh2_ref, sc2_ref, n2_ref,
                  sel_ref, wab_ref, wglu_ref, wout_ref, x1_ref, h2_ref):
    nb, tokens_all, _ = x_ref.shape
    tokens = tokens_all // MERGE_SPLIT
    rows = nb * tokens
    crow = tokens // CHUNK * nb

    def branch_dots(i):
        ts = slice(i * tokens, (i + 1) * tokens)
        act = [_gelu_tanh(y4_ref[j, i * crow:(i + 1) * crow]).astype(BF16)
               for j in range(N_LANE_BLOCKS)]
        act = jnp.concatenate(
            [jnp.concatenate([act[j][:, s * LANES:(s + 1) * LANES] for s in range(CHUNK)], axis=0)
             for j in range(N_LANE_BLOCKS)], axis=-1)
        act = _bdot(sel_ref[...], act).astype(BF16)
        p_attn = _bdot(attn_ref[:, ts].reshape(rows, ATTN_DIM), wab_ref[...])
        return p_attn, _bdot(act, wglu_ref[...])

    def mix_dot(i, p_attn, glu):
        ts = slice(i * tokens, (i + 1) * tokens)
        p_ssm = glu[:, :D_MODEL] * _sigmoid(glu[:, D_MODEL:])
        g = g_ref[:, ts].reshape(rows, 2 * D_MODEL).astype(F32)
        mix = _sigmoid(g[:, :D_MODEL]) * p_attn + _sigmoid(g[:, D_MODEL:]) * p_ssm
        return _bdot(mix.astype(BF16), wout_ref[...]).reshape(nb, tokens, D_MODEL)

    def finish(i, x_mix):
        ts = slice(i * tokens, (i + 1) * tokens)
        x1 = x_ref[:, ts] + g1_ref[...] * x_mix
        x1_ref[:, ts] = x1
        h2 = _rms(x1) * n2_ref[...]
        h2_ref[:, ts] = (h2 * (1.0 + sc2_ref[...]) + sh2_ref[...]).astype(h2_ref.dtype)

    dots = [None] * MERGE_SPLIT
    mixed = [None] * MERGE_SPLIT
    dots[0] = branch_dots(0)
    for i in range(MERGE_SPLIT):
        if i + 1 < MERGE_SPLIT:
            dots[i + 1] = branch_dots(i + 1)
        mixed[i] = mix_dot(i, *dots[i])
        if i > 0:
            finish(i - 1, mixed[i - 1])
    finish(MERGE_SPLIT - 1, mixed[MERGE_SPLIT - 1])


def _merge(attn, y4, g, x, g1, sh2, sc2, n2, wab, wglu, wout):
    b, l, _ = x.shape
    tt = TOK_TILE
    tok = lambda w: pl.BlockSpec((b, tt, w), lambda i: (0, i, 0))
    per_b = _const_spec((b, 1, D_MODEL))
    y4_spec = pl.BlockSpec((N_LANE_BLOCKS, tt // CHUNK * b, CHUNK_W), lambda i: (0, i, 0))
    sel = _chunk_row_select(b, tt // MERGE_SPLIT, inverse=True)
    return pl.pallas_call(
        _merge_kernel,
        grid=(l // tt,),
        in_specs=[tok(ATTN_DIM), y4_spec, tok(2 * D_MODEL), tok(D_MODEL),
                  per_b, per_b, per_b, _const_spec((1, D_MODEL)), _const_spec(sel.shape),
                  _const_spec(wab.shape), _const_spec(wglu.shape), _const_spec(wout.shape)],
        out_specs=[tok(D_MODEL), tok(D_MODEL)],
        out_shape=[jax.ShapeDtypeStruct((b, l, D_MODEL), F32),
                   jax.ShapeDtypeStruct((b, l, D_MODEL), BF16)],
        compiler_params=_params(1),
        name="merge",
    )(attn, y4, g, x, g1, sh2, sc2, n2, sel, wab, wglu, wout)


def _ffn_row_select(tm):
    seg_len = tm // SUB
    tile = np.zeros((tm, tm), np.float32)
    for seg in range(SUB):
        for i in range(seg_len):
            tile[SUB * i + seg, seg * seg_len + i] = 1.0
    halo = np.zeros((SUB, 2 * HALO), np.float32)
    halo[0, HALO - 1] = 1.0
    halo[1, HALO] = 1.0
    return jnp.asarray(tile, dtype=BF16), jnp.asarray(halo, dtype=BF16)


def _ffn_kernel(sel_ref, selh_ref, hp_ref, h_ref, hn_ref, x1_ref, g2_ref, wup_ref, cw_ref, cb_ref,
                wd_ref, fw_ref, o_ref, acc_scr, *, tm):
    j = pl.program_id(1)
    keep_prev = (j > 0).astype(F32)
    keep_next = (j < pl.num_programs(1) - 1).astype(F32)
    halo = jnp.concatenate([hp_ref[0], hn_ref[0]], axis=0)
    lhs = jnp.concatenate([_bdot(selh_ref[...], halo), _bdot(sel_ref[...], h_ref[0])],
                          axis=0).astype(BF16)
    sub = lax.broadcasted_iota(jnp.int32, (SUB, 1), 0)

    def up(f):
        return (_bdot(lhs, wup_ref[:, f * FF_CHUNK:(f + 1) * FF_CHUNK]),
                _bdot(lhs, wup_ref[:, D_FF + f * FF_CHUNK:D_FF + (f + 1) * FF_CHUNK]))

    def conv(zall, col):
        z = zall[SUB:]
        before = jnp.where(sub == 0, zall[0:1] * keep_prev, pltpu.roll(z[tm - SUB:], 1, 0))
        after = jnp.where(sub == SUB - 1, zall[1:2] * keep_next, pltpu.roll(z[:SUB], SUB - 1, 0))
        z_prev = jnp.concatenate([before, z[:tm - SUB]], axis=0)
        z_next = jnp.concatenate([z[SUB:], after], axis=0)
        cw = cw_ref[:, col:col + FF_CHUNK]
        return z_prev * cw[0:1] + z * cw[1:2] + z_next * cw[2:3] + cb_ref[:, col:col + FF_CHUNK]

    acts = []
    cur = up(0)
    for f in range(N_FF_CHUNKS):
        nxt = up(f + 1) if f + 1 < N_FF_CHUNKS else None
        val = conv(cur[0], f * FF_CHUNK)
        gate = conv(cur[1], D_FF + f * FF_CHUNK)
        acts.append((gate * _sigmoid(gate) * val).astype(BF16))
        cur = nxt
    acc = _bdot(jnp.concatenate(acts, axis=-1), wd_ref[...])
    seg_len = tm // SUB
    for k in range(D_MODEL // LANES):
        acc_scr[k] = acc[:, k * LANES:(k + 1) * LANES]
        for seg in range(SUB):
            o_ref[0, seg * seg_len:(seg + 1) * seg_len, k * LANES:(k + 1) * LANES] = (
                acc_scr[k, pl.ds(seg, seg_len, stride=SUB), :])
    x2 = x1_ref[0] + g2_ref[0] * o_ref[0]
    o_ref[0] = _rms(x2) * fw_ref[...]


def _conv_ffn(h2, x1, g2, wup, cw, cb, wd, fw):
    b, l, _ = x1.shape
    tm = TM_FFN
    nh = tm // HALO
    last = l // HALO - 1
    tok = lambda: pl.BlockSpec((1, tm, D_MODEL), lambda i, j: (i, j, 0))
    prev = pl.BlockSpec((1, HALO, D_MODEL), lambda i, j: (i, jnp.maximum(j * nh - 1, 0), 0))
    nxt = pl.BlockSpec((1, HALO, D_MODEL), lambda i, j: (i, jnp.minimum((j + 1) * nh, last), 0))
    per_b = pl.BlockSpec((1, 1, D_MODEL), lambda i, j: (i, 0, 0))
    once = lambda shape: pl.BlockSpec(shape, lambda *_: (0,) * len(shape),
                                      pipeline_mode=pl.Buffered(1))
    sel, selh = _ffn_row_select(tm)
    return pl.pallas_call(
        functools.partial(_ffn_kernel, tm=tm),
        grid=(b, l // tm),
        in_specs=[once(sel.shape), once(selh.shape), prev, tok(), nxt, tok(), per_b,
                  once(wup.shape), once(cw.shape), once(cb.shape), once(wd.shape),
                  _const_spec((1, D_MODEL))],
        out_specs=tok(),
        out_shape=jax.ShapeDtypeStruct((b, l, D_MODEL), F32),
        scratch_shapes=[pltpu.VMEM((D_MODEL // LANES, tm, LANES), F32)],
        compiler_params=_params(2),
        name="conv_ffn",
    )(sel, selh, h2, h2, h2, x1, g2, wup, cw, cb, wd, fw)


def _rope_tables(l):
    rows = jnp.repeat(jnp.arange(l // GRID_W, dtype=F32), GRID_W)
    cols = jnp.tile(jnp.arange(GRID_W, dtype=F32), l // GRID_W)
    inv_freq = ROPE_THETA ** (-jnp.arange(0, ROPE_AXIS_DIM, 2, dtype=F32) / ROPE_AXIS_DIM)
    ang = jnp.concatenate([rows[:, None] * inv_freq, cols[:, None] * inv_freq], axis=-1)
    cos = jnp.repeat(jnp.cos(ang), 2, axis=-1)
    sin = jnp.repeat(jnp.sin(ang), 2, axis=-1)
    odd = (jnp.arange(HEAD_DIM) % 2 == 1)[None, :]
    return cos, jnp.where(odd, sin, 0.0), jnp.where(odd, 0.0, -sin)


def kernel(x, c, ctx, c_ctx, w_mod, b_mod, norm1_w, norm2_w, w_in, q_norm_w, k_norm_w, w_attn_br,
           ssm_lambda_re, ssm_lambda_im, ssm_log_dt, ssm_b_re, ssm_b_im, ssm_c_re, ssm_c_im, ssm_d,
           w_glu, w_out, w_up, conv_w, conv_b, w_down, final_norm_w):
    b, l, d = x.shape
    lc = ctx.shape[1]
    assert w_mod.shape[0] == 1 and d == D_MODEL and l % TOK_TILE == 0 and lc % TOK_TILE == 0
    assert (l // CHUNK * b) % S5_STATE_ROWS == 0 and (l // CHUNK * b) % S5_OUT_ROWS == 0
    layer = 0

    pad = (-(b + 1)) % 16
    c_rows = jnp.concatenate([c, c_ctx[None, :], jnp.zeros((pad, d), F32)], axis=0)
    mod = _modulation(c_rows, w_mod[layer], b_mod[layer])
    sh1, sc1, g1, sh2, sc2, g2 = [m[:b, None, :] for m in jnp.split(mod, N_MOD, axis=-1)]
    csh1, csc1 = mod[b:b + 1, None, :d], mod[b:b + 1, None, d:2 * d]

    w_in_b = w_in[layer].astype(BF16)
    qn = (q_norm_w[layer] * (1.0 / math.sqrt(HEAD_DIM))).reshape(1, HEAD_DIM)
    kn = k_norm_w[layer].reshape(1, HEAD_DIM)
    n1 = norm1_w[layer].reshape(1, d)

    no_rope = (jnp.ones((lc, HEAD_DIM), F32),) + (jnp.zeros((lc, HEAD_DIM), F32),) * 2
    q, k, v, u4l, g = _inproj(x, sh1, sc1, n1, w_in_b, qn, kn, _rope_tables(l), latent=True)
    kc, vc, u4c = _inproj(ctx, csh1, csc1, n1, w_in_b, qn, kn, no_rope, latent=False)

    attn = _attention(q, kc, k, vc, v)

    disc = _s5_discretise(ssm_lambda_re[layer], ssm_lambda_im[layer], ssm_log_dt[layer],
                          ssm_b_re[layer], ssm_b_im[layer])
    mf_tab, e_tab, a_pow = _s5_tables(*disc, ssm_c_re[layer], ssm_c_im[layer])
    a_tab = a_pow.reshape(2, N_LANE_BLOCKS, 2, 1, STATE_W)
    d_tab = jnp.tile(ssm_d[layer].reshape(N_LANE_BLOCKS, 1, LANES), (1, CHUNK, 1))
    d_tab = d_tab.reshape(N_LANE_BLOCKS, 1, CHUNK_W)
    sin = _s5_states(u4c, u4l, e_tab, a_tab, b)
    y4 = _s5_outputs(u4l, sin, mf_tab, d_tab)

    x1, h2 = _merge(attn, y4, g, x, g1, sh2, sc2, norm2_w[layer].reshape(1, d),
                    w_attn_br[layer].astype(BF16), w_glu[layer].astype(BF16),
                    w_out[layer].astype(BF16))

    return _conv_ffn(h2, x1, g2, w_up[layer].astype(BF16), conv_w[layer],
                     conv_b[layer].reshape(1, 2 * D_FF), w_down[layer].astype(BF16),
                     final_norm_w.reshape(1, d))
```
